```python
import math
import jax, jax.numpy as jnp
from jax import lax
import numpy as np

D_MODEL = 1024
BATCH = 4
SEQ = 4096
DEPTH = 4
DEC_BATCH = 32
DEC_SEQ = 8
PAST_LEN = 8192
PAGE_SIZE = 128

N_MIXERS = 4
QBLK = 128
ROPE_THETA = 10000.0
NORM_EPS = 1e-6
MASK_VALUE = -1e30

FOX_HEADS = 16
FOX_HDIM = D_MODEL // FOX_HEADS
FORGET_BIAS = 3.0
SB_HEADS = 16
SB_HDIM = D_MODEL // SB_HEADS
NSA_HEADS = 16
NSA_KV_HEADS = 4
NSA_HDIM = D_MODEL // NSA_HEADS
CMP_STRIDE = 16
CMP_BLK = 2 * CMP_STRIDE
CMP_HID = 4 * NSA_HDIM
SEL_BLK = 64
N_SELECT = 16
WINDOW = 512
SEL_FORCE = 1e4
DIFF_HEADS = 8
DIFF_HDIM = D_MODEL // (2 * DIFF_HEADS)
MEM_LEN = 256
X_HEADS = 4
X_HDIM = 128
D_FF = 2816
CONV_W = 3

kernel_name = 'hybrid_fox_sb_nsa_diff_step'


def rmsnorm(x, g):
    xf = x.astype(jnp.float32)
    y = xf * lax.rsqrt(jnp.mean(xf * xf, axis=-1, keepdims=True) + NORM_EPS)
    return (y * g.astype(jnp.float32)).astype(x.dtype)


def rope(x, pos):
    half = x.shape[-1] // 2
    inv = ROPE_THETA ** (-jnp.arange(half, dtype=jnp.float32) / half)
    ang = pos.astype(jnp.float32)[:, None] * inv[None, :]
    cos = jnp.cos(ang)[None, :, None, :]
    sin = jnp.sin(ang)[None, :, None, :]
    xf = x.astype(jnp.float32)
    x1, x2 = xf[..., :half], xf[..., half:]
    return jnp.concatenate([x1 * cos - x2 * sin, x2 * cos + x1 * sin], axis=-1).astype(x.dtype)


def masked_softmax(s, mask):
    s = jnp.where(mask, s.astype(jnp.float32), MASK_VALUE)
    p = jnp.exp(s - jnp.max(s, axis=-1, keepdims=True)) * mask
    return p / jnp.maximum(jnp.sum(p, axis=-1, keepdims=True), 1e-30)


def split_heads(x, n, d):
    return x.reshape(x.shape[0], x.shape[1], n, d)


def gather_pages(cache, page_table):
    g = cache[page_table]
    return g.reshape(g.shape[0], g.shape[1] * g.shape[2], *g.shape[3:])


def sweep_queries(block_fn, tq):
    if tq > QBLK and tq % QBLK == 0:
        out = lax.map(lambda i: block_fn(i * QBLK, QBLK), jnp.arange(tq // QBLK))
        out = jnp.moveaxis(out, 0, 1)
        return out.reshape(out.shape[0], tq, *out.shape[3:])
    return block_fn(0, tq)


def fox_mixer(h, pos0, past, w_in, b_f, w_out):
    B, T, _ = h.shape
    hd = FOX_HEADS * FOX_HDIM
    proj = h @ w_in
    q = split_heads(proj[..., :hd], FOX_HEADS, FOX_HDIM)
    k = split_heads(proj[..., hd:2 * hd], FOX_HEADS, FOX_HDIM)
    v = split_heads(proj[..., 2 * hd:3 * hd], FOX_HEADS, FOX_HDIM)
    logf = jax.nn.log_sigmoid((proj[..., 3 * hd:] + b_f).astype(jnp.float32))
    if past is None:
        k_all, v_all, lf_all = k, v, logf
    else:
        k_all = jnp.concatenate([past[0], k], axis=1)
        v_all = jnp.concatenate([past[1], v], axis=1)
        lf_all = jnp.concatenate([past[2].astype(jnp.float32), logf], axis=1)
    tk = k_all.shape[1]
    c = jnp.transpose(lax.cumsum(lf_all, axis=1), (0, 2, 1))
    k_pos = jnp.arange(tk)
    scale = FOX_HDIM ** -0.5

    def block(qs, qb):
        q_blk = lax.dynamic_slice_in_dim(q, qs, qb, 1)
        c_q = lax.dynamic_slice_in_dim(c, pos0 + qs, qb, 2)
        q_pos = pos0 + qs + jnp.arange(qb)
        s = jnp.einsum('bthd,bshd->bhts', q_blk, k_all).astype(jnp.float32) * scale
        s = s + c_q[..., :, None] - c[..., None, :]
        p = masked_softmax(s, k_pos[None, :] <= q_pos[:, None])
        return jnp.einsum('bhts,bshd->bthd', p.astype(v_all.dtype), v_all)

    o = sweep_queries(block, T)
    return o.reshape(B, T, hd) @ w_out, (k, v, logf)


def sb_mixer(h, pos0, past, w_in, w_out):
    B, T, _ = h.shape
    hd = SB_HEADS * SB_HDIM
    proj = h @ w_in
    q = split_heads(proj[..., :hd], SB_HEADS, SB_HDIM)
    k = split_heads(proj[..., hd:2 * hd], SB_HEADS, SB_HDIM)
    v = split_heads(proj[..., 2 * hd:], SB_HEADS, SB_HDIM)
    if past is None:
        k_all, v_all = k, v
    else:
        k_all = jnp.concatenate([past[0], k], axis=1)
        v_all = jnp.concatenate([past[1], v], axis=1)
    k_pos = jnp.arange(k_all.shape[1])
    scale = SB_HDIM ** -0.5

    def block(qs, qb):
        q_pos = pos0 + qs + jnp.arange(qb)
        z = jnp.einsum('bthd,bshd->bhts', lax.dynamic_slice_in_dim(q, qs, qb, 1), k_all).astype(jnp.float32) * scale
        mask = k_pos[None, :] < q_pos[:, None]
        log_beta = jax.nn.log_sigmoid(z)
        log_rest = jnp.where(mask, jax.nn.log_sigmoid(-z), 0.0)
        after = lax.cumsum(log_rest, axis=3, reverse=True) - log_rest
        a = jnp.where(mask, jnp.exp(log_beta + after), 0.0)
        return jnp.einsum('bhts,bshd->bthd', a.astype(v_all.dtype), v_all)

    o = sweep_queries(block, T)
    return o.reshape(B, T, hd) @ w_out, (k, v)


def compress(k_rows, pe, w1, b1, w2):
    B, Tp, G, d = k_rows.shape
    chunks = k_rows.reshape(B, Tp // CMP_STRIDE, CMP_STRIDE, G, d)
    h_first = jnp.einsum('bcrgd,rdh->bcgh', chunks, w1[:CMP_STRIDE])
    h_second = jnp.einsum('bcrgd,rdh->bcgh', chunks, w1[CMP_STRIDE:])
    bias = b1 + jnp.einsum('rd,rdh->h', pe, w1)
    hid = jax.nn.gelu(h_first[:, :-1] + h_second[:, 1:] + bias)
    return jnp.einsum('bcgh,hd->bcgd', hid, w2)


def nsa_mixer(h, pos0, past, win_buf, w_in, cmp_pe, cmp_w1, cmp_b1, cmp_w2, w_out):
    B, T, _ = h.shape
    H, G, d = NSA_HEADS, NSA_KV_HEADS, NSA_HDIM
    R = H // G
    hq, hk = H * d, G * d
    proj = h @ w_in
    pos = pos0 + jnp.arange(T)
    q = split_heads(proj[..., :hq], H, d)
    kc, vc, ks, vs, kw, vw = [split_heads(proj[..., hq + j * hk:hq + (j + 1) * hk], G, d) for j in range(6)]
    ks = rope(ks, pos)
    kw = rope(kw, pos)
    q_cmp = q.reshape(B, T, G, R, d)
    q_rot = rope(q, pos).reshape(B, T, G, R, d)
    gates = jax.nn.sigmoid(proj[..., hq + 6 * hk:].astype(jnp.float32)).reshape(B, T, 3, G, R)
    if past is None:
        kc_all, vc_all, ks_all, vs_all = kc, vc, ks, vs
        zpad = jnp.zeros((B, WINDOW, G, d), kw.dtype)
        kw_all = jnp.concatenate([zpad, kw], axis=1)
        vw_all = jnp.concatenate([zpad, vw], axis=1)
    else:
        kc_all = jnp.concatenate([past[0], kc], axis=1)
        vc_all = jnp.concatenate([past[1], vc], axis=1)
        ks_all = jnp.concatenate([past[2], ks], axis=1)
        vs_all = jnp.concatenate([past[3], vs], axis=1)
        front = WINDOW - win_buf[0].shape[1]
        wpad = ((0, 0), (front, 0), (0, 0), (0, 0))
        kw_all = jnp.concatenate([jnp.pad(win_buf[0], wpad), kw], axis=1)
        vw_all = jnp.concatenate([jnp.pad(win_buf[1], wpad), vw], axis=1)
    wlen = min(WINDOW, pos0 + T)
    new_win = (kw_all[:, -wlen:], vw_all[:, -wlen:])

    tk = pos0 + T
    tp = -(-tk // SEL_BLK) * SEL_BLK
    tpad = ((0, 0), (0, tp - tk), (0, 0), (0, 0))
    k_cmp = compress(jnp.pad(kc_all, tpad), cmp_pe[0], cmp_w1[0], cmp_b1[0], cmp_w2[0])
    v_cmp = compress(jnp.pad(vc_all, tpad), cmp_pe[1], cmp_w1[1], cmp_b1[1], cmp_w2[1])
    n_cmp = k_cmp.shape[1]
    n_sel = tp // SEL_BLK
    cmp_end = jnp.arange(n_cmp) * CMP_STRIDE + CMP_BLK - 1
    c_start = jnp.arange(n_cmp)[:, None] * CMP_STRIDE
    s_start = jnp.arange(n_sel)[None, :] * SEL_BLK
    overlap = ((c_start < s_start + SEL_BLK) & (c_start + CMP_BLK > s_start)).astype(jnp.float32)
    k_blk = jnp.pad(ks_all, tpad).reshape(B, n_sel, SEL_BLK, G, d).transpose(0, 3, 1, 2, 4)
    v_blk = jnp.pad(vs_all, tpad).reshape(B, n_sel, SEL_BLK, G, d).transpose(0, 3, 1, 2, 4)
    n_top = min(N_SELECT, n_sel)
    blk_ids = jnp.arange(n_sel)
    gather_blocks = jax.vmap(jax.vmap(lambda blocks, ids: blocks[ids]))
    scale = d ** -0.5

    def block(qs, qb):
        q_pos = pos0 + qs + jnp.arange(qb)
        qc = lax.dynamic_slice_in_dim(q_cmp, qs, qb, 1)
        qr = lax.dynamic_slice_in_dim(q_rot, qs, qb, 1)
        gb = lax.dynamic_slice_in_dim(gates, qs, qb, 1)
        s_c = jnp.einsum('btgrd,bcgd->bgrtc', qc, k_cmp).astype(jnp.float32) * scale
        p_c = masked_softmax(s_c, cmp_end[None, :] <= q_pos[:, None])
        o_c = jnp.einsum('bgrtc,bcgd->btgrd', p_c.astype(v_cmp.dtype), v_cmp)
        imp = jnp.einsum('bgrtc,cj->bgtj', p_c, overlap)
        cur = q_pos // SEL_BLK
        forced = (blk_ids[None, :] == 0) | (blk_ids[None, :] == cur[:, None]) | (blk_ids[None, :] == cur[:, None] - 1)
        valid = blk_ids[None, :] * SEL_BLK <= q_pos[:, None]
        score = jnp.where(forced, SEL_FORCE, jnp.where(valid, imp, -1.0))
        _, idx = lax.top_k(score, n_top)
        k_sel = gather_blocks(k_blk, idx).reshape(B, G, qb, n_top * SEL_BLK, d)
        v_sel = gather_blocks(v_blk, idx).reshape(B, G, qb, n_top * SEL_BLK, d)
        sel_pos = (idx[..., None] * SEL_BLK + jnp.arange(SEL_BLK)).reshape(B, G, qb, n_top * SEL_BLK)
        s_s = jnp.einsum('btgrd,bgtmd->bgrtm', qr, k_sel).astype(jnp.float32) * scale
        p_s = masked_softmax(s_s, (sel_pos <= q_pos[:, None])[:, :, None])
        o_s = jnp.einsum('bgrtm,bgtmd->btgrd', p_s.astype(v_sel.dtype), v_sel)
        k_win = lax.dynamic_slice_in_dim(kw_all, qs, WINDOW + qb, 1)
        v_win = lax.dynamic_slice_in_dim(vw_all, qs, WINDOW + qb, 1)
        w_pos = pos0 - WINDOW + qs + jnp.arange(WINDOW + qb)
        dist = q_pos[:, None] - w_pos[None, :]
        m_w = (dist >= 0) & (dist < WINDOW) & (w_pos[None, :] >= 0)
        s_w = jnp.einsum('btgrd,bsgd->bgrts', qr, k_win).astype(jnp.float32) * scale
        p_w = masked_softmax(s_w, m_w)
        o_w = jnp.einsum('bgrts,bsgd->btgrd', p_w.astype(v_win.dtype), v_win)
        o = gb[:, :, 0, :, :, None] * o_c + gb[:, :, 1, :, :, None] * o_s + gb[:, :, 2, :, :, None] * o_w
        return o.reshape(B, qb, H * d).astype(h.dtype)

    o = sweep_queries(block, T)
    return o @ w_out, (kc, vc, ks, vs), new_win


def diff_mixer(h, pos0, past, w_in, lq1, lk1, lq2, lk2, subln, w_out, lambda_init):
    B, T, _ = h.shape
    Hd, d = DIFF_HEADS, DIFF_HDIM
    hq = 2 * Hd * d
    proj = h @ w_in
    pos = pos0 + jnp.arange(T)
    q = rope(split_heads(proj[..., :hq], 2 * Hd, d), pos)
    k = rope(split_heads(proj[..., hq:2 * hq], 2 * Hd, d), pos)
    v = split_heads(proj[..., 2 * hq:], Hd, 2 * d)
    if past is None:
        k_all, v_all = k, v
    else:
        k_all = jnp.concatenate([past[0], k], axis=1)
        v_all = jnp.concatenate([past[1], v], axis=1)
    tk = k_all.shape[1]
    k_pos = jnp.arange(tk)
    f32 = jnp.float32
    lam = (jnp.exp(jnp.sum(lq1.astype(f32) * lk1.astype(f32)))
           - jnp.exp(jnp.sum(lq2.astype(f32) * lk2.astype(f32))) + lambda_init)
    scale = d ** -0.5

    def block(qs, qb):
        q_pos = pos0 + qs + jnp.arange(qb)
        s = jnp.einsum('bthd,bshd->bhts', lax.dynamic_slice_in_dim(q, qs, qb, 1), k_all).astype(f32) * scale
        p = masked_softmax(s, k_pos[None, :] <= q_pos[:, None]).reshape(B, Hd, 2, qb, tk)
        a = p[:, :, 0] - lam * p[:, :, 1]
        o = jnp.einsum('bhts,bshe->bthe', a.astype(v_all.dtype), v_all)
        return rmsnorm(o, subln) * (1.0 - lambda_init)

    o = sweep_queries(block, T)
    return o.reshape(B, T, Hd * 2 * d) @ w_out, (k, v)


def memory_kv(mem, mem_norm, w_kv):
    mn = rmsnorm(mem[None], mem_norm[:, None, None, :])
    kv = jnp.einsum('lbmd,lde->lbme', mn, w_kv)
    L, B, M, _ = kv.shape
    xk = kv[..., :X_HEADS * X_HDIM].reshape(L, B, M, X_HEADS, X_HDIM)
    xv = kv[..., X_HEADS * X_HDIM:].reshape(L, B, M, X_HEADS, X_HDIM)
    return xk, xv


def cross_attention(h, mk, mv, w_q, w_o):
    B, T, _ = h.shape
    q = split_heads(h @ w_q, X_HEADS, X_HDIM)
    s = jnp.einsum('bthd,bmhd->bhtm', q, mk).astype(jnp.float32) * X_HDIM ** -0.5
    p = jax.nn.softmax(s, axis=-1)
    o = jnp.einsum('bhtm,bmhd->bthd', p.astype(mv.dtype), mv)
    return o.reshape(B, T, X_HEADS * X_HDIM) @ w_o


def conv_ffn(h, conv_state, w_up, conv_w, conv_b, w_down):
    B, T, _ = h.shape
    u = h @ w_up
    if conv_state is None:
        prev = jnp.zeros((B, CONV_W - 1, u.shape[-1]), u.dtype)
    else:
        prev = conv_state.astype(u.dtype)
    ext = jnp.concatenate([prev, u], axis=1)
    c = conv_b
    for j in range(CONV_W):
        c = c + conv_w[j] * ext[:, j:j + T]
    val, gate = c[..., :D_FF], c[..., D_FF:]
    return (jax.nn.silu(gate) * val) @ w_down, ext[:, -(CONV_W - 1):]


def trunk(x, pos0, mem_k, mem_v, conv_state, cache, page_table, P):
    new = {}
    conv_rows = []
    for i in range(DEPTH):
        kind = i % N_MIXERS
        h = rmsnorm(x, P['norm_mix'][i])
        if kind == 0:
            past = None if cache is None else tuple(gather_pages(cache[n], page_table) for n in ('fox_k', 'fox_v', 'fox_logf'))
            out, rows = fox_mixer(h, pos0, past, P['fox_w_in'], P['fox_b_f'], P['fox_w_out'])
            new['fox_k'], new['fox_v'], new['fox_logf'] = rows
        elif kind == 1:
            past = None if cache is None else tuple(gather_pages(cache[n], page_table) for n in ('sb_k', 'sb_v'))
            out, rows = sb_mixer(h, pos0, past, P['sb_w_in'], P['sb_w_out'])
            new['sb_k'], new['sb_v'] = rows
        elif kind == 2:
            past = None if cache is None else tuple(gather_pages(cache[n], page_table) for n in ('nsa_kcmp', 'nsa_vcmp', 'nsa_ksel', 'nsa_vsel'))
            win = None if cache is None else (cache['nsa_kwin'], cache['nsa_vwin'])
            out, rows, win_new = nsa_mixer(h, pos0, past, win, P['nsa_w_in'], P['nsa_cmp_pe'], P['nsa_cmp_w1'],
                                           P['nsa_cmp_b1'], P['nsa_cmp_w2'], P['nsa_w_out'])
            new['nsa_kcmp'], new['nsa_vcmp'], new['nsa_ksel'], new['nsa_vsel'] = rows
            new['nsa_kwin'], new['nsa_vwin'] = win_new
        else:
            past = None if cache is None else tuple(gather_pages(cache[n], page_table) for n in ('diff_k', 'diff_v'))
            lambda_init = 0.8 - 0.6 * math.exp(-0.3 * i)
            out, rows = diff_mixer(h, pos0, past, P['diff_w_in'], P['diff_lq1'], P['diff_lk1'], P['diff_lq2'],
                                   P['diff_lk2'], P['diff_subln'], P['diff_w_out'], lambda_init)
            new['diff_k'], new['diff_v'] = rows
        x = x + out
        h = rmsnorm(x, P['norm_mem'][i])
        x = x + cross_attention(h, mem_k[i], mem_v[i], P['x_w_q'][i], P['x_w_o'][i])
        h = rmsnorm(x, P['norm_ffn'][i])
        out, conv_new = conv_ffn(h, None if conv_state is None else conv_state[i], P['ffn_w_up'][i],
                                 P['ffn_conv_w'][i], P['ffn_conv_b'][i], P['ffn_w_down'][i])
        conv_rows.append(conv_new)
        x = x + out
    new['ffn_conv'] = jnp.stack(conv_rows)
    return rmsnorm(x, P['norm_final']), new


def setup_inputs(seed: int = 0) -> dict:
    key = jax.random.key(seed)
    keys = iter(jax.random.split(key, 64))

    def nrm(shape, scale=1.0):
        return jax.random.normal(next(keys), shape, jnp.float32) * scale

    def gain(shape):
        return 1.0 + nrm(shape, 0.01)

    n_pages = PAST_LEN // PAGE_SIZE
    n_used = DEC_BATCH * n_pages
    n_pool = n_used + max(1, n_used // 4)
    page_table = jax.random.permutation(next(keys), n_pool)[:n_used].reshape(DEC_BATCH, n_pages).astype(jnp.int32)
    wbuf = min(WINDOW, PAST_LEN)
    G, dn = NSA_KV_HEADS, NSA_HDIM
    D = D_MODEL
    return {
        'x_prompt': nrm((BATCH, SEQ, D)),
        'x_sample': nrm((DEC_BATCH, DEC_SEQ, D)),
        'cache_fox_k': nrm((n_pool, PAGE_SIZE, FOX_HEADS, FOX_HDIM)),
        'cache_fox_v': nrm((n_pool, PAGE_SIZE, FOX_HEADS, FOX_HDIM)),
        'cache_fox_logf': jax.nn.log_sigmoid(FORGET_BIAS + nrm((n_pool, PAGE_SIZE, FOX_HEADS))),
        'cache_sb_k': nrm((n_pool, PAGE_SIZE, SB_HEADS, SB_HDIM)),
        'cache_sb_v': nrm((n_pool, PAGE_SIZE, SB_HEADS, SB_HDIM)),
        'cache_nsa_kcmp': nrm((n_pool, PAGE_SIZE, G, dn)),
        'cache_nsa_vcmp': nrm((n_pool, PAGE_SIZE, G, dn)),
        'cache_nsa_ksel': nrm((n_pool, PAGE_SIZE, G, dn)),
        'cache_nsa_vsel': nrm((n_pool, PAGE_SIZE, G, dn)),
        'state_nsa_kwin': nrm((DEC_BATCH, wbuf, G, dn)),
        'state_nsa_vwin': nrm((DEC_BATCH, wbuf, G, dn)),
        'cache_diff_k': nrm((n_pool, PAGE_SIZE, 2 * DIFF_HEADS, DIFF_HDIM)),
        'cache_diff_v': nrm((n_pool, PAGE_SIZE, DIFF_HEADS, 2 * DIFF_HDIM)),
        'cache_mem_k': nrm((DEPTH, DEC_BATCH, MEM_LEN, X_HEADS, X_HDIM)),
        'cache_mem_v': nrm((DEPTH, DEC_BATCH, MEM_LEN, X_HEADS, X_HDIM)),
        'state_ffn_conv': nrm((DEPTH, DEC_BATCH, CONV_W - 1, 2 * D_FF)),
        'page_table': page_table,
        'mem_prompt': nrm((BATCH, MEM_LEN, D)),
        'norm_mix': gain((DEPTH, D)),
        'norm_mem': gain((DEPTH, D)),
        'norm_ffn': gain((DEPTH, D)),
        'norm_final': gain((D,)),
        'fox_w_in': nrm((D, 3 * FOX_HEADS * FOX_HDIM + FOX_HEADS), D ** -0.5),
        'fox_b_f': FORGET_BIAS + nrm((FOX_HEADS,), 0.3),
        'fox_w_out': nrm((FOX_HEADS * FOX_HDIM, D), (FOX_HEADS * FOX_HDIM) ** -0.5),
        'sb_w_in': nrm((D, 3 * SB_HEADS * SB_HDIM), D ** -0.5),
        'sb_w_out': nrm((SB_HEADS * SB_HDIM, D), (SB_HEADS * SB_HDIM) ** -0.5),
        'nsa_w_in': nrm((D, NSA_HEADS * dn + 6 * G * dn + 3 * NSA_HEADS), D ** -0.5),
        'nsa_cmp_pe': nrm((2, CMP_BLK, dn), 0.1),
        'nsa_cmp_w1': nrm((2, CMP_BLK, dn, CMP_HID), (CMP_BLK * dn) ** -0.5),
        'nsa_cmp_b1': nrm((2, CMP_HID), 0.01),
        'nsa_cmp_w2': nrm((2, CMP_HID, dn), CMP_HID ** -0.5),
        'nsa_w_out': nrm((NSA_HEADS * dn, D), (NSA_HEADS * dn) ** -0.5),
        'diff_w_in': nrm((D, 4 * DIFF_HEADS * DIFF_HDIM + 2 * DIFF_HEADS * DIFF_HDIM), D ** -0.5),
        'diff_lq1': nrm((DIFF_HDIM,), 0.1),
        'diff_lk1': nrm((DIFF_HDIM,), 0.1),
        'diff_lq2': nrm((DIFF_HDIM,), 0.1),
        'diff_lk2': nrm((DIFF_HDIM,), 0.1),
        'diff_subln': gain((2 * DIFF_HDIM,)),
        'diff_w_out': nrm((2 * DIFF_HEADS * DIFF_HDIM, D), (2 * DIFF_HEADS * DIFF_HDIM) ** -0.5),
        'mem_norm': gain((DEPTH, D)),
        'x_w_q': nrm((DEPTH, D, X_HEADS * X_HDIM), D ** -0.5),
        'x_w_kv': nrm((DEPTH, D, 2 * X_HEADS * X_HDIM), D ** -0.5),
        'x_w_o': nrm((DEPTH, X_HEADS * X_HDIM, D), (X_HEADS * X_HDIM) ** -0.5),
        'ffn_w_up': nrm((DEPTH, D, 2 * D_FF), D ** -0.5),
        'ffn_conv_w': nrm((DEPTH, CONV_W, 2 * D_FF), CONV_W ** -0.5),
        'ffn_conv_b': nrm((DEPTH, 2 * D_FF), 0.01),
        'ffn_w_down': nrm((DEPTH, D_FF, D), D_FF ** -0.5),
    }


def reference(x_prompt, x_sample, cache_fox_k, cache_fox_v, cache_fox_logf, cache_sb_k, cache_sb_v,
              cache_nsa_kcmp, cache_nsa_vcmp, cache_nsa_ksel, cache_nsa_vsel, state_nsa_kwin, state_nsa_vwin,
              cache_diff_k, cache_diff_v, cache_mem_k, cache_mem_v, state_ffn_conv, page_table, mem_prompt,
              norm_mix, norm_mem, norm_ffn, norm_final, fox_w_in, fox_b_f, fox_w_out, sb_w_in, sb_w_out,
              nsa_w_in, nsa_cmp_pe, nsa_cmp_w1, nsa_cmp_b1, nsa_cmp_w2, nsa_w_out,
              diff_w_in, diff_lq1, diff_lk1, diff_lq2, diff_lk2, diff_subln, diff_w_out,
              mem_norm, x_w_q, x_w_kv, x_w_o, ffn_w_up, ffn_conv_w, ffn_conv_b, ffn_w_down):
    P = {
        'norm_mix': norm_mix, 'norm_mem': norm_mem, 'norm_ffn': norm_ffn, 'norm_final': norm_final,
        'fox_w_in': fox_w_in, 'fox_b_f': fox_b_f, 'fox_w_out': fox_w_out,
        'sb_w_in': sb_w_in, 'sb_w_out': sb_w_out,
        'nsa_w_in': nsa_w_in, 'nsa_cmp_pe': nsa_cmp_pe, 'nsa_cmp_w1': nsa_cmp_w1, 'nsa_cmp_b1': nsa_cmp_b1,
        'nsa_cmp_w2': nsa_cmp_w2, 'nsa_w_out': nsa_w_out,
        'diff_w_in': diff_w_in, 'diff_lq1': diff_lq1, 'diff_lk1': diff_lk1, 'diff_lq2': diff_lq2,
        'diff_lk2': diff_lk2, 'diff_subln': diff_subln, 'diff_w_out': diff_w_out,
        'x_w_q': x_w_q, 'x_w_o': x_w_o,
        'ffn_w_up': ffn_w_up, 'ffn_conv_w': ffn_conv_w, 'ffn_conv_b': ffn_conv_b, 'ffn_w_down': ffn_w_down,
    }
    mem_k_p, mem_v_p = memory_kv(mem_prompt, mem_norm, x_w_kv)
    y_prompt, sp = trunk(x_prompt, 0, mem_k_p, mem_v_p, None, None, None, P)
    cache = {
        'fox_k': cache_fox_k, 'fox_v': cache_fox_v, 'fox_logf': cache_fox_logf,
        'sb_k': cache_sb_k, 'sb_v': cache_sb_v,
        'nsa_kcmp': cache_nsa_kcmp, 'nsa_vcmp': cache_nsa_vcmp, 'nsa_ksel': cache_nsa_ksel,
        'nsa_vsel': cache_nsa_vsel, 'nsa_kwin': state_nsa_kwin, 'nsa_vwin': state_nsa_vwin,
        'diff_k': cache_diff_k, 'diff_v': cache_diff_v,
    }
    y_sample, ss = trunk(x_sample, PAST_LEN, cache_mem_k, cache_mem_v, state_ffn_conv, cache, page_table, P)
    return (y_prompt, y_sample,
            sp['fox_k'], sp['fox_v'], sp['fox_logf'], sp['sb_k'], sp['sb_v'],
            sp['nsa_kcmp'], sp['nsa_vcmp'], sp['nsa_ksel'], sp['nsa_vsel'], sp['nsa_kwin'], sp['nsa_vwin'],
            sp['diff_k'], sp['diff_v'], mem_k_p, mem_v_p, sp['ffn_conv'],
            ss['fox_k'], ss['fox_v'], ss['fox_logf'], ss['sb_k'], ss['sb_v'],
            ss['nsa_kcmp'], ss['nsa_vcmp'], ss['nsa_ksel'], ss['nsa_vsel'], ss['nsa_kwin'], ss['nsa_vwin'],
            ss['diff_k'], ss['diff_v'], ss['ffn_conv'])
```

```python
import functools
import math

import jax
import jax.numpy as jnp
from jax import lax
from jax.experimental import pallas as pl
from jax.experimental.pallas import tpu as pltpu

F32 = jnp.float32
BF16 = jnp.bfloat16

V7X_VMEM_BYTES = 64 * 1024 * 1024
LANES = 128
SUBLANES = 8
VMEM_LIMIT = V7X_VMEM_BYTES - 8 * 1024 * 1024

NORM_EPS = 1e-6
MASK_VALUE = -1e30
ROPE_THETA = 10000.0
FORGET_HEADS = 16
HEAD_DIM = 64
GROUP = 4
GLANES = GROUP * HEAD_DIM


def _cparams(n_axes):
    return pltpu.CompilerParams(dimension_semantics=("arbitrary",) * n_axes, vmem_limit_bytes=VMEM_LIMIT)


def _nt_dot(a, b):
    return lax.dot_general(a, b, (((1,), (1,)), ((), ())), preferred_element_type=F32)


def _dot(a, b):
    return jnp.dot(a, b, preferred_element_type=F32)


def _split3(x):
    hi = x.astype(BF16)
    r1 = x - hi.astype(F32)
    mid = r1.astype(BF16)
    lo = (r1 - mid.astype(F32)).astype(BF16)
    return hi, mid, lo


def _dot_exact_rhs01(x, m01):
    hi, mid, lo = _split3(x)
    return _dot(hi, m01) + _dot(mid, m01) + _dot(lo, m01)


def _dot_exact_lhs01(m01, x):
    hi, mid, lo = _split3(x)
    return _dot(m01, hi) + _dot(m01, mid) + _dot(m01, lo)


def _softplus(z):
    return jnp.maximum(z, 0.0) + jnp.log1p(jnp.exp(-jnp.abs(z)))


def _log_sigmoid(z):
    return jnp.minimum(z, 0.0) - jnp.log1p(jnp.exp(-jnp.abs(z)))


def _mm_kernel(*refs, prologue, has_res, stage):
    it = iter(refs)
    x_ref = next(it)
    if prologue == "norm":
        g_ref = next(it)
    elif prologue == "add_gelu":
        x2_ref, pb_ref = next(it), next(it)
    w_ref = next(it)
    r_ref = next(it) if has_res else None
    o_ref = next(it)
    xs_ref = next(it) if stage else None

    if stage:
        @pl.when(pl.program_id(1) == 0)
        def _():
            x = x_ref[...].astype(F32)
            if prologue == "norm":
                x = x * lax.rsqrt(jnp.mean(x * x, axis=-1, keepdims=True) + NORM_EPS)
                x = x * g_ref[...]
            elif prologue == "add_gelu":
                x = jax.nn.gelu(x + x2_ref[...] + pb_ref[...])
            xs_ref[...] = x.astype(BF16)
        xb = xs_ref[...]
    else:
        xb = x_ref[...]
    y = _dot(xb, w_ref[...])
    if has_res:
        y = y + r_ref[...]
    o_ref[...] = y.astype(o_ref.dtype)


def _row_tile(n, cap):
    t = min(n, cap)
    while n % t:
        t //= 2
    return t


def matmul(x, w, *, gain=None, add=None, pre_bias=None, res=None, out_dtype=F32, tm_cap=1024, tn_cap=512):
    n, k = x.shape
    e = w.shape[1]
    tm = _row_tile(n, tm_cap)
    tn = _row_tile(e, tn_cap)
    assert n % tm == 0 and e % tn == 0 and tm % SUBLANES == 0 and tn % LANES == 0, (n, e, tm, tn)
    prologue = "norm" if gain is not None else ("add_gelu" if add is not None else None)
    stage = prologue is not None or x.dtype != BF16
    ins = [x]
    specs = [pl.BlockSpec((tm, k), lambda i, j: (i, 0))]
    if prologue == "norm":
        ins.append(gain.reshape(1, k).astype(F32))
        specs.append(pl.BlockSpec((1, k), lambda i, j: (0, 0)))
    elif prologue == "add_gelu":
        ins += [add, pre_bias.reshape(1, k).astype(F32)]
        specs += [pl.BlockSpec((tm, k), lambda i, j: (i, 0)), pl.BlockSpec((1, k), lambda i, j: (0, 0))]
    ins.append(w)
    specs.append(pl.BlockSpec((k, tn), lambda i, j: (0, j)))
    if res is not None:
        ins.append(res)
        specs.append(pl.BlockSpec((tm, tn), lambda i, j: (i, j)))
    kern = functools.partial(_mm_kernel, prologue=prologue, has_res=res is not None, stage=stage)
    return pl.pallas_call(
        kern,
        out_shape=jax.ShapeDtypeStruct((n, e), out_dtype),
        grid=(n // tm, e // tn),
        in_specs=specs,
        out_specs=pl.BlockSpec((tm, tn), lambda i, j: (i, j)),
        scratch_shapes=[pltpu.VMEM((tm, k), BF16)] if stage else [],
        compiler_params=_cparams(2),
        name="matmul",
    )(*ins)


def _conv_gate_kernel(uv_ref, ug_ref, hv_ref, hg_ref, sv_ref, sg_ref, wv_ref, wg_ref, o_ref):
    first = pl.program_id(1) == 0

    def conv(u_ref, h_ref, s_ref, w_ref):
        u = u_ref[0]
        prev = jnp.where(first, s_ref[0], h_ref[0])
        p2 = prev[6:7, :]
        p1 = prev[7:8, :]
        row = lax.broadcasted_iota(jnp.int32, u.shape, 0)
        um1 = jnp.where(row == 0, p1, pltpu.roll(u, 1, 0))
        um2 = jnp.where(row == 0, p2, jnp.where(row == 1, p1, pltpu.roll(u, 2, 0)))
        w = w_ref[...]
        return w[3:4, :] + w[0:1, :] * um2 + w[1:2, :] * um1 + w[2:3, :] * u

    val = conv(uv_ref, hv_ref, sv_ref, wv_ref)
    gate = conv(ug_ref, hg_ref, sg_ref, wg_ref)
    o_ref[0] = (gate * jax.nn.sigmoid(gate) * val).astype(o_ref.dtype)


def conv_gate(u, state8, wpack):
    b, t, f2 = u.shape
    f = f2 // 2
    tf = f // 2
    assert tf % LANES == 0
    nf = f // tf
    tt = _row_tile(t, 512)
    hb = tt // SUBLANES

    def halo(i):
        return jnp.maximum(i * hb - 1, 0)

    return pl.pallas_call(
        _conv_gate_kernel,
        out_shape=jax.ShapeDtypeStruct((b, t, f), BF16),
        grid=(b, t // tt, nf),
        in_specs=[
            pl.BlockSpec((1, tt, tf), lambda bi, i, j: (bi, i, j)),
            pl.BlockSpec((1, tt, tf), lambda bi, i, j: (bi, i, nf + j)),
            pl.BlockSpec((1, SUBLANES, tf), lambda bi, i, j: (bi, halo(i), j)),
            pl.BlockSpec((1, SUBLANES, tf), lambda bi, i, j: (bi, halo(i), nf + j)),
            pl.BlockSpec((1, SUBLANES, tf), lambda bi, i, j: (bi, 0, j)),
            pl.BlockSpec((1, SUBLANES, tf), lambda bi, i, j: (bi, 0, nf + j)),
            pl.BlockSpec((SUBLANES, tf), lambda bi, i, j: (0, j)),
            pl.BlockSpec((SUBLANES, tf), lambda bi, i, j: (0, nf + j)),
        ],
        out_specs=pl.BlockSpec((1, tt, tf), lambda bi, i, j: (bi, i, j)),
        compiler_params=_cparams(3),
        name="conv_gate",
    )(u, u, u, u, state8, state8, wpack, wpack)


def _cross_kernel(x_ref, g_ref, wq_ref, mk_ref, mv_ref, wo_ref, o_ref, *, heads, hdim):
    x = x_ref[0]
    h = x * lax.rsqrt(jnp.mean(x * x, axis=-1, keepdims=True) + NORM_EPS) * g_ref[...]
    q = _dot(h.astype(BF16), wq_ref[...]) * (hdim ** -0.5)
    outs = []
    for hh in range(heads):
        sl = slice(hh * hdim, (hh + 1) * hdim)
        s = _nt_dot(q[:, sl].astype(BF16), mk_ref[0, :, sl].astype(BF16))
        p = jnp.exp(s - jnp.max(s, axis=-1, keepdims=True))
        p = p / jnp.sum(p, axis=-1, keepdims=True)
        outs.append(_dot(p.astype(BF16), mv_ref[0, :, sl].astype(BF16)))
    o = jnp.concatenate(outs, axis=-1)
    o_ref[0] = x + _dot(o.astype(BF16), wo_ref[...])


def cross_block(x, gain, wq, mk, mv, wo, *, heads=4, hdim=128):
    b, t, d = x.shape
    m = mk.shape[1]
    e = heads * hdim
    tt = _row_tile(t, 512)
    return pl.pallas_call(
        functools.partial(_cross_kernel, heads=heads, hdim=hdim),
        out_shape=jax.ShapeDtypeStruct((b, t, d), F32),
        grid=(b, t // tt),
        in_specs=[
            pl.BlockSpec((1, tt, d), lambda bi, i: (bi, i, 0)),
            pl.BlockSpec((1, d), lambda bi, i: (0, 0)),
            pl.BlockSpec((d, e), lambda bi, i: (0, 0)),
            pl.BlockSpec((1, m, e), lambda bi, i: (bi, 0, 0)),
            pl.BlockSpec((1, m, e), lambda bi, i: (bi, 0, 0)),
            pl.BlockSpec((e, d), lambda bi, i: (0, 0)),
        ],
        out_specs=pl.BlockSpec((1, tt, d), lambda bi, i: (bi, i, 0)),
        compiler_params=_cparams(2),
        name="cross_block",
    )(x, gain.reshape(1, d), wq, mk, mv, wo)


def rope_tables(pos):
    half = HEAD_DIM // 2
    inv = ROPE_THETA ** (-jnp.arange(half, dtype=F32) / half)
    ang = pos.astype(F32)[:, None] * inv[None, :]
    cos = jnp.cos(ang)
    sin = jnp.sin(ang)
    cos_t = jnp.tile(jnp.concatenate([cos, cos], axis=-1), (1, GROUP))
    sin_t = jnp.tile(jnp.concatenate([-sin, sin], axis=-1), (1, GROUP))
    return cos_t, sin_t


def _rope_apply(x, cos_t, sin_t):
    half = HEAD_DIM // 2
    lane = lax.broadcasted_iota(jnp.int32, x.shape, 1)
    first = (lane % HEAD_DIM) < half
    n = x.shape[1]
    swapped = jnp.where(first, pltpu.roll(x, n - half, 1), pltpu.roll(x, half, 1))
    return x * cos_t + swapped * sin_t


def _rope_kernel(x_ref, c_ref, s_ref, o_ref):
    o_ref[0] = _rope_apply(x_ref[0], c_ref[...], s_ref[...])


def rope_cols(x, col0, width, cos_t, sin_t):
    b, t, _ = x.shape
    assert col0 % GLANES == 0 and width % GLANES == 0
    tt = _row_tile(t, 512)
    c0 = col0 // GLANES
    return pl.pallas_call(
        _rope_kernel,
        out_shape=jax.ShapeDtypeStruct((b, t, width), F32),
        grid=(b, t // tt, width // GLANES),
        in_specs=[
            pl.BlockSpec((1, tt, GLANES), lambda bi, i, j: (bi, i, c0 + j)),
            pl.BlockSpec((tt, GLANES), lambda bi, i, j: (i, 0)),
            pl.BlockSpec((tt, GLANES), lambda bi, i, j: (i, 0)),
        ],
        out_specs=pl.BlockSpec((1, tt, GLANES), lambda bi, i, j: (bi, i, j)),
        compiler_params=_cparams(3),
        name="rope",
    )(x, cos_t, sin_t)


def _logf_cumsum_kernel(g_ref, b_ref, lf_ref, c_ref, *, chunk):
    t = g_ref.shape[1]
    row = lax.broadcasted_iota(jnp.int32, (chunk, chunk), 0)
    col = lax.broadcasted_iota(jnp.int32, (chunk, chunk), 1)
    tri = jnp.where(col <= row, 1.0, 0.0).astype(BF16)

    def body(i, carry):
        sl = pl.ds(pl.multiple_of(i * chunk, chunk), chunk)
        lf = _log_sigmoid(g_ref[0, sl, :] + b_ref[...])
        lf_ref[0, sl, :] = lf
        c = _dot_exact_lhs01(tri, lf) + carry
        c_ref[0, sl, :] = c
        return c[chunk - 1:chunk, :]

    lax.fori_loop(0, t // chunk, body, jnp.zeros((1, g_ref.shape[2]), F32))


def logf_cumsum(gate, bias):
    b, t, n = gate.shape
    chunk = _row_tile(t, 256)
    spec = pl.BlockSpec((1, t, n), lambda bi: (bi, 0, 0))
    return pl.pallas_call(
        functools.partial(_logf_cumsum_kernel, chunk=chunk),
        out_shape=(jax.ShapeDtypeStruct((b, t, n), F32), jax.ShapeDtypeStruct((b, t, n), F32)),
        grid=(b,),
        in_specs=[spec, pl.BlockSpec((1, n), lambda bi: (0, 0))],
        out_specs=(spec, spec),
        compiler_params=_cparams(1),
        name="logf_cumsum",
    )(gate, bias.reshape(1, n))


def _stack_heads(q):
    lane = lax.broadcasted_iota(jnp.int32, q.shape, 1) // HEAD_DIM
    return jnp.concatenate([jnp.where(lane == r, q, 0.0).astype(BF16) for r in range(GROUP)], axis=0)


def _causal_mask(tq, tk, q0, k0, strict):
    row = lax.broadcasted_iota(jnp.int32, (GROUP * tq, tk), 0) % tq + q0
    col = lax.broadcasted_iota(jnp.int32, (GROUP * tq, tk), 1) + k0
    return col < row if strict else col <= row


def _softmax_tile(s, mask, v, m_ref, l_ref, acc_ref):
    if mask is not None:
        s = jnp.where(mask, s, MASK_VALUE)
    m_prev = m_ref[...]
    m_new = jnp.maximum(m_prev, jnp.max(s, axis=1, keepdims=True))
    alpha = jnp.exp(m_prev - m_new)
    p = jnp.exp(s - m_new)
    if mask is not None:
        p = jnp.where(mask, p, 0.0)
    l_ref[...] = alpha * l_ref[...] + jnp.sum(p, axis=1, keepdims=True)
    acc_ref[...] = alpha * acc_ref[...] + _dot(p.astype(BF16), v)
    m_ref[...] = m_new


def _pflash_kernel(*refs, kind, tq, tk, out_scale):
    if kind == "fox":
        q_ref, k_ref, v_ref, ck_ref, o_ref, m_ref, l_ref, acc_ref = refs
    else:
        q_ref, k_ref, v_ref, lam_ref, sub_ref, o_ref, m_ref, l_ref, acc_ref = refs
    qi = pl.program_id(2)
    q4 = _stack_heads(q_ref[0] * (HEAD_DIM ** -0.5))
    m_ref[...] = jnp.full(m_ref.shape, MASK_VALUE, F32)
    l_ref[...] = jnp.zeros(l_ref.shape, F32)
    acc_ref[...] = jnp.zeros(acc_ref.shape, F32)

    def tile(kt, masked):
        ks = pl.ds(pl.multiple_of(kt * tk, tk), tk)
        s = _nt_dot(q4, k_ref[0, ks, :].astype(BF16))
        if kind == "fox":
            ck = ck_ref[0, 0, :, ks]
            s = s - jnp.concatenate(
                [jnp.broadcast_to(ck[r:r + 1, :], (tq, tk)) for r in range(GROUP)], axis=0)
        mask = _causal_mask(tq, tk, qi * tq, kt * tk, False) if masked else None
        _softmax_tile(s, mask, v_ref[0, ks, :].astype(BF16), m_ref, l_ref, acc_ref)

    def body(kt, c):
        tile(kt, False)
        return c

    lax.fori_loop(0, qi, body, 0)
    tile(qi, True)

    inv = 1.0 / jnp.maximum(l_ref[...], 1e-30)
    lane = lax.broadcasted_iota(jnp.int32, (tq, GLANES), 1)
    blk = [acc_ref[r * tq:(r + 1) * tq, :] * inv[r * tq:(r + 1) * tq, :] for r in range(GROUP)]
    if kind == "fox":
        out = jnp.zeros((tq, GLANES), F32)
        for r in range(GROUP):
            out = jnp.where(lane // HEAD_DIM == r, blk[r], out)
        o_ref[0] = out
    else:
        lam = lam_ref[0, 0]
        halves = []
        for hh in range(2):
            d = (blk[2 * hh] - lam * blk[2 * hh + 1])[:, hh * LANES:(hh + 1) * LANES]
            y = d * lax.rsqrt(jnp.mean(d * d, axis=-1, keepdims=True) + NORM_EPS) * sub_ref[...]
            halves.append(y * out_scale)
        o_ref[0] = jnp.concatenate(halves, axis=1)


def prompt_flash(kind, qa, qcol, ka, kcol, va, vcol, *, ck=None, lam=None, subln=None, out_scale=1.0, tile=256):
    b, t, _ = qa.shape
    tq = tk = _row_tile(t, tile)
    ngroups = 4
    ins = [qa, ka, va]
    specs = [
        pl.BlockSpec((1, tq, GLANES), lambda bi, g, i: (bi, i, qcol + g)),
        pl.BlockSpec((1, t, GLANES), lambda bi, g, i: (bi, 0, kcol + g)),
        pl.BlockSpec((1, t, GLANES), lambda bi, g, i: (bi, 0, vcol + g)),
    ]
    if kind == "fox":
        ins.append(ck)
        specs.append(pl.BlockSpec((1, 1, SUBLANES, t), lambda bi, g, i: (bi, g, 0, 0)))
    else:
        ins += [lam.reshape(1, 1), subln.reshape(1, LANES)]
        specs += [pl.BlockSpec(memory_space=pltpu.SMEM), pl.BlockSpec((1, LANES), lambda bi, g, i: (0, 0))]
    return pl.pallas_call(
        functools.partial(_pflash_kernel, kind=kind, tq=tq, tk=tk, out_scale=out_scale),
        out_shape=jax.ShapeDtypeStruct((b, t, ngroups * GLANES), F32),
        grid=(b, ngroups, t // tq),
        in_specs=specs,
        out_specs=pl.BlockSpec((1, tq, GLANES), lambda bi, g, i: (bi, i, g)),
        scratch_shapes=[
            pltpu.VMEM((GROUP * tq, 1), F32),
            pltpu.VMEM((GROUP * tq, 1), F32),
            pltpu.VMEM((GROUP * tq, GLANES), F32),
        ],
        compiler_params=_cparams(3),
        name="prompt_flash_" + kind,
    )(*ins)


def _strict_lower_ones(n):
    row = lax.broadcasted_iota(jnp.int32, (n, n), 0)
    col = lax.broadcasted_iota(jnp.int32, (n, n), 1)
    return jnp.where(row > col, 1.0, 0.0).astype(BF16)


def _sb_tile(z, mask, v, tri, carry_ref, acc_ref):
    sp = _softplus(z)
    log_rest = -sp
    if mask is not None:
        log_rest = jnp.where(mask, log_rest, 0.0)
    after = carry_ref[...] + _dot_exact_rhs01(log_rest, tri)
    a = jnp.exp(z - sp + after)
    if mask is not None:
        a = jnp.where(mask, a, 0.0)
    acc_ref[...] += _dot(a.astype(BF16), v)
    carry_ref[...] = after[:, 0:1] + log_rest[:, 0:1]


def _psb_kernel(q_ref, k_ref, v_ref, o_ref, carry_ref, acc_ref, *, tq, tk):
    qi = pl.program_id(2)
    q4 = _stack_heads(q_ref[0] * (HEAD_DIM ** -0.5))
    carry_ref[...] = jnp.zeros(carry_ref.shape, F32)
    acc_ref[...] = jnp.zeros(acc_ref.shape, F32)
    tri = _strict_lower_ones(tk)

    def tile(kt, masked):
        ks = pl.ds(pl.multiple_of(kt * tk, tk), tk)
        z = _nt_dot(q4, k_ref[0, ks, :].astype(BF16))
        mask = _causal_mask(tq, tk, qi * tq, kt * tk, True) if masked else None
        _sb_tile(z, mask, v_ref[0, ks, :].astype(BF16), tri, carry_ref, acc_ref)

    tile(qi, True)

    def body(i, c):
        tile(qi - 1 - i, False)
        return c

    lax.fori_loop(0, qi, body, 0)
    lane = lax.broadcasted_iota(jnp.int32, (tq, GLANES), 1)
    out = jnp.zeros((tq, GLANES), F32)
    for r in range(GROUP):
        out = jnp.where(lane // HEAD_DIM == r, acc_ref[r * tq:(r + 1) * tq, :], out)
    o_ref[0] = out


def prompt_sb(proj, *, tile=256):
    b, t, _ = proj.shape
    tq = tk = _row_tile(t, tile)
    ngroups = 4
    return pl.pallas_call(
        functools.partial(_psb_kernel, tq=tq, tk=tk),
        out_shape=jax.ShapeDtypeStruct((b, t, ngroups * GLANES), F32),
        grid=(b, ngroups, t // tq),
        in_specs=[
            pl.BlockSpec((1, tq, GLANES), lambda bi, g, i: (bi, i, g)),
            pl.BlockSpec((1, t, GLANES), lambda bi, g, i: (bi, 0, ngroups + g)),
            pl.BlockSpec((1, t, GLANES), lambda bi, g, i: (bi, 0, 2 * ngroups + g)),
        ],
        out_specs=pl.BlockSpec((1, tq, GLANES), lambda bi, g, i: (bi, i, g)),
        scratch_shapes=[
            pltpu.VMEM((GROUP * tq, 1), F32),
            pltpu.VMEM((GROUP * tq, GLANES), F32),
        ],
        compiler_params=_cparams(3),
        name="prompt_sb",
    )(proj, proj, proj)


CMP_STRIDE = 16
CMP_BLK = 32
SEL_BLK = 64
N_SELECT = 16
WINDOW = 512
SEL_FORCE = 1e4


def _stack_group_heads(q):
    return jnp.concatenate([q[:, r * HEAD_DIM:(r + 1) * HEAD_DIM] for r in range(GROUP)], axis=0).astype(BF16)


def _unstack_group_heads(o, t):
    return jnp.concatenate([o[r * t:(r + 1) * t, :] for r in range(GROUP)], axis=1)


def _select_blocks(p_sum, q_pos, n_sel, n_top):
    t, nc = p_sum.shape
    nb = -(-n_sel // LANES) * LANES
    c_idx = lax.broadcasted_iota(jnp.int32, (nc, nb), 0) * CMP_STRIDE
    j_idx = lax.broadcasted_iota(jnp.int32, (nc, nb), 1)
    overlap = (c_idx < j_idx * SEL_BLK + SEL_BLK) & (c_idx + CMP_BLK > j_idx * SEL_BLK) & (j_idx < n_sel)
    imp = _dot_exact_rhs01(p_sum, jnp.where(overlap, 1.0, 0.0).astype(BF16))
    blk = lax.broadcasted_iota(jnp.int32, (t, nb), 1)
    cur = q_pos // SEL_BLK
    forced = (blk == 0) | (blk == cur) | (blk == cur - 1)
    valid = blk * SEL_BLK <= q_pos
    score = jnp.where(forced, SEL_FORCE, jnp.where(valid, imp, -1.0))
    score = jnp.where(blk < n_sel, score, -3e38)
    rows = -(-n_sel // SUBLANES) * SUBLANES
    sc_t = score.T[:rows, :]
    j_iota = lax.broadcasted_iota(jnp.int32, sc_t.shape, 0)
    rank = jnp.zeros(sc_t.shape, F32)
    for jp in range(n_sel):
        other = sc_t[jp:jp + 1, :]
        ge = jnp.where(other >= sc_t, 1.0, 0.0)
        gt = jnp.where(other > sc_t, 1.0, 0.0)
        rank = rank + jnp.where(j_iota > jp, ge, gt)
    sel_t = jnp.where(rank < n_top, 1.0, 0.0)
    if rows < nb:
        sel_t = jnp.concatenate([sel_t, jnp.zeros((nb - rows, t), F32)], axis=0)
    return sel_t.T


def _expand_block_mask(sel, k0, tk):
    nb = sel.shape[1]
    j = lax.broadcasted_iota(jnp.int32, (nb, tk), 0)
    key = lax.broadcasted_iota(jnp.int32, (nb, tk), 1) + k0
    e = jnp.where(j == key // SEL_BLK, 1.0, 0.0).astype(BF16)
    return _dot(sel.astype(BF16), e)


def _finish_softmax(l_ref, acc_ref):
    return acc_ref[...] / jnp.maximum(l_ref[...], 1e-30)


def _reset_softmax(m_ref, l_ref, acc_ref):
    m_ref[...] = jnp.full(m_ref.shape, MASK_VALUE, F32)
    l_ref[...] = jnp.zeros(l_ref.shape, F32)
    acc_ref[...] = jnp.zeros(acc_ref.shape, F32)


def _expand_gates(gates, branch, g, width):
    c = lax.broadcasted_iota(jnp.int32, (LANES, width), 0)
    lane = lax.broadcasted_iota(jnp.int32, (LANES, width), 1)
    e = jnp.where(c == branch * (GROUP * GROUP) + g * GROUP + lane // HEAD_DIM, 1.0, 0.0).astype(BF16)
    return _dot_exact_rhs01(gates, e)


def _pnsa_kernel(qc_ref, qr_ref, gate_ref, kc_ref, vc_ref, ks_ref, vs_ref, kw_ref, vw_ref, o_ref,
                 m_ref, l_ref, acc_ref, *, tq, tk, n_sel):
    g = pl.program_id(1)
    qi = pl.program_id(2)
    scale = HEAD_DIM ** -0.5
    q0 = qi * tq
    rows4 = GROUP * tq
    q_pos4 = lax.broadcasted_iota(jnp.int32, (rows4, 1), 0) % tq + q0

    qc = _stack_group_heads(qc_ref[0] * scale)
    nc = kc_ref.shape[2]
    s = _nt_dot(qc, kc_ref[0, 0].astype(BF16))
    cmp_end = lax.broadcasted_iota(jnp.int32, (rows4, nc), 1) * CMP_STRIDE + (CMP_BLK - 1)
    mask = cmp_end <= q_pos4
    s = jnp.where(mask, s, MASK_VALUE)
    p = jnp.where(mask, jnp.exp(s - jnp.max(s, axis=1, keepdims=True)), 0.0)
    p = p / jnp.maximum(jnp.sum(p, axis=1, keepdims=True), 1e-30)
    o_c = _dot(p.astype(BF16), vc_ref[0, 0].astype(BF16))
    p_sum = p[0:tq] + p[tq:2 * tq] + p[2 * tq:3 * tq] + p[3 * tq:4 * tq]
    sel = _select_blocks(p_sum, q_pos4[0:tq], n_sel, min(N_SELECT, n_sel))

    qr = _stack_group_heads(qr_ref[0] * scale)

    _reset_softmax(m_ref, l_ref, acc_ref)

    def sel_tile(kt, c):
        ks = pl.ds(pl.multiple_of(kt * tk, tk), tk)
        s = _nt_dot(qr, ks_ref[0, 0, ks, :].astype(BF16))
        chosen = _expand_block_mask(sel, kt * tk, tk)
        kpos = lax.broadcasted_iota(jnp.int32, (tq, tk), 1) + kt * tk
        m1 = jnp.where(kpos <= q_pos4[0:tq], chosen, 0.0)
        m4 = jnp.concatenate([m1] * GROUP, axis=0) > 0.5
        _softmax_tile(s, m4, vs_ref[0, 0, ks, :].astype(BF16), m_ref, l_ref, acc_ref)
        return c

    lax.fori_loop(0, qi + 1, sel_tile, 0)
    o_s = _finish_softmax(l_ref, acc_ref)

    _reset_softmax(m_ref, l_ref, acc_ref)

    def win_tile(kt, c):
        ks = pl.ds(pl.multiple_of(kt * tk, tk), tk)
        s = _nt_dot(qr, kw_ref[0, 0, ks, :].astype(BF16))
        dist = q_pos4 - (lax.broadcasted_iota(jnp.int32, (rows4, tk), 1) + kt * tk)
        m4 = (dist >= 0) & (dist < WINDOW)
        _softmax_tile(s, m4, vw_ref[0, 0, ks, :].astype(BF16), m_ref, l_ref, acc_ref)
        return c

    kt_lo = jnp.maximum(q0 - (WINDOW - 1), 0) // tk
    lax.fori_loop(kt_lo, qi + 1, win_tile, 0)
    o_w = _finish_softmax(l_ref, acc_ref)

    gates = jax.nn.sigmoid(gate_ref[0])
    out = (_expand_gates(gates, 0, g, GLANES) * _unstack_group_heads(o_c, tq)
           + _expand_gates(gates, 1, g, GLANES) * _unstack_group_heads(o_s, tq)
           + _expand_gates(gates, 2, g, GLANES) * _unstack_group_heads(o_w, tq))
    o_ref[0] = out


def prompt_nsa(proj, q_rot, gate, kcmp, vcmp, ksel, vsel, kwin, vwin, *, tile=256):
    b, t, _ = proj.shape
    tq = tk = _row_tile(t, tile)
    nc = kcmp.shape[2]
    n_sel = t // SEL_BLK
    assert t % SEL_BLK == 0 and n_sel <= LANES and nc % LANES == 0
    kv_spec = pl.BlockSpec((1, 1, t, HEAD_DIM), lambda bi, g, i: (bi, g, 0, 0))
    cmp_spec = pl.BlockSpec((1, 1, nc, HEAD_DIM), lambda bi, g, i: (bi, g, 0, 0))
    return pl.pallas_call(
        functools.partial(_pnsa_kernel, tq=tq, tk=tk, n_sel=n_sel),
        out_shape=jax.ShapeDtypeStruct((b, t, GROUP * GLANES), F32),
        grid=(b, GROUP, t // tq),
        in_specs=[
            pl.BlockSpec((1, tq, GLANES), lambda bi, g, i: (bi, i, g)),
            pl.BlockSpec((1, tq, GLANES), lambda bi, g, i: (bi, i, g)),
            pl.BlockSpec((1, tq, LANES), lambda bi, g, i: (bi, i, 0)),
            cmp_spec, cmp_spec, kv_spec, kv_spec, kv_spec, kv_spec,
        ],
        out_specs=pl.BlockSpec((1, tq, GLANES), lambda bi, g, i: (bi, i, g)),
        scratch_shapes=[
            pltpu.VMEM((GROUP * tq, 1), F32),
            pltpu.VMEM((GROUP * tq, 1), F32),
            pltpu.VMEM((GROUP * tq, HEAD_DIM), F32),
        ],
        compiler_params=_cparams(3),
        name="prompt_nsa",
    )(proj, q_rot, gate, kcmp, vcmp, ksel, vsel, kwin, vwin)


def compress_rows(rows, pe, w1, b1, w2):
    b, g, tp, d = rows.shape
    n = tp // CMP_STRIDE
    hid = w1.shape[-1]
    a = rows.reshape(b * g * n, CMP_STRIDE * d)
    w1f = w1.reshape(CMP_BLK * d, hid)
    w1cat = jnp.concatenate([w1f[:CMP_STRIDE * d], w1f[CMP_STRIDE * d:]], axis=1).astype(BF16)
    h = matmul(a, w1cat).reshape(b, g, n, 2 * hid)
    h_first = h[..., :hid]
    h_second = jnp.concatenate([h[:, :, 1:, hid:], jnp.zeros((b, g, 1, hid), F32)], axis=2)
    bias = b1 + jnp.einsum("k,kh->h", pe.reshape(-1), w1f, precision=lax.Precision.HIGHEST)
    w2p = jnp.pad(w2, ((0, 0), (0, LANES - d))).astype(BF16)
    out = matmul(h_first.reshape(b * g * n, hid), w2p, add=h_second.reshape(b * g * n, hid), pre_bias=bias)
    return out[:, :d].reshape(b, g, n, d)


NEW_PAD = 16


def _rows_tile(x, reps):
    return jnp.concatenate([x] * reps, axis=0)


def _sattn_kernel(*refs, kind, n_new, heads, vdim, out_scale):
    it = iter(refs)
    pt_ref = next(it)
    q_ref, kn_ref, vn_ref, k_ref, v_ref = (next(it) for _ in range(5))
    if kind == "fox":
        lf_ref, nc_ref = next(it), next(it)
    if kind == "diff":
        lam_ref, sub_ref = next(it), next(it)
    o_ref = next(it)
    qs_ref = next(it)
    if kind == "sb":
        carry_ref, acc_ref = next(it), next(it)
    else:
        m_ref, l_ref, acc_ref = next(it), next(it), next(it)
    if kind == "fox":
        later_ref = next(it)
    del pt_ref
    p = pl.program_id(1)
    rows = n_new * heads
    page = k_ref.shape[1]

    @pl.when(p == 0)
    def _():
        qs_ref[...] = (q_ref[0] * (HEAD_DIM ** -0.5)).astype(BF16)
        acc_ref[...] = jnp.zeros(acc_ref.shape, F32)
        s = _nt_dot(qs_ref[...], kn_ref[0].astype(BF16))
        t_idx = lax.broadcasted_iota(jnp.int32, (rows, NEW_PAD), 0) // heads
        s_idx = lax.broadcasted_iota(jnp.int32, (rows, NEW_PAD), 1)
        vn = vn_ref[0].astype(BF16)
        if kind == "sb":
            carry_ref[...] = jnp.zeros(carry_ref.shape, F32)
            _sb_tile(s, s_idx < t_idx, vn, _strict_lower_ones(NEW_PAD), carry_ref, acc_ref)
        else:
            m_ref[...] = jnp.full(m_ref.shape, MASK_VALUE, F32)
            l_ref[...] = jnp.zeros(l_ref.shape, F32)
            if kind == "fox":
                later_ref[...] = jnp.zeros(later_ref.shape, F32)
                s = s - _rows_tile(nc_ref[0], n_new)
            _softmax_tile(s, s_idx <= t_idx, vn, m_ref, l_ref, acc_ref)

    s = _nt_dot(qs_ref[...], k_ref[0].astype(BF16))
    v = v_ref[0].astype(BF16)
    if kind == "sb":
        _sb_tile(s, None, v, _strict_lower_ones(page), carry_ref, acc_ref)
    else:
        if kind == "fox":
            lf = lf_ref[0]
            suffix = _dot_exact_rhs01(lf, _strict_lower_ones(page))
            s = s + _rows_tile(suffix + later_ref[...], n_new)
            later_ref[...] = later_ref[...] + suffix[:, 0:1] + lf[:, 0:1]
        _softmax_tile(s, None, v, m_ref, l_ref, acc_ref)

    @pl.when(p == pl.num_programs(1) - 1)
    def _():
        width = acc_ref.shape[1]
        h_idx = lax.broadcasted_iota(jnp.int32, (heads, width), 0)
        lane = lax.broadcasted_iota(jnp.int32, (heads, width), 1)
        own = lane // vdim == (h_idx * HEAD_DIM) // vdim
        if kind == "sb":
            accn = acc_ref[...]
        else:
            accn = acc_ref[...] / jnp.maximum(l_ref[...], 1e-30)
        if kind == "diff":
            sign = jnp.where(h_idx % 2 == 0, 1.0, -lam_ref[0, 0])
        out_rows = []
        for t in range(n_new):
            blk = accn[t * heads:(t + 1) * heads, :]
            if kind == "diff":
                blk = blk * sign
            out_rows.append(jnp.sum(jnp.where(own, blk, 0.0), axis=0, keepdims=True))
        out = jnp.concatenate(out_rows, axis=0)
        if kind == "diff":
            parts = []
            for hh in range(width // vdim):
                d = out[:, hh * vdim:(hh + 1) * vdim]
                parts.append(d * lax.rsqrt(jnp.mean(d * d, axis=-1, keepdims=True) + NORM_EPS) * sub_ref[...]
                             * out_scale)
            out = jnp.concatenate(parts, axis=1)
        o_ref[0] = out


def _block_diag_rows(q, heads):
    b, t, e = q.shape
    qh = q.reshape(b, t, heads, e // heads)
    eye = jnp.eye(heads, dtype=q.dtype)
    return jnp.einsum("bthd,hk->bthkd", qh, eye).reshape(b, t * heads, e)


def _pad_rows(x, n):
    return jnp.pad(x, ((0, 0), (0, n - x.shape[1]), (0, 0)))


def sample_attn(kind, page_table, q, k_new, v_new, k_cache, v_cache, *, lf_cache_t=None, new_cum_t=None,
                lam=None, subln=None, out_scale=1.0):
    b, n_new, e = q.shape
    heads = e // HEAD_DIM
    rows = n_new * heads
    n_pages = page_table.shape[1]
    page = k_cache.shape[1]
    vdim = LANES if kind == "diff" else HEAD_DIM
    last = n_pages - 1
    ins = [_block_diag_rows(q, heads), _pad_rows(k_new, NEW_PAD), _pad_rows(v_new, NEW_PAD), k_cache, v_cache]
    specs = [
        pl.BlockSpec((1, rows, e), lambda bi, p, pt: (bi, 0, 0)),
        pl.BlockSpec((1, NEW_PAD, e), lambda bi, p, pt: (bi, 0, 0)),
        pl.BlockSpec((1, NEW_PAD, e), lambda bi, p, pt: (bi, 0, 0)),
        pl.BlockSpec((1, page, e), lambda bi, p, pt: (pt[bi, last - p], 0, 0)),
        pl.BlockSpec((1, page, e), lambda bi, p, pt: (pt[bi, last - p], 0, 0)),
    ]
    if kind == "fox":
        ins += [lf_cache_t, new_cum_t]
        specs += [pl.BlockSpec((1, heads, page), lambda bi, p, pt: (pt[bi, last - p], 0, 0)),
                  pl.BlockSpec((1, heads, NEW_PAD), lambda bi, p, pt: (bi, 0, 0))]
    if kind == "diff":
        ins += [lam.reshape(1, 1), subln.reshape(1, LANES)]
        specs += [pl.BlockSpec(memory_space=pltpu.SMEM), pl.BlockSpec((1, LANES), lambda bi, p, pt: (0, 0))]
    scratch = [pltpu.VMEM((rows, e), BF16)]
    if kind == "sb":
        scratch += [pltpu.VMEM((rows, 1), F32), pltpu.VMEM((rows, e), F32)]
    else:
        scratch += [pltpu.VMEM((rows, 1), F32), pltpu.VMEM((rows, 1), F32), pltpu.VMEM((rows, e), F32)]
    if kind == "fox":
        scratch.append(pltpu.VMEM((heads, 1), F32))
    return pl.pallas_call(
        functools.partial(_sattn_kernel, kind=kind, n_new=n_new, heads=heads, vdim=vdim, out_scale=out_scale),
        out_shape=jax.ShapeDtypeStruct((b, n_new, e), F32),
        grid_spec=pltpu.PrefetchScalarGridSpec(
            num_scalar_prefetch=1,
            grid=(b, n_pages),
            in_specs=specs,
            out_specs=pl.BlockSpec((1, n_new, e), lambda bi, p, pt: (bi, 0, 0)),
            scratch_shapes=scratch,
        ),
        compiler_params=_cparams(2),
        name="sample_attn_" + kind,
    )(page_table, *ins)


def _gather_groups_kernel(pt_ref, page_ref, new_ref, o_ref):
    del pt_ref
    is_new = pl.program_id(1) == pl.num_programs(1) - 1

    @pl.when(is_new)
    def _():
        for g in range(GROUP):
            o_ref[0, g] = new_ref[0, :, g * HEAD_DIM:(g + 1) * HEAD_DIM]

    @pl.when(jnp.logical_not(is_new))
    def _():
        for g in range(GROUP):
            o_ref[0, g] = page_ref[0, :, g * HEAD_DIM:(g + 1) * HEAD_DIM]


def gather_groups(cache, page_table, new_rows):
    b, n_pages = page_table.shape
    page = cache.shape[1]
    new_pad = _pad_rows(new_rows, page)
    return pl.pallas_call(
        _gather_groups_kernel,
        out_shape=jax.ShapeDtypeStruct((b, GROUP, (n_pages + 1) * page, HEAD_DIM), F32),
        grid_spec=pltpu.PrefetchScalarGridSpec(
            num_scalar_prefetch=1,
            grid=(b, n_pages + 1),
            in_specs=[
                pl.BlockSpec((1, page, GLANES), lambda bi, p, pt: (pt[bi, jnp.minimum(p, n_pages - 1)], 0, 0)),
                pl.BlockSpec((1, page, GLANES), lambda bi, p, pt: (bi, 0, 0)),
            ],
            out_specs=pl.BlockSpec((1, GROUP, page, HEAD_DIM), lambda bi, p, pt: (bi, 0, p, 0)),
        ),
        compiler_params=_cparams(2),
        name="gather_groups",
    )(page_table, cache, new_pad)


def _snsa_kernel(pt_ref, qc_ref, qr_ref, gate_ref, kc_ref, vc_ref, kwb_ref, vwb_ref, ksn_ref, vsn_ref,
                 kwn_ref, vwn_ref, ks_ref, vs_ref, o_ref, qs_ref, sel_ref, oc_ref, ow_ref, m_ref, l_ref, acc_ref,
                 *, n_new, pos0, n_sel):
    del pt_ref
    p = pl.program_id(1)
    heads = GROUP * GROUP
    rows = n_new * heads
    scale = HEAD_DIM ** -0.5
    page = ks_ref.shape[1]

    @pl.when(p == 0)
    def _():
        t_idx = lax.broadcasted_iota(jnp.int32, (rows, 1), 0) // heads
        q_pos = t_idx + pos0
        nc = kc_ref.shape[1]
        s = _nt_dot((qc_ref[0] * scale).astype(BF16), kc_ref[0].astype(BF16))
        cmp_end = lax.broadcasted_iota(jnp.int32, (rows, nc), 1) * CMP_STRIDE + (CMP_BLK - 1)
        mask = cmp_end <= q_pos
        s = jnp.where(mask, s, MASK_VALUE)
        pc = jnp.where(mask, jnp.exp(s - jnp.max(s, axis=1, keepdims=True)), 0.0)
        pc = pc / jnp.maximum(jnp.sum(pc, axis=1, keepdims=True), 1e-30)
        oc_ref[...] = _dot(pc.astype(BF16), vc_ref[0].astype(BF16))
        i_idx = lax.broadcasted_iota(jnp.int32, (rows, rows), 0)
        r_idx = lax.broadcasted_iota(jnp.int32, (rows, rows), 1)
        group_sum = jnp.where((r_idx // GROUP == i_idx) & (i_idx < n_new * GROUP), 1.0, 0.0).astype(BF16)
        p_sum = _dot_exact_lhs01(group_sum, pc)
        g_pos = lax.broadcasted_iota(jnp.int32, (rows, 1), 0) // GROUP + pos0
        sel = _select_blocks(p_sum, g_pos, n_sel, min(N_SELECT, n_sel))
        spread = jnp.where(i_idx // GROUP == r_idx, 1.0, 0.0).astype(BF16)
        sel_ref[...] = _dot(spread, sel.astype(BF16)).astype(BF16)
        qs_ref[...] = (qr_ref[0] * scale).astype(BF16)
        _reset_softmax(m_ref, l_ref, acc_ref)
        wlen = kwb_ref.shape[1]
        s = _nt_dot(qs_ref[...], kwb_ref[0].astype(BF16))
        dist = t_idx + wlen - lax.broadcasted_iota(jnp.int32, (rows, wlen), 1)
        _softmax_tile(s, (dist >= 0) & (dist < WINDOW), vwb_ref[0].astype(BF16), m_ref, l_ref, acc_ref)
        s = _nt_dot(qs_ref[...], kwn_ref[0].astype(BF16))
        s_idx = lax.broadcasted_iota(jnp.int32, (rows, NEW_PAD), 1)
        _softmax_tile(s, s_idx <= t_idx, vwn_ref[0].astype(BF16), m_ref, l_ref, acc_ref)
        ow_ref[...] = _finish_softmax(l_ref, acc_ref)
        _reset_softmax(m_ref, l_ref, acc_ref)
        s = _nt_dot(qs_ref[...], ksn_ref[0].astype(BF16))
        chosen = _expand_block_mask(sel_ref[...], pos0, NEW_PAD)
        m_new = jnp.where(s_idx <= t_idx, chosen, 0.0) > 0.5
        _softmax_tile(s, m_new, vsn_ref[0].astype(BF16), m_ref, l_ref, acc_ref)

    s = _nt_dot(qs_ref[...], ks_ref[0].astype(BF16))
    chosen = _expand_block_mask(sel_ref[...], p * page, page) > 0.5
    _softmax_tile(s, chosen, vs_ref[0].astype(BF16), m_ref, l_ref, acc_ref)

    @pl.when(p == pl.num_programs(1) - 1)
    def _():
        o_s = _finish_softmax(l_ref, acc_ref)
        gates = jax.nn.sigmoid(gate_ref[0])
        o_ref[0] = gates[:, 0:1] * oc_ref[...] + gates[:, 1:2] * o_s + gates[:, 2:3] * ow_ref[...]


def _group_rows(q):
    b, t, _ = q.shape
    qh = q.reshape(b, t, GROUP, GROUP, HEAD_DIM)
    eye = jnp.eye(GROUP, dtype=q.dtype)
    return jnp.einsum("btgrd,gk->btgrkd", qh, eye).reshape(b, t * GROUP * GROUP, GLANES)


def sample_nsa(page_table, pos0, qc, qr, gate_rows, kcmp, vcmp, kwin_buf, vwin_buf, ks_new, vs_new, kw_new,
               vw_new, ks_cache, vs_cache):
    b, rows, _ = qc.shape
    n_pages = page_table.shape[1]
    page = ks_cache.shape[1]
    n_new = rows // (GROUP * GROUP)
    assert pos0 == n_pages * page and pos0 % SEL_BLK == 0 and n_new <= SEL_BLK
    n_sel = -(-(pos0 + n_new) // SEL_BLK)
    nb = -(-n_sel // LANES) * LANES
    nc = kcmp.shape[1]
    wlen = kwin_buf.shape[1]
    per_b = lambda r, c: pl.BlockSpec((1, r, c), lambda bi, p, pt: (bi, 0, 0))
    page_spec = pl.BlockSpec((1, page, GLANES), lambda bi, p, pt: (pt[bi, p], 0, 0))
    return pl.pallas_call(
        functools.partial(_snsa_kernel, n_new=n_new, pos0=pos0, n_sel=n_sel),
        out_shape=jax.ShapeDtypeStruct((b, rows, GLANES), F32),
        grid_spec=pltpu.PrefetchScalarGridSpec(
            num_scalar_prefetch=1,
            grid=(b, n_pages),
            in_specs=[per_b(rows, GLANES), per_b(rows, GLANES), per_b(rows, SUBLANES), per_b(nc, GLANES),
                      per_b(nc, GLANES), per_b(wlen, GLANES), per_b(wlen, GLANES)]
            + [per_b(NEW_PAD, GLANES)] * 4 + [page_spec, page_spec],
            out_specs=per_b(rows, GLANES),
            scratch_shapes=[
                pltpu.VMEM((rows, GLANES), BF16),
                pltpu.VMEM((rows, nb), BF16),
                pltpu.VMEM((rows, GLANES), F32),
                pltpu.VMEM((rows, GLANES), F32),
                pltpu.VMEM((rows, 1), F32),
                pltpu.VMEM((rows, 1), F32),
                pltpu.VMEM((rows, GLANES), F32),
            ],
        ),
        compiler_params=_cparams(2),
        name="sample_nsa",
    )(page_table, qc, qr, gate_rows, kcmp, vcmp, kwin_buf, vwin_buf, ks_new, vs_new, kw_new, vw_new,
      ks_cache, vs_cache)


def sample_nsa_mixer(proj, gate, pos0, cos_t, sin_t, cache, page_table, p):
    b, t, _ = proj.shape
    pool, page = cache["nsa_ksel"].shape[:2]
    flat = lambda c: c.reshape(c.shape[0], c.shape[1], GLANES)
    q_rot = rope_cols(proj, 0, 1024, cos_t, sin_t)
    ks = rope_cols(proj, 1536, 256, cos_t, sin_t)
    kw = rope_cols(proj, 2048, 256, cos_t, sin_t)
    kc, vc = proj[..., 1024:1280], proj[..., 1280:1536]
    vs, vw = proj[..., 1792:2048], proj[..., 2304:2560]

    def summaries(cache_rows, new_rows, idx):
        rows = gather_groups(flat(cache_rows), page_table, new_rows)
        cmp = compress_rows(rows, p["nsa_cmp_pe"][idx], p["nsa_cmp_w1"][idx], p["nsa_cmp_b1"][idx],
                            p["nsa_cmp_w2"][idx])
        cmp = cmp.transpose(0, 2, 1, 3).reshape(b, cmp.shape[2], GLANES)
        return _pad_rows(cmp, -(-cmp.shape[1] // LANES) * LANES)

    kcmp = summaries(cache["nsa_kcmp"], kc, 0)
    vcmp = summaries(cache["nsa_vcmp"], vc, 1)
    heads = GROUP * GROUP
    gate_rows = gate[..., :3 * heads].reshape(b, t, 3, heads).transpose(0, 1, 3, 2).reshape(b, t * heads, 3)
    gate_rows = jnp.pad(gate_rows, ((0, 0), (0, 0), (0, SUBLANES - 3)))
    newp = lambda a: _pad_rows(a, NEW_PAD)
    o_rows = sample_nsa(page_table, pos0, _group_rows(proj[..., :1024]), _group_rows(q_rot), gate_rows, kcmp, vcmp,
                        flat(cache["nsa_kwin"]), flat(cache["nsa_vwin"]), newp(ks), newp(vs), newp(kw), newp(vw),
                        flat(cache["nsa_ksel"]), flat(cache["nsa_vsel"]))
    o6 = o_rows.reshape(b, t, GROUP, GROUP, GROUP, HEAD_DIM)
    o = jnp.stack([o6[:, :, g, :, g, :] for g in range(GROUP)], axis=2).reshape(b, t, heads * HEAD_DIM)
    g4 = lambda a: a.reshape(b, t, GROUP, HEAD_DIM)
    wlen = min(WINDOW, pos0 + t)
    kw_all = jnp.concatenate([cache["nsa_kwin"], g4(kw)], axis=1)
    vw_all = jnp.concatenate([cache["nsa_vwin"], g4(vw)], axis=1)
    return o, (g4(kc), g4(vc), g4(ks), g4(vs)), (kw_all[:, -wlen:], vw_all[:, -wlen:])


def _rmsnorm_kernel(x_ref, g_ref, o_ref):
    x = x_ref[...]
    o_ref[...] = x * lax.rsqrt(jnp.mean(x * x, axis=-1, keepdims=True) + NORM_EPS) * g_ref[...]


def rmsnorm_rows(x, gain):
    n, d = x.shape
    tm = _row_tile(n, 1024)
    return pl.pallas_call(
        _rmsnorm_kernel,
        out_shape=jax.ShapeDtypeStruct((n, d), F32),
        grid=(n // tm,),
        in_specs=[pl.BlockSpec((tm, d), lambda i: (i, 0)), pl.BlockSpec((1, d), lambda i: (0, 0))],
        out_specs=pl.BlockSpec((tm, d), lambda i: (i, 0)),
        compiler_params=_cparams(1),
        name="rmsnorm",
    )(x, gain.reshape(1, d))


D_MODEL = 1024
DEPTH = 4
D_FF = 2816
CONV_W = 3
X_HEADS = 4
X_HDIM = 128


def _pad_cols(w, n):
    return jnp.pad(w, ((0, 0), (0, n - w.shape[1])))


def _pad_vec(v, n):
    return jnp.pad(v, (0, n - v.shape[0]))


def _to_groups(x):
    b, t, _ = x.shape
    return x.reshape(b, t, GROUP, HEAD_DIM).transpose(0, 2, 1, 3)


def prepare_weights(p):
    w = {}
    hd = FORGET_HEADS * HEAD_DIM
    w["fox_main"] = p["fox_w_in"][:, :3 * hd].astype(BF16)
    w["fox_gate"] = _pad_cols(p["fox_w_in"][:, 3 * hd:], LANES).astype(BF16)
    w["fox_bias"] = _pad_vec(p["fox_b_f"], LANES)
    w["fox_out"] = p["fox_w_out"].astype(BF16)
    w["sb_in"] = p["sb_w_in"].astype(BF16)
    w["sb_out"] = p["sb_w_out"].astype(BF16)
    nsa_main = hd + 6 * GLANES
    w["nsa_main"] = p["nsa_w_in"][:, :nsa_main].astype(BF16)
    w["nsa_gate"] = _pad_cols(p["nsa_w_in"][:, nsa_main:], LANES).astype(BF16)
    w["nsa_out"] = p["nsa_w_out"].astype(BF16)
    w["diff_in"] = p["diff_w_in"].astype(BF16)
    w["diff_out"] = p["diff_w_out"].astype(BF16)
    w["x_q"] = p["x_w_q"].astype(BF16)
    w["x_o"] = p["x_w_o"].astype(BF16)
    w["ffn_up"] = p["ffn_w_up"].astype(BF16)
    w["ffn_down"] = p["ffn_w_down"].astype(BF16)
    w["conv_pack"] = jnp.concatenate(
        [p["ffn_conv_w"], p["ffn_conv_b"][:, None, :], jnp.zeros((DEPTH, SUBLANES - CONV_W - 1, 2 * D_FF), F32)],
        axis=1)
    return w


def _diff_lambda(p, lambda_init):
    return (jnp.exp(jnp.sum(p["diff_lq1"] * p["diff_lk1"])) - jnp.exp(jnp.sum(p["diff_lq2"] * p["diff_lk2"]))
            + lambda_init)


def _lambda_init(layer):
    return 0.8 - 0.6 * math.exp(-0.3 * layer)


def _layer_tail(x, layer, p, w, mem_k, mem_v, state8):
    b, t, d = x.shape
    x = cross_block(x, p["norm_mem"][layer], w["x_q"][layer], mem_k[layer].reshape(b, -1, X_HEADS * X_HDIM),
                    mem_v[layer].reshape(b, -1, X_HEADS * X_HDIM), w["x_o"][layer])
    x2 = x.reshape(b * t, d)
    u = matmul(x2, w["ffn_up"][layer], gain=p["norm_ffn"][layer]).reshape(b, t, 2 * D_FF)
    act = conv_gate(u, state8, w["conv_pack"][layer])
    x = matmul(act.reshape(b * t, D_FF), w["ffn_down"][layer], res=x2).reshape(b, t, d)
    return x, u


def prompt_trunk(x, mem_k, mem_v, p, w):
    b, t, d = x.shape
    n = b * t
    new = {}
    conv_rows = []
    cos_t, sin_t = rope_tables(jnp.arange(t))
    zero_state = jnp.zeros((b, SUBLANES, 2 * D_FF), F32)
    for layer in range(DEPTH):
        kind = layer % 4
        x2 = x.reshape(n, d)
        gain = p["norm_mix"][layer]
        if kind == 0:
            proj = matmul(x2, w["fox_main"], gain=gain).reshape(b, t, -1)
            gate = matmul(x2, w["fox_gate"], gain=gain).reshape(b, t, LANES)
            logf, cum = logf_cumsum(gate, w["fox_bias"])
            ck = cum[..., :FORGET_HEADS].reshape(b, t, GROUP, GROUP).transpose(0, 2, 3, 1)
            ck = jnp.pad(ck, ((0, 0), (0, 0), (0, SUBLANES - GROUP), (0, 0)))
            o = prompt_flash("fox", proj, 0, proj, 4, proj, 8, ck=ck)
            x = matmul(o.reshape(n, -1), w["fox_out"], res=x2).reshape(b, t, d)
            new["fox_k"] = proj[..., 1024:2048].reshape(b, t, 16, 64)
            new["fox_v"] = proj[..., 2048:3072].reshape(b, t, 16, 64)
            new["fox_logf"] = logf[..., :FORGET_HEADS]
        elif kind == 1:
            proj = matmul(x2, w["sb_in"], gain=gain).reshape(b, t, -1)
            o = prompt_sb(proj)
            x = matmul(o.reshape(n, -1), w["sb_out"], res=x2).reshape(b, t, d)
            new["sb_k"] = proj[..., 1024:2048].reshape(b, t, 16, 64)
            new["sb_v"] = proj[..., 2048:3072].reshape(b, t, 16, 64)
        elif kind == 2:
            proj = matmul(x2, w["nsa_main"], gain=gain).reshape(b, t, -1)
            gate = matmul(x2, w["nsa_gate"], gain=gain).reshape(b, t, LANES)
            q_rot = rope_cols(proj, 0, 1024, cos_t, sin_t)
            kc, vc = proj[..., 1024:1280], proj[..., 1280:1536]
            vs, vw = proj[..., 1792:2048], proj[..., 2304:2560]
            ks = rope_cols(proj, 1536, 256, cos_t, sin_t)
            kw = rope_cols(proj, 2048, 256, cos_t, sin_t)
            tp = -(-t // SEL_BLK) * SEL_BLK
            tpad = ((0, 0), (0, 0), (0, tp - t), (0, 0))
            kcmp = compress_rows(jnp.pad(_to_groups(kc), tpad), p["nsa_cmp_pe"][0], p["nsa_cmp_w1"][0],
                                 p["nsa_cmp_b1"][0], p["nsa_cmp_w2"][0])
            vcmp = compress_rows(jnp.pad(_to_groups(vc), tpad), p["nsa_cmp_pe"][1], p["nsa_cmp_w1"][1],
                                 p["nsa_cmp_b1"][1], p["nsa_cmp_w2"][1])
            ncp = -(-kcmp.shape[2] // LANES) * LANES
            cpad = ((0, 0), (0, 0), (0, ncp - kcmp.shape[2]), (0, 0))
            o = prompt_nsa(proj, q_rot, gate, jnp.pad(kcmp, cpad), jnp.pad(vcmp, cpad),
                           _to_groups(ks), _to_groups(vs), _to_groups(kw), _to_groups(vw))
            x = matmul(o.reshape(n, -1), w["nsa_out"], res=x2).reshape(b, t, d)
            g4 = lambda a: a.reshape(b, t, GROUP, HEAD_DIM)
            new["nsa_kcmp"], new["nsa_vcmp"], new["nsa_ksel"], new["nsa_vsel"] = g4(kc), g4(vc), g4(ks), g4(vs)
            wlen = min(WINDOW, t)
            new["nsa_kwin"], new["nsa_vwin"] = g4(kw)[:, t - wlen:], g4(vw)[:, t - wlen:]
        else:
            proj = matmul(x2, w["diff_in"], gain=gain).reshape(b, t, -1)
            qk = rope_cols(proj, 0, 2048, cos_t, sin_t)
            li = _lambda_init(layer)
            o = prompt_flash("diff", qk, 0, qk, 4, proj, 8, lam=_diff_lambda(p, li), subln=p["diff_subln"],
                             out_scale=1.0 - li)
            x = matmul(o.reshape(n, -1), w["diff_out"], res=x2).reshape(b, t, d)
            new["diff_k"] = qk[..., 1024:2048].reshape(b, t, 16, 64)
            new["diff_v"] = proj[..., 2048:3072].reshape(b, t, 8, 128)
        x, u = _layer_tail(x, layer, p, w, mem_k, mem_v, zero_state)
        conv_rows.append(u[:, t - (CONV_W - 1):])
    new["ffn_conv"] = jnp.stack(conv_rows)
    y = rmsnorm_rows(x.reshape(n, d), p["norm_final"]).reshape(b, t, d)
    return y, new


def sample_trunk(x, pos0, mem_k, mem_v, conv_state, cache, page_table, p, w):
    b, t, d = x.shape
    n = b * t
    new = {}
    conv_rows = []
    cos_t, sin_t = rope_tables(pos0 + jnp.arange(t))
    pool = cache["fox_k"].shape[0]
    page = cache["fox_k"].shape[1]
    flat = lambda c: c.reshape(pool, page, -1)
    for layer in range(DEPTH):
        kind = layer % 4
        x2 = x.reshape(n, d)
        gain = p["norm_mix"][layer]
        if kind == 0:
            proj = matmul(x2, w["fox_main"], gain=gain).reshape(b, t, -1)
            gate = matmul(x2, w["fox_gate"], gain=gain).reshape(b, t, LANES)
            logf, cum = logf_cumsum(gate, w["fox_bias"])
            new_cum_t = _pad_rows(cum[..., :FORGET_HEADS], NEW_PAD).transpose(0, 2, 1)
            o = sample_attn("fox", page_table, proj[..., :1024], proj[..., 1024:2048], proj[..., 2048:3072],
                            flat(cache["fox_k"]), flat(cache["fox_v"]),
                            lf_cache_t=cache["fox_logf"].transpose(0, 2, 1), new_cum_t=new_cum_t)
            x = matmul(o.reshape(n, -1), w["fox_out"], res=x2).reshape(b, t, d)
            new["fox_k"] = proj[..., 1024:2048].reshape(b, t, 16, 64)
            new["fox_v"] = proj[..., 2048:3072].reshape(b, t, 16, 64)
            new["fox_logf"] = logf[..., :FORGET_HEADS]
        elif kind == 1:
            proj = matmul(x2, w["sb_in"], gain=gain).reshape(b, t, -1)
            o = sample_attn("sb", page_table, proj[..., :1024], proj[..., 1024:2048], proj[..., 2048:3072],
                            flat(cache["sb_k"]), flat(cache["sb_v"]))
            x = matmul(o.reshape(n, -1), w["sb_out"], res=x2).reshape(b, t, d)
            new["sb_k"] = proj[..., 1024:2048].reshape(b, t, 16, 64)
            new["sb_v"] = proj[..., 2048:3072].reshape(b, t, 16, 64)
        elif kind == 2:
            proj = matmul(x2, w["nsa_main"], gain=gain).reshape(b, t, -1)
            gate = matmul(x2, w["nsa_gate"], gain=gain).reshape(b, t, LANES)
            o, rows, win = sample_nsa_mixer(proj, gate, pos0, cos_t, sin_t, cache, page_table, p)
            x = matmul(o.reshape(n, -1), w["nsa_out"], res=x2).reshape(b, t, d)
            new["nsa_kcmp"], new["nsa_vcmp"], new["nsa_ksel"], new["nsa_vsel"] = rows
            new["nsa_kwin"], new["nsa_vwin"] = win
        else:
            proj = matmul(x2, w["diff_in"], gain=gain).reshape(b, t, -1)
            qk = rope_cols(proj, 0, 2048, cos_t, sin_t)
            li = _lambda_init(layer)
            o = sample_attn("diff", page_table, qk[..., :1024], qk[..., 1024:2048], proj[..., 2048:3072],
                            flat(cache["diff_k"]), flat(cache["diff_v"]),
                            lam=_diff_lambda(p, li), subln=p["diff_subln"], out_scale=1.0 - li)
            x = matmul(o.reshape(n, -1), w["diff_out"], res=x2).reshape(b, t, d)
            new["diff_k"] = qk[..., 1024:2048].reshape(b, t, 16, 64)
            new["diff_v"] = proj[..., 2048:3072].reshape(b, t, 8, 128)
        state8 = jnp.pad(conv_state[layer], ((0, 0), (SUBLANES - (CONV_W - 1), 0), (0, 0)))
        x, u = _layer_tail(x, layer, p, w, mem_k, mem_v, state8)
        ext = jnp.concatenate([conv_state[layer], u], axis=1)
        conv_rows.append(ext[:, -(CONV_W - 1):])
    new["ffn_conv"] = jnp.stack(conv_rows)
    y = rmsnorm_rows(x.reshape(n, d), p["norm_final"]).reshape(b, t, d)
    return y, new


def memory_kv(mem, mem_norm, w_kv):
    b, m, d = mem.shape
    e = X_HEADS * X_HDIM
    kv = jnp.stack([matmul(mem.reshape(b * m, d), w_kv[layer].astype(BF16), gain=mem_norm[layer])
                    for layer in range(DEPTH)])
    xk = kv[..., :e].reshape(DEPTH, b, m, X_HEADS, X_HDIM)
    xv = kv[..., e:].reshape(DEPTH, b, m, X_HEADS, X_HDIM)
    return xk, xv


def kernel(x_prompt, x_sample, cache_fox_k, cache_fox_v, cache_fox_logf, cache_sb_k, cache_sb_v,
           cache_nsa_kcmp, cache_nsa_vcmp, cache_nsa_ksel, cache_nsa_vsel, state_nsa_kwin, state_nsa_vwin,
           cache_diff_k, cache_diff_v, cache_mem_k, cache_mem_v, state_ffn_conv, page_table, mem_prompt,
           norm_mix, norm_mem, norm_ffn, norm_final, fox_w_in, fox_b_f, fox_w_out, sb_w_in, sb_w_out,
           nsa_w_in, nsa_cmp_pe, nsa_cmp_w1, nsa_cmp_b1, nsa_cmp_w2, nsa_w_out,
           diff_w_in, diff_lq1, diff_lk1, diff_lq2, diff_lk2, diff_subln, diff_w_out,
           mem_norm, x_w_q, x_w_kv, x_w_o, ffn_w_up, ffn_conv_w, ffn_conv_b, ffn_w_down):
    p = {
        "norm_mix": norm_mix, "norm_mem": norm_mem, "norm_ffn": norm_ffn, "norm_final": norm_final,
        "fox_w_in": fox_w_in, "fox_b_f": fox_b_f, "fox_w_out": fox_w_out,
        "sb_w_in": sb_w_in, "sb_w_out": sb_w_out,
        "nsa_w_in": nsa_w_in, "nsa_cmp_pe": nsa_cmp_pe, "nsa_cmp_w1": nsa_cmp_w1, "nsa_cmp_b1": nsa_cmp_b1,
        "nsa_cmp_w2": nsa_cmp_w2, "nsa_w_out": nsa_w_out,
        "diff_w_in": diff_w_in, "diff_lq1": diff_lq1, "diff_lk1": diff_lk1, "diff_lq2": diff_lq2,
        "diff_lk2": diff_lk2, "diff_subln": diff_subln, "diff_w_out": diff_w_out,
        "x_w_q": x_w_q, "x_w_o": x_w_o,
        "ffn_w_up": ffn_w_up, "ffn_conv_w": ffn_conv_w, "ffn_conv_b": ffn_conv_b, "ffn_w_down": ffn_w_down,
    }
    w = prepare_weights(p)
    mem_k_p, mem_v_p = memory_kv(mem_prompt, mem_norm, x_w_kv)
    y_prompt, sp = prompt_trunk(x_prompt, mem_k_p, mem_v_p, p, w)
    cache = {
        "fox_k": cache_fox_k, "fox_v": cache_fox_v, "fox_logf": cache_fox_logf,
        "sb_k": cache_sb_k, "sb_v": cache_sb_v,
        "nsa_kcmp": cache_nsa_kcmp, "nsa_vcmp": cache_nsa_vcmp, "nsa_ksel": cache_nsa_ksel,
        "nsa_vsel": cache_nsa_vsel, "nsa_kwin": state_nsa_kwin, "nsa_vwin": state_nsa_vwin,
        "diff_k": cache_diff_k, "diff_v": cache_diff_v,
    }
    past_len = page_table.shape[1] * cache_fox_k.shape[1]
    y_sample, ss = sample_trunk(x_sample, past_len, cache_mem_k, cache_mem_v, state_ffn_conv, cache, page_table,
                                p, w)
    return (y_prompt, y_sample,
            sp["fox_k"], sp["fox_v"], sp["fox_logf"], sp["sb_k"], sp["sb_v"],
            sp["nsa_kcmp"], sp["nsa_vcmp"], sp["nsa_ksel"], sp["nsa_vsel"], sp["nsa_kwin"], sp["nsa_vwin"],
            sp["diff_k"], sp["diff_v"], mem_k_p, mem_v_p, sp["ffn_conv"],
            ss["fox_k"], ss["fox_v"], ss["fox_logf"], ss["sb_k"], ss["sb_v"],
            ss["nsa_kcmp"], ss["nsa_vcmp"], ss["nsa_ksel"], ss["nsa_vsel"], ss["nsa_kwin"], ss["nsa_vwin"],
            ss["diff_k"], ss["diff_v"], ss["ffn_conv"])
```

```python
import functools
import math

import jax
import jax.numpy as jnp
from jax import lax
from jax.experimental import pallas as pl
from jax.experimental.pallas import tpu as pltpu

F32 = jnp.float32
BF16 = jnp.bfloat16

V7X_VMEM_BYTES = 64 * 1024 * 1024
LANES = 128
SUBLANES = 8
VMEM_LIMIT = V7X_VMEM_BYTES - 8 * 1024 * 1024

NORM_EPS = 1e-6
MASK_VALUE = -1e30
ROPE_THETA = 10000.0
FORGET_HEADS = 16
HEAD_DIM = 64
GROUP = 4
GLANES = GROUP * HEAD_DIM


def _cparams(n_axes):
    return pltpu.CompilerParams(dimension_semantics=("arbitrary",) * n_axes, vmem_limit_bytes=VMEM_LIMIT)


def _nt_dot(a, b):
    return lax.dot_general(a, b, (((1,), (1,)), ((), ())), preferred_element_type=F32)


def _dot(a, b):
    return jnp.dot(a, b, preferred_element_type=F32)


def _split3(x):
    hi = x.astype(BF16)
    r1 = x - hi.astype(F32)
    mid = r1.astype(BF16)
    lo = (r1 - mid.astype(F32)).astype(BF16)
    return hi, mid, lo


def _dot_exact_rhs01(x, m01):
    hi, mid, lo = _split3(x)
    return _dot(hi, m01) + _dot(mid, m01) + _dot(lo, m01)


def _dot_exact_lhs01(m01, x):
    hi, mid, lo = _split3(x)
    return _dot(m01, hi) + _dot(m01, mid) + _dot(m01, lo)


def _softplus(z):
    return jnp.maximum(z, 0.0) + jnp.log1p(jnp.exp(-jnp.abs(z)))


def _log_sigmoid(z):
    return jnp.minimum(z, 0.0) - jnp.log1p(jnp.exp(-jnp.abs(z)))


def _mm_kernel(*refs, prologue, has_res, stage):
    it = iter(refs)
    x_ref = next(it)
    if prologue == "norm":
        g_ref = next(it)
    elif prologue == "add_gelu":
        x2_ref, pb_ref = next(it), next(it)
    w_ref = next(it)
    r_ref = next(it) if has_res else None
    o_ref = next(it)
    xs_ref = next(it) if stage else None

    if stage:
        @pl.when(pl.program_id(1) == 0)
        def _():
            x = x_ref[...].astype(F32)
            if prologue == "norm":
                x = x * lax.rsqrt(jnp.mean(x * x, axis=-1, keepdims=True) + NORM_EPS)
                x = x * g_ref[...]
            elif prologue == "add_gelu":
                x = jax.nn.gelu(x + x2_ref[...] + pb_ref[...])
            xs_ref[...] = x.astype(BF16)
        xb = xs_ref[...]
    else:
        xb = x_ref[...]
    y = _dot(xb, w_ref[...])
    if has_res:
        y = y + r_ref[...]
    o_ref[...] = y.astype(o_ref.dtype)


def _row_tile(n, cap):
    t = min(n, cap)
    while n % t:
        t //= 2
    return t


def matmul(x, w, *, gain=None, add=None, pre_bias=None, res=None, out_dtype=F32, tm_cap=1024, tn_cap=512):
    n, k = x.shape
    e = w.shape[1]
    tm = _row_tile(n, tm_cap)
    tn = _row_tile(e, tn_cap)
    assert n % tm == 0 and e % tn == 0 and tm % SUBLANES == 0 and tn % LANES == 0, (n, e, tm, tn)
    prologue = "norm" if gain is not None else ("add_gelu" if add is not None else None)
    stage = prologue is not None or x.dtype != BF16
    ins = [x]
    specs = [pl.BlockSpec((tm, k), lambda i, j: (i, 0))]
    if prologue == "norm":
        ins.append(gain.reshape(1, k).astype(F32))
        specs.append(pl.BlockSpec((1, k), lambda i, j: (0, 0)))
    elif prologue == "add_gelu":
        ins += [add, pre_bias.reshape(1, k).astype(F32)]
        specs += [pl.BlockSpec((tm, k), lambda i, j: (i, 0)), pl.BlockSpec((1, k), lambda i, j: (0, 0))]
    ins.append(w)
    specs.append(pl.BlockSpec((k, tn), lambda i, j: (0, j)))
    if res is not None:
        ins.append(res)
        specs.append(pl.BlockSpec((tm, tn), lambda i, j: (i, j)))
    kern = functools.partial(_mm_kernel, prologue=prologue, has_res=res is not None, stage=stage)
    return pl.pallas_call(
        kern,
        out_shape=jax.ShapeDtypeStruct((n, e), out_dtype),
        grid=(n // tm, e // tn),
        in_specs=specs,
        out_specs=pl.BlockSpec((tm, tn), lambda i, j: (i, j)),
        scratch_shapes=[pltpu.VMEM((tm, k), BF16)] if stage else [],
        compiler_params=_cparams(2),
        name="matmul",
    )(*ins)


def _conv_gate_kernel(uv_ref, ug_ref, hv_ref, hg_ref, sv_ref, sg_ref, wv_ref, wg_ref, o_ref):
    first = pl.program_id(1) == 0

    def conv(u_ref, h_ref, s_ref, w_ref):
        u = u_ref[0]
        prev = jnp.where(first, s_ref[0], h_ref[0])
        p2 = prev[6:7, :]
        p1 = prev[7:8, :]
        row = lax.broadcasted_iota(jnp.int32, u.shape, 0)
        um1 = jnp.where(row == 0, p1, pltpu.roll(u, 1, 0))
        um2 = jnp.where(row == 0, p2, jnp.where(row == 1, p1, pltpu.roll(u, 2, 0)))
        w = w_ref[...]
        return w[3:4, :] + w[0:1, :] * um2 + w[1:2, :] * um1 + w[2:3, :] * u

    val = conv(uv_ref, hv_ref, sv_ref, wv_ref)
    gate = conv(ug_ref, hg_ref, sg_ref, wg_ref)
    o_ref[0] = (gate * jax.nn.sigmoid(gate) * val).astype(o_ref.dtype)


def conv_gate(u, state8, wpack):
    b, t, f2 = u.shape
    f = f2 // 2
    tf = f // 2
    assert tf % LANES == 0
    nf = f // tf
    tt = _row_tile(t, 512)
    hb = tt // SUBLANES

    def halo(i):
        return jnp.maximum(i * hb - 1, 0)

    return pl.pallas_call(
        _conv_gate_kernel,
        out_shape=jax.ShapeDtypeStruct((b, t, f), BF16),
        grid=(b, t // tt, nf),
        in_specs=[
            pl.BlockSpec((1, tt, tf), lambda bi, i, j: (bi, i, j)),
            pl.BlockSpec((1, tt, tf), lambda bi, i, j: (bi, i, nf + j)),
            pl.BlockSpec((1, SUBLANES, tf), lambda bi, i, j: (bi, halo(i), j)),
            pl.BlockSpec((1, SUBLANES, tf), lambda bi, i, j: (bi, halo(i), nf + j)),
            pl.BlockSpec((1, SUBLANES, tf), lambda bi, i, j: (bi, 0, j)),
            pl.BlockSpec((1, SUBLANES, tf), lambda bi, i, j: (bi, 0, nf + j)),
            pl.BlockSpec((SUBLANES, tf), lambda bi, i, j: (0, j)),
            pl.BlockSpec((SUBLANES, tf), lambda bi, i, j: (0, nf + j)),
        ],
        out_specs=pl.BlockSpec((1, tt, tf), lambda bi, i, j: (bi, i, j)),
        compiler_params=_cparams(3),
        name="conv_gate",
    )(u, u, u, u, state8, state8, wpack, wpack)


def _cross_kernel(x_ref, g_ref, wq_ref, mk_ref, mv_ref, wo_ref, o_ref, *, heads, hdim):
    x = x_ref[0]
    h = x * lax.rsqrt(jnp.mean(x * x, axis=-1, keepdims=True) + NORM_EPS) * g_ref[...]
    q = _dot(h.astype(BF16), wq_ref[...]) * (hdim ** -0.5)
    outs = []
    for hh in range(heads):
        sl = slice(hh * hdim, (hh + 1) * hdim)
        s = _nt_dot(q[:, sl].astype(BF16), mk_ref[0, :, sl].astype(BF16))
        p = jnp.exp(s - jnp.max(s, axis=-1, keepdims=True))
        p = p / jnp.sum(p, axis=-1, keepdims=True)
        outs.append(_dot(p.astype(BF16), mv_ref[0, :, sl].astype(BF16)))
    o = jnp.concatenate(outs, axis=-1)
    o_ref[0] = x + _dot(o.astype(BF16), wo_ref[...])


def cross_block(x, gain, wq, mk, mv, wo, *, heads=4, hdim=128):
    b, t, d = x.shape
    m = mk.shape[1]
    e = heads * hdim
    tt = _row_tile(t, 512)
    return pl.pallas_call(
        functools.partial(_cross_kernel, heads=heads, hdim=hdim),
        out_shape=jax.ShapeDtypeStruct((b, t, d), F32),
        grid=(b, t // tt),
        in_specs=[
            pl.BlockSpec((1, tt, d), lambda bi, i: (bi, i, 0)),
            pl.BlockSpec((1, d), lambda bi, i: (0, 0)),
            pl.BlockSpec((d, e), lambda bi, i: (0, 0)),
            pl.BlockSpec((1, m, e), lambda bi, i: (bi, 0, 0)),
            pl.BlockSpec((1, m, e), lambda bi, i: (bi, 0, 0)),
            pl.BlockSpec((e, d), lambda bi, i: (0, 0)),
        ],
        out_specs=pl.BlockSpec((1, tt, d), lambda bi, i: (bi, i, 0)),
        compiler_params=_cparams(2),
        name="cross_block",
    )(x, gain.reshape(1, d), wq, mk, mv, wo)


def rope_tables(pos):
    half = HEAD_DIM // 2
    inv = ROPE_THETA ** (-jnp.arange(half, dtype=F32) / half)
    ang = pos.astype(F32)[:, None] * inv[None, :]
    cos = jnp.cos(ang)
    sin = jnp.sin(ang)
    cos_t = jnp.tile(jnp.concatenate([cos, cos], axis=-1), (1, GROUP))
    sin_t = jnp.tile(jnp.concatenate([-sin, sin], axis=-1), (1, GROUP))
    return cos_t, sin_t


def _rope_apply(x, cos_t, sin_t):
    half = HEAD_DIM // 2
    lane = lax.broadcasted_iota(jnp.int32, x.shape, 1)
    first = (lane % HEAD_DIM) < half
    n = x.shape[1]
    swapped = jnp.where(first, pltpu.roll(x, n - half, 1), pltpu.roll(x, half, 1))
    return x * cos_t + swapped * sin_t


def _rope_kernel(x_ref, c_ref, s_ref, o_ref):
    o_ref[0] = _rope_apply(x_ref[0], c_ref[...], s_ref[...])


def rope_cols(x, col0, width, cos_t, sin_t):
    b, t, _ = x.shape
    assert col0 % GLANES == 0 and width % GLANES == 0
    tt = _row_tile(t, 512)
    c0 = col0 // GLANES
    return pl.pallas_call(
        _rope_kernel,
        out_shape=jax.ShapeDtypeStruct((b, t, width), F32),
        grid=(b, t // tt, width // GLANES),
        in_specs=[
            pl.BlockSpec((1, tt, GLANES), lambda bi, i, j: (bi, i, c0 + j)),
            pl.BlockSpec((tt, GLANES), lambda bi, i, j: (i, 0)),
            pl.BlockSpec((tt, GLANES), lambda bi, i, j: (i, 0)),
        ],
        out_specs=pl.BlockSpec((1, tt, GLANES), lambda bi, i, j: (bi, i, j)),
        compiler_params=_cparams(3),
        name="rope",
    )(x, cos_t, sin_t)


def _logf_cumsum_kernel(g_ref, b_ref, lf_ref, c_ref, cp_ref, *, chunk):
    t = g_ref.shape[1]
    row = lax.broadcasted_iota(jnp.int32, (chunk, chunk), 0)
    col = lax.broadcasted_iota(jnp.int32, (chunk, chunk), 1)
    tri = jnp.where(col <= row, 1.0, 0.0).astype(BF16)
    head_lane = lax.broadcasted_iota(jnp.int32, (chunk, g_ref.shape[2]), 1) < FORGET_HEADS

    def body(i, carry):
        sl = pl.ds(pl.multiple_of(i * chunk, chunk), chunk)
        lf = _log_sigmoid(g_ref[0, sl, :] + b_ref[...])
        lf_ref[0, sl, :] = lf
        c = _dot_exact_lhs01(tri, lf) + carry
        c_ref[0, sl, :] = c
        hi, mid, lo = _split3(jnp.where(head_lane, c, 0.0))
        placed = (hi.astype(F32) + pltpu.roll(mid.astype(F32), FORGET_HEADS, 1)
                  + pltpu.roll(lo.astype(F32), 2 * FORGET_HEADS, 1))
        cp_ref[0, sl, :] = placed.astype(BF16)
        return c[chunk - 1:chunk, :]

    lax.fori_loop(0, t // chunk, body, jnp.zeros((1, g_ref.shape[2]), F32))


def logf_cumsum(gate, bias):
    b, t, n = gate.shape
    chunk = _row_tile(t, 256)
    spec = pl.BlockSpec((1, t, n), lambda bi: (bi, 0, 0))
    return pl.pallas_call(
        functools.partial(_logf_cumsum_kernel, chunk=chunk),
        out_shape=(jax.ShapeDtypeStruct((b, t, n), F32), jax.ShapeDtypeStruct((b, t, n), F32),
                   jax.ShapeDtypeStruct((b, t, n), BF16)),
        grid=(b,),
        in_specs=[spec, pl.BlockSpec((1, n), lambda bi: (0, 0))],
        out_specs=(spec, spec, spec),
        compiler_params=_cparams(1),
        name="logf_cumsum",
    )(gate, bias.reshape(1, n))


def _stack_heads(q):
    lane = lax.broadcasted_iota(jnp.int32, q.shape, 1) // HEAD_DIM
    return jnp.concatenate([jnp.where(lane == r, q, 0.0).astype(BF16) for r in range(GROUP)], axis=0)


def _causal_mask(tq, tk, q0, k0, strict):
    row = lax.broadcasted_iota(jnp.int32, (GROUP * tq, tk), 0) % tq + q0
    col = lax.broadcasted_iota(jnp.int32, (GROUP * tq, tk), 1) + k0
    return col < row if strict else col <= row


def _softmax_tile(s, mask, v, m_ref, l_ref, acc_ref):
    if mask is not None:
        s = jnp.where(mask, s, MASK_VALUE)
    m_prev = m_ref[...]
    m_new = jnp.maximum(m_prev, jnp.max(s, axis=1, keepdims=True))
    alpha = jnp.exp(m_prev - m_new)
    p = jnp.exp(s - m_new)
    if mask is not None:
        p = jnp.where(mask, p, 0.0)
    l_ref[...] = alpha * l_ref[...] + jnp.sum(p, axis=1, keepdims=True)
    acc_ref[...] = alpha * acc_ref[...] + _dot(p.astype(BF16), v)
    m_ref[...] = m_new


def _pflash_kernel(*refs, kind, tq, tk, out_scale):
    if kind == "fox":
        q_ref, k_ref, v_ref, ck_ref, o_ref, m_ref, l_ref, acc_ref = refs
    else:
        q_ref, k_ref, v_ref, lam_ref, sub_ref, o_ref, m_ref, l_ref, acc_ref = refs
    qi = pl.program_id(2)
    q4 = _stack_heads(q_ref[0] * (HEAD_DIM ** -0.5))
    m_ref[...] = jnp.full(m_ref.shape, MASK_VALUE, F32)
    l_ref[...] = jnp.zeros(l_ref.shape, F32)
    acc_ref[...] = jnp.zeros(acc_ref.shape, F32)

    def tile(kt, masked):
        ks = pl.ds(pl.multiple_of(kt * tk, tk), tk)
        s = _nt_dot(q4, k_ref[0, ks, :].astype(BF16))
        if kind == "fox":
            ck = ck_ref[0, 0, :, ks]
            s = s - jnp.concatenate(
                [jnp.broadcast_to(ck[r:r + 1, :], (tq, tk)) for r in range(GROUP)], axis=0)
        mask = _causal_mask(tq, tk, qi * tq, kt * tk, False) if masked else None
        _softmax_tile(s, mask, v_ref[0, ks, :].astype(BF16), m_ref, l_ref, acc_ref)

    def body(kt, c):
        tile(kt, False)
        return c

    lax.fori_loop(0, qi, body, 0)
    tile(qi, True)

    inv = 1.0 / jnp.maximum(l_ref[...], 1e-30)
    lane = lax.broadcasted_iota(jnp.int32, (tq, GLANES), 1)
    blk = [acc_ref[r * tq:(r + 1) * tq, :] * inv[r * tq:(r + 1) * tq, :] for r in range(GROUP)]
    if kind == "fox":
        out = jnp.zeros((tq, GLANES), F32)
        for r in range(GROUP):
            out = jnp.where(lane // HEAD_DIM == r, blk[r], out)
        o_ref[0] = out
    else:
        lam = lam_ref[0, 0]
        halves = []
        for hh in range(2):
            d = (blk[2 * hh] - lam * blk[2 * hh + 1])[:, hh * LANES:(hh + 1) * LANES]
            y = d * lax.rsqrt(jnp.mean(d * d, axis=-1, keepdims=True) + NORM_EPS) * sub_ref[...]
            halves.append(y * out_scale)
        o_ref[0] = jnp.concatenate(halves, axis=1)


def prompt_flash(kind, qa, qcol, ka, kcol, va, vcol, *, ck=None, lam=None, subln=None, out_scale=1.0, tile=256):
    b, t, _ = qa.shape
    tq = tk = _row_tile(t, tile)
    ngroups = 4
    ins = [qa, ka, va]
    specs = [
        pl.BlockSpec((1, tq, GLANES), lambda bi, g, i: (bi, i, qcol + g)),
        pl.BlockSpec((1, t, GLANES), lambda bi, g, i: (bi, 0, kcol + g)),
        pl.BlockSpec((1, t, GLANES), lambda bi, g, i: (bi, 0, vcol + g)),
    ]
    if kind == "fox":
        ins.append(ck)
        specs.append(pl.BlockSpec((1, 1, SUBLANES, t), lambda bi, g, i: (bi, g, 0, 0)))
    else:
        ins += [lam.reshape(1, 1), subln.reshape(1, LANES)]
        specs += [pl.BlockSpec(memory_space=pltpu.SMEM), pl.BlockSpec((1, LANES), lambda bi, g, i: (0, 0))]
    return pl.pallas_call(
        functools.partial(_pflash_kernel, kind=kind, tq=tq, tk=tk, out_scale=out_scale),
        out_shape=jax.ShapeDtypeStruct((b, t, ngroups * GLANES), F32),
        grid=(b, ngroups, t // tq),
        in_specs=specs,
        out_specs=pl.BlockSpec((1, tq, GLANES), lambda bi, g, i: (bi, i, g)),
        scratch_shapes=[
            pltpu.VMEM((GROUP * tq, 1), F32),
            pltpu.VMEM((GROUP * tq, 1), F32),
            pltpu.VMEM((GROUP * tq, GLANES), F32),
        ],
        compiler_params=_cparams(3),
        name="prompt_flash_" + kind,
    )(*ins)


def _strict_lower_ones(n):
    row = lax.broadcasted_iota(jnp.int32, (n, n), 0)
    col = lax.broadcasted_iota(jnp.int32, (n, n), 1)
    return jnp.where(row > col, 1.0, 0.0).astype(BF16)


def _sb_tile(z, mask, v, tri, carry_ref, acc_ref):
    sp = _softplus(z)
    log_rest = -sp
    if mask is not None:
        log_rest = jnp.where(mask, log_rest, 0.0)
    hi, lo = _split2(log_rest)
    after = carry_ref[...] + _dot(hi, tri) + _dot(lo, tri)
    a = jnp.exp(z - sp + after)
    if mask is not None:
        a = jnp.where(mask, a, 0.0)
    acc_ref[...] += _dot(a.astype(BF16), v)
    carry_ref[...] = after[:, 0:1] + log_rest[:, 0:1]


def _psb_kernel(q_ref, k_ref, v_ref, o_ref, carry_ref, acc_ref, *, tq, tk):
    qi = pl.program_id(2)
    q4 = _stack_heads(q_ref[0] * (HEAD_DIM ** -0.5))
    carry_ref[...] = jnp.zeros(carry_ref.shape, F32)
    acc_ref[...] = jnp.zeros(acc_ref.shape, F32)
    tri = _strict_lower_ones(tk)

    def tile(kt, masked):
        ks = pl.ds(pl.multiple_of(kt * tk, tk), tk)
        z = _nt_dot(q4, k_ref[0, ks, :].astype(BF16))
        mask = _causal_mask(tq, tk, qi * tq, kt * tk, True) if masked else None
        _sb_tile(z, mask, v_ref[0, ks, :].astype(BF16), tri, carry_ref, acc_ref)

    tile(qi, True)

    def body(i, c):
        tile(qi - 1 - i, False)
        return c

    lax.fori_loop(0, qi, body, 0)
    lane = lax.broadcasted_iota(jnp.int32, (tq, GLANES), 1)
    out = jnp.zeros((tq, GLANES), F32)
    for r in range(GROUP):
        out = jnp.where(lane // HEAD_DIM == r, acc_ref[r * tq:(r + 1) * tq, :], out)
    o_ref[0] = out


def prompt_sb(proj, *, tile=256):
    b, t, _ = proj.shape
    tq = tk = _row_tile(t, tile)
    ngroups = 4
    return pl.pallas_call(
        functools.partial(_psb_kernel, tq=tq, tk=tk),
        out_shape=jax.ShapeDtypeStruct((b, t, ngroups * GLANES), F32),
        grid=(b, ngroups, t // tq),
        in_specs=[
            pl.BlockSpec((1, tq, GLANES), lambda bi, g, i: (bi, i, g)),
            pl.BlockSpec((1, t, GLANES), lambda bi, g, i: (bi, 0, ngroups + g)),
            pl.BlockSpec((1, t, GLANES), lambda bi, g, i: (bi, 0, 2 * ngroups + g)),
        ],
        out_specs=pl.BlockSpec((1, tq, GLANES), lambda bi, g, i: (bi, i, g)),
        scratch_shapes=[
            pltpu.VMEM((GROUP * tq, 1), F32),
            pltpu.VMEM((GROUP * tq, GLANES), F32),
        ],
        compiler_params=_cparams(3),
        name="prompt_sb",
    )(proj, proj, proj)


CMP_STRIDE = 16
CMP_BLK = 32
SEL_BLK = 64
N_SELECT = 16
WINDOW = 512
SEL_FORCE = 1e4


def _stack_group_heads(q):
    return jnp.concatenate([q[:, r * HEAD_DIM:(r + 1) * HEAD_DIM] for r in range(GROUP)], axis=0).astype(BF16)


def _unstack_group_heads(o, t):
    return jnp.concatenate([o[r * t:(r + 1) * t, :] for r in range(GROUP)], axis=1)


def _select_blocks(p_sum, q_pos, n_sel, n_top):
    t, nc = p_sum.shape
    nb = -(-n_sel // LANES) * LANES
    c_idx = lax.broadcasted_iota(jnp.int32, (nc, nb), 0) * CMP_STRIDE
    j_idx = lax.broadcasted_iota(jnp.int32, (nc, nb), 1)
    overlap = (c_idx < j_idx * SEL_BLK + SEL_BLK) & (c_idx + CMP_BLK > j_idx * SEL_BLK) & (j_idx < n_sel)
    imp = _dot_exact_rhs01(p_sum, jnp.where(overlap, 1.0, 0.0).astype(BF16))
    blk = lax.broadcasted_iota(jnp.int32, (t, nb), 1)
    cur = q_pos // SEL_BLK
    forced = (blk == 0) | (blk == cur) | (blk == cur - 1)
    valid = blk * SEL_BLK <= q_pos
    score = jnp.where(forced, SEL_FORCE, jnp.where(valid, imp, -1.0))
    score = jnp.where(blk < n_sel, score, -3e38)
    rows = -(-n_sel // SUBLANES) * SUBLANES
    sc_t = score.T[:rows, :]
    j_iota = lax.broadcasted_iota(jnp.int32, sc_t.shape, 0)
    rank = jnp.zeros(sc_t.shape, F32)
    for jp in range(n_sel):
        other = sc_t[jp:jp + 1, :]
        ge = jnp.where(other >= sc_t, 1.0, 0.0)
        gt = jnp.where(other > sc_t, 1.0, 0.0)
        rank = rank + jnp.where(j_iota > jp, ge, gt)
    sel_t = jnp.where(rank < n_top, 1.0, 0.0)
    if rows < nb:
        sel_t = jnp.concatenate([sel_t, jnp.zeros((nb - rows, t), F32)], axis=0)
    return sel_t.T


def _expand_block_mask(sel, k0, tk):
    nb = sel.shape[1]
    j = lax.broadcasted_iota(jnp.int32, (nb, tk), 0)
    key = lax.broadcasted_iota(jnp.int32, (nb, tk), 1) + k0
    e = jnp.where(j == key // SEL_BLK, 1.0, 0.0).astype(BF16)
    return _dot(sel.astype(BF16), e)


def _finish_softmax(l_ref, acc_ref):
    return acc_ref[...] / jnp.maximum(l_ref[...], 1e-30)


def _reset_softmax(m_ref, l_ref, acc_ref):
    m_ref[...] = jnp.full(m_ref.shape, MASK_VALUE, F32)
    l_ref[...] = jnp.zeros(l_ref.shape, F32)
    acc_ref[...] = jnp.zeros(acc_ref.shape, F32)


def _expand_gates(gates, branch, g, width):
    c = lax.broadcasted_iota(jnp.int32, (LANES, width), 0)
    lane = lax.broadcasted_iota(jnp.int32, (LANES, width), 1)
    e = jnp.where(c == branch * (GROUP * GROUP) + g * GROUP + lane // HEAD_DIM, 1.0, 0.0).astype(BF16)
    return _dot_exact_rhs01(gates, e)


def _pnsa_kernel(qc_ref, qr_ref, gate_ref, kc_ref, vc_ref, ks_ref, vs_ref, kw_ref, vw_ref, o_ref,
                 m_ref, l_ref, acc_ref, *, tq, tk, n_sel):
    g = pl.program_id(1)
    qi = pl.program_id(2)
    scale = HEAD_DIM ** -0.5
    q0 = qi * tq
    rows4 = GROUP * tq
    q_pos4 = lax.broadcasted_iota(jnp.int32, (rows4, 1), 0) % tq + q0

    qc = _stack_group_heads(qc_ref[0] * scale)
    nc = kc_ref.shape[2]
    s = _nt_dot(qc, kc_ref[0, 0].astype(BF16))
    cmp_end = lax.broadcasted_iota(jnp.int32, (rows4, nc), 1) * CMP_STRIDE + (CMP_BLK - 1)
    mask = cmp_end <= q_pos4
    s = jnp.where(mask, s, MASK_VALUE)
    p = jnp.where(mask, jnp.exp(s - jnp.max(s, axis=1, keepdims=True)), 0.0)
    p = p / jnp.maximum(jnp.sum(p, axis=1, keepdims=True), 1e-30)
    o_c = _dot(p.astype(BF16), vc_ref[0, 0].astype(BF16))
    p_sum = p[0:tq] + p[tq:2 * tq] + p[2 * tq:3 * tq] + p[3 * tq:4 * tq]
    sel = _select_blocks(p_sum, q_pos4[0:tq], n_sel, min(N_SELECT, n_sel))

    qr = _stack_group_heads(qr_ref[0] * scale)

    _reset_softmax(m_ref, l_ref, acc_ref)

    def sel_tile(kt, c):
        ks = pl.ds(pl.multiple_of(kt * tk, tk), tk)
        s = _nt_dot(qr, ks_ref[0, 0, ks, :].astype(BF16))
        chosen = _expand_block_mask(sel, kt * tk, tk)
        kpos = lax.broadcasted_iota(jnp.int32, (tq, tk), 1) + kt * tk
        m1 = jnp.where(kpos <= q_pos4[0:tq], chosen, 0.0)
        m4 = jnp.concatenate([m1] * GROUP, axis=0) > 0.5
        _softmax_tile(s, m4, vs_ref[0, 0, ks, :].astype(BF16), m_ref, l_ref, acc_ref)
        return c

    lax.fori_loop(0, qi + 1, sel_tile, 0)
    o_s = _finish_softmax(l_ref, acc_ref)

    _reset_softmax(m_ref, l_ref, acc_ref)

    def win_tile(kt, c):
        ks = pl.ds(pl.multiple_of(kt * tk, tk), tk)
        s = _nt_dot(qr, kw_ref[0, 0, ks, :].astype(BF16))
        dist = q_pos4 - (lax.broadcasted_iota(jnp.int32, (rows4, tk), 1) + kt * tk)
        m4 = (dist >= 0) & (dist < WINDOW)
        _softmax_tile(s, m4, vw_ref[0, 0, ks, :].astype(BF16), m_ref, l_ref, acc_ref)
        return c

    kt_lo = jnp.maximum(q0 - (WINDOW - 1), 0) // tk
    lax.fori_loop(kt_lo, qi + 1, win_tile, 0)
    o_w = _finish_softmax(l_ref, acc_ref)

    gates = jax.nn.sigmoid(gate_ref[0])
    out = (_expand_gates(gates, 0, g, GLANES) * _unstack_group_heads(o_c, tq)
           + _expand_gates(gates, 1, g, GLANES) * _unstack_group_heads(o_s, tq)
           + _expand_gates(gates, 2, g, GLANES) * _unstack_group_heads(o_w, tq))
    o_ref[0] = out


def prompt_nsa(proj, q_rot, gate, kcmp, vcmp, ksel, vsel, kwin, vwin, *, tile=256):
    b, t, _ = proj.shape
    tq = tk = _row_tile(t, tile)
    nc = kcmp.shape[2]
    n_sel = t // SEL_BLK
    assert t % SEL_BLK == 0 and n_sel <= LANES and nc % LANES == 0
    kv_spec = pl.BlockSpec((1, 1, t, HEAD_DIM), lambda bi, g, i: (bi, g, 0, 0))
    cmp_spec = pl.BlockSpec((1, 1, nc, HEAD_DIM), lambda bi, g, i: (bi, g, 0, 0))
    return pl.pallas_call(
        functools.partial(_pnsa_kernel, tq=tq, tk=tk, n_sel=n_sel),
        out_shape=jax.ShapeDtypeStruct((b, t, GROUP * GLANES), F32),
        grid=(b, GROUP, t // tq),
        in_specs=[
            pl.BlockSpec((1, tq, GLANES), lambda bi, g, i: (bi, i, g)),
            pl.BlockSpec((1, tq, GLANES), lambda bi, g, i: (bi, i, g)),
            pl.BlockSpec((1, tq, LANES), lambda bi, g, i: (bi, i, 0)),
            cmp_spec, cmp_spec, kv_spec, kv_spec, kv_spec, kv_spec,
        ],
        out_specs=pl.BlockSpec((1, tq, GLANES), lambda bi, g, i: (bi, i, g)),
        scratch_shapes=[
            pltpu.VMEM((GROUP * tq, 1), F32),
            pltpu.VMEM((GROUP * tq, 1), F32),
            pltpu.VMEM((GROUP * tq, HEAD_DIM), F32),
        ],
        compiler_params=_cparams(3),
        name="prompt_nsa",
    )(proj, q_rot, gate, kcmp, vcmp, ksel, vsel, kwin, vwin)


def compress_rows(rows, pe, w1, b1, w2):
    b, g, tp, d = rows.shape
    n = tp // CMP_STRIDE
    hid = w1.shape[-1]
    a = rows.reshape(b * g * n, CMP_STRIDE * d)
    w1f = w1.reshape(CMP_BLK * d, hid)
    w1cat = jnp.concatenate([w1f[:CMP_STRIDE * d], w1f[CMP_STRIDE * d:]], axis=1).astype(BF16)
    h = matmul(a, w1cat).reshape(b, g, n, 2 * hid)
    h_first = h[..., :hid]
    h_second = jnp.concatenate([h[:, :, 1:, hid:], jnp.zeros((b, g, 1, hid), F32)], axis=2)
    bias = b1 + jnp.einsum("k,kh->h", pe.reshape(-1), w1f, precision=lax.Precision.HIGHEST)
    w2p = jnp.pad(w2, ((0, 0), (0, LANES - d))).astype(BF16)
    out = matmul(h_first.reshape(b * g * n, hid), w2p, add=h_second.reshape(b * g * n, hid), pre_bias=bias)
    return out[:, :d].reshape(b, g, n, d)


def _split2(x):
    hi = x.astype(BF16)
    lo = (x - hi.astype(F32)).astype(BF16)
    return hi, lo


def _masked_q_t(q_t):
    row = lax.broadcasted_iota(jnp.int32, q_t.shape, 0) // HEAD_DIM
    return jnp.concatenate([jnp.where(row == r, q_t, 0.0).astype(BF16) for r in range(GROUP)], axis=1)


def _causal_mask_t(tk, tq, k0, q0, strict):
    key = lax.broadcasted_iota(jnp.int32, (tk, GROUP * tq), 0) + k0
    qry = lax.broadcasted_iota(jnp.int32, (tk, GROUP * tq), 1) % tq + q0
    return key < qry if strict else key <= qry


def _softmax_step_t(s, mask, m, l):
    if mask is not None:
        s = jnp.where(mask, s, MASK_VALUE)
    m_new = jnp.maximum(m, jnp.max(s, axis=0, keepdims=True))
    alpha = jnp.exp(m - m_new)
    p = jnp.exp(s - m_new)
    if mask is not None:
        p = jnp.where(mask, p, 0.0)
    return p, alpha, m_new, alpha * l + jnp.sum(p, axis=0, keepdims=True)


def _pflash_t_kernel(*refs, kind, tq, tk, vrows, out_scale):
    if kind == "fox":
        q_ref, k_ref, vt_ref, cp_ref, o_ref, acc_ref = refs
    else:
        q_ref, k_ref, vt_ref, lam_ref, sub_ref, o_ref, acc_ref = refs
    g = pl.program_id(1)
    qi = pl.program_id(2)
    n = GROUP * tq
    q4t = _masked_q_t((q_ref[0] * (HEAD_DIM ** -0.5)).T)
    if kind == "fox":
        piece_row = lax.broadcasted_iota(jnp.int32, (LANES, n), 0)
        head = lax.broadcasted_iota(jnp.int32, (LANES, n), 1) // tq + g * GROUP
        cind = jnp.where((piece_row < 3 * FORGET_HEADS) & (piece_row % FORGET_HEADS == head), -1.0, 0.0).astype(BF16)
    acc_ref[...] = jnp.zeros(acc_ref.shape, F32)

    def tile(kt, masked, m, l):
        ks = pl.ds(pl.multiple_of(kt * tk, tk), tk)
        s = _dot(k_ref[0, ks, :].astype(BF16), q4t)
        if kind == "fox":
            s = s + _dot(cp_ref[0, ks, :], cind)
        mask = _causal_mask_t(tk, tq, kt * tk, qi * tq, False) if masked else None
        p, alpha, m, l = _softmax_step_t(s, mask, m, l)
        pb = p.astype(BF16)
        for r in range(GROUP):
            v0 = (r * HEAD_DIM // vrows) * vrows
            cols = slice(r * tq, (r + 1) * tq)
            acc_ref[r] = alpha[:, cols] * acc_ref[r] + _dot(vt_ref[0, v0:v0 + vrows, ks].astype(BF16), pb[:, cols])
        return m, l

    init = (jnp.full((1, n), MASK_VALUE, F32), jnp.zeros((1, n), F32))
    m, l = lax.fori_loop(0, qi, lambda kt, c: tile(kt, False, *c), init)
    m, l = tile(qi, True, m, l)

    inv = 1.0 / jnp.maximum(l, 1e-30)
    blk = [acc_ref[r] * inv[:, r * tq:(r + 1) * tq] for r in range(GROUP)]
    if kind == "fox":
        o_t = jnp.concatenate(blk, axis=0)
    else:
        lam = lam_ref[0, 0]
        halves = []
        for hh in range(2):
            d = blk[2 * hh] - lam * blk[2 * hh + 1]
            y = d * lax.rsqrt(jnp.mean(d * d, axis=0, keepdims=True) + NORM_EPS) * sub_ref[...]
            halves.append(y * out_scale)
        o_t = jnp.concatenate(halves, axis=0)
    o_ref[0] = o_t.T


def prompt_flash(kind, qa, qcol, ka, kcol, v_t, *, cp=None, lam=None, subln=None, out_scale=1.0, tile=256):
    b, t, _ = qa.shape
    tq = tk = _row_tile(t, tile)
    vrows = HEAD_DIM if kind == "fox" else LANES
    ins = [qa, ka, v_t]
    specs = [
        pl.BlockSpec((1, tq, GLANES), lambda bi, g, i: (bi, i, qcol + g)),
        pl.BlockSpec((1, t, GLANES), lambda bi, g, i: (bi, 0, kcol + g)),
        pl.BlockSpec((1, GLANES, t), lambda bi, g, i: (bi, g, 0)),
    ]
    if kind == "fox":
        ins.append(cp)
        specs.append(pl.BlockSpec((1, t, LANES), lambda bi, g, i: (bi, 0, 0)))
    else:
        ins += [lam.reshape(1, 1), subln.reshape(LANES, 1)]
        specs += [pl.BlockSpec(memory_space=pltpu.SMEM), pl.BlockSpec((LANES, 1), lambda bi, g, i: (0, 0))]
    return pl.pallas_call(
        functools.partial(_pflash_t_kernel, kind=kind, tq=tq, tk=tk, vrows=vrows, out_scale=out_scale),
        out_shape=jax.ShapeDtypeStruct((b, t, GROUP * GLANES), F32),
        grid=(b, GROUP, t // tq),
        in_specs=specs,
        out_specs=pl.BlockSpec((1, tq, GLANES), lambda bi, g, i: (bi, i, g)),
        scratch_shapes=[pltpu.VMEM((GROUP, vrows, tq), F32)],
        compiler_params=_cparams(3),
        name="prompt_flash_" + kind,
    )(*ins)


def _psb_t_kernel(q_ref, k_ref, vt_ref, o_ref, acc_ref, *, tq, tk):
    qi = pl.program_id(2)
    n = GROUP * tq
    q4t = _masked_q_t((q_ref[0] * (HEAD_DIM ** -0.5)).T)
    acc_ref[...] = jnp.zeros(acc_ref.shape, F32)
    row = lax.broadcasted_iota(jnp.int32, (tk, tk), 0)
    col = lax.broadcasted_iota(jnp.int32, (tk, tk), 1)
    later = jnp.where(col > row, 1.0, 0.0).astype(BF16)

    def tile(kt, masked, carry):
        ks = pl.ds(pl.multiple_of(kt * tk, tk), tk)
        z = _dot(k_ref[0, ks, :].astype(BF16), q4t)
        sp = _softplus(z)
        log_rest = -sp
        mask = _causal_mask_t(tk, tq, kt * tk, qi * tq, True) if masked else None
        if masked:
            log_rest = jnp.where(mask, log_rest, 0.0)
        hi, lo = _split2(log_rest)
        after = carry + _dot(later, hi) + _dot(later, lo)
        a = jnp.exp(z - sp + after)
        if masked:
            a = jnp.where(mask, a, 0.0)
        ab = a.astype(BF16)
        for r in range(GROUP):
            cols = slice(r * tq, (r + 1) * tq)
            acc_ref[r] += _dot(vt_ref[0, r * HEAD_DIM:(r + 1) * HEAD_DIM, ks].astype(BF16), ab[:, cols])
        return carry + jnp.sum(log_rest, axis=0, keepdims=True)

    carry = tile(qi, True, jnp.zeros((1, n), F32))
    lax.fori_loop(0, qi, lambda i, c: tile(qi - 1 - i, False, c), carry)
    o_ref[0] = jnp.concatenate([acc_ref[r] for r in range(GROUP)], axis=0).T


def prompt_sb(proj, v_t, *, tile=256):
    b, t, _ = proj.shape
    tq = tk = _row_tile(t, tile)
    return pl.pallas_call(
        functools.partial(_psb_t_kernel, tq=tq, tk=tk),
        out_shape=jax.ShapeDtypeStruct((b, t, GROUP * GLANES), F32),
        grid=(b, GROUP, t // tq),
        in_specs=[
            pl.BlockSpec((1, tq, GLANES), lambda bi, g, i: (bi, i, g)),
            pl.BlockSpec((1, t, GLANES), lambda bi, g, i: (bi, 0, GROUP + g)),
            pl.BlockSpec((1, GLANES, t), lambda bi, g, i: (bi, g, 0)),
        ],
        out_specs=pl.BlockSpec((1, tq, GLANES), lambda bi, g, i: (bi, i, g)),
        scratch_shapes=[pltpu.VMEM((GROUP, HEAD_DIM, tq), F32)],
        compiler_params=_cparams(3),
        name="prompt_sb",
    )(proj, proj, v_t)


def _group_q_t(q):
    q_t = q.T
    return jnp.concatenate([q_t[r * HEAD_DIM:(r + 1) * HEAD_DIM, :] for r in range(GROUP)], axis=1).astype(BF16)


def _select_blocks_t(p_sum_t, q_pos, n_sel, n_top):
    nc, t = p_sum_t.shape
    nb = -(-n_sel // LANES) * LANES
    j_idx = lax.broadcasted_iota(jnp.int32, (nb, nc), 0)
    c_idx = lax.broadcasted_iota(jnp.int32, (nb, nc), 1) * CMP_STRIDE
    overlap = (c_idx < j_idx * SEL_BLK + SEL_BLK) & (c_idx + CMP_BLK > j_idx * SEL_BLK) & (j_idx < n_sel)
    imp = _dot_exact_lhs01(jnp.where(overlap, 1.0, 0.0).astype(BF16), p_sum_t)
    rows = -(-n_sel // SUBLANES) * SUBLANES
    imp = imp[:rows]
    blk = lax.broadcasted_iota(jnp.int32, (rows, t), 0)
    cur = q_pos // SEL_BLK
    forced = (blk == 0) | (blk == cur) | (blk == cur - 1)
    valid = blk * SEL_BLK <= q_pos
    score = jnp.where(forced, SEL_FORCE, jnp.where(valid, imp, -1.0))
    score = jnp.where(blk < n_sel, score, -3e38)
    rank = jnp.zeros(score.shape, F32)
    for jp in range(n_sel):
        other = score[jp:jp + 1, :]
        ge = jnp.where(other >= score, 1.0, 0.0)
        gt = jnp.where(other > score, 1.0, 0.0)
        rank = rank + jnp.where(blk > jp, ge, gt)
    sel = jnp.where(rank < n_top, 1.0, 0.0)
    if rows < nb:
        sel = jnp.concatenate([sel, jnp.zeros((nb - rows, t), F32)], axis=0)
    return sel


def _pnsa_t_kernel(qc_ref, qr_ref, gate_ref, kc_ref, vct_ref, ks_ref, vst_ref, kw_ref, vwt_ref, o_ref, acc_ref,
                   *, tq, tk, n_sel):
    qi = pl.program_id(2)
    scale = HEAD_DIM ** -0.5
    q0 = qi * tq
    n = GROUP * tq
    q_pos = lax.broadcasted_iota(jnp.int32, (1, n), 1) % tq + q0

    qc = _group_q_t(qc_ref[0] * scale)
    nc = kc_ref.shape[2]
    s = _dot(kc_ref[0, 0].astype(BF16), qc)
    cmp_end = lax.broadcasted_iota(jnp.int32, (nc, n), 0) * CMP_STRIDE + (CMP_BLK - 1)
    mask = cmp_end <= q_pos
    s = jnp.where(mask, s, MASK_VALUE)
    p = jnp.where(mask, jnp.exp(s - jnp.max(s, axis=0, keepdims=True)), 0.0)
    p = p / jnp.maximum(jnp.sum(p, axis=0, keepdims=True), 1e-30)
    o_c = _dot(vct_ref[0, 0].astype(BF16), p.astype(BF16))
    p_sum = p[:, 0:tq] + p[:, tq:2 * tq] + p[:, 2 * tq:3 * tq] + p[:, 3 * tq:4 * tq]
    sel = _select_blocks_t(p_sum, q_pos[:, 0:tq], n_sel, min(N_SELECT, n_sel)).astype(BF16)
    nb = sel.shape[0]

    qr = _group_q_t(qr_ref[0] * scale)
    init = (jnp.full((1, n), MASK_VALUE, F32), jnp.zeros((1, n), F32))
    acc_ref[...] = jnp.zeros(acc_ref.shape, F32)

    def sel_tile(kt, c):
        m, l = c
        ks = pl.ds(pl.multiple_of(kt * tk, tk), tk)
        s = _dot(ks_ref[0, 0, ks, :].astype(BF16), qr)
        key = lax.broadcasted_iota(jnp.int32, (tk, nb), 0) + kt * tk
        blk = lax.broadcasted_iota(jnp.int32, (tk, nb), 1)
        chosen = _dot(jnp.where(blk == key // SEL_BLK, 1.0, 0.0).astype(BF16), sel)
        kpos = lax.broadcasted_iota(jnp.int32, (tk, tq), 0) + kt * tk
        m1 = jnp.where(kpos <= q_pos[:, 0:tq], chosen, 0.0)
        m4 = jnp.concatenate([m1] * GROUP, axis=1) > 0.5
        p, alpha, m, l = _softmax_step_t(s, m4, m, l)
        acc_ref[...] = alpha * acc_ref[...] + _dot(vst_ref[0, :, ks].astype(BF16), p.astype(BF16))
        return m, l

    m, l = lax.fori_loop(0, qi + 1, sel_tile, init)
    o_s = acc_ref[...] / jnp.maximum(l, 1e-30)
    acc_ref[...] = jnp.zeros(acc_ref.shape, F32)

    def win_tile(kt, c):
        m, l = c
        ks = pl.ds(pl.multiple_of(kt * tk, tk), tk)
        s = _dot(kw_ref[0, 0, ks, :].astype(BF16), qr)
        dist = q_pos - (lax.broadcasted_iota(jnp.int32, (tk, n), 0) + kt * tk)
        m4 = (dist >= 0) & (dist < WINDOW)
        p, alpha, m, l = _softmax_step_t(s, m4, m, l)
        acc_ref[...] = alpha * acc_ref[...] + _dot(vwt_ref[0, :, ks].astype(BF16), p.astype(BF16))
        return m, l

    kt_lo = jnp.maximum(q0 - (WINDOW - 1), 0) // tk
    m, l = lax.fori_loop(kt_lo, qi + 1, win_tile, init)
    o_w = acc_ref[...] / jnp.maximum(l, 1e-30)

    gates = jax.nn.sigmoid(gate_ref[0, 0])
    outs = []
    for r in range(GROUP):
        cols = slice(r * tq, (r + 1) * tq)
        outs.append(gates[r:r + 1, :] * o_c[:, cols] + gates[GROUP + r:GROUP + r + 1, :] * o_s[:, cols]
                    + gates[2 * GROUP + r:2 * GROUP + r + 1, :] * o_w[:, cols])
    o_ref[0] = jnp.concatenate(outs, axis=0).T


def prompt_nsa(proj, q_rot, gate_t, kcmp, vcmp_t, ksel, vsel_t, kwin, vwin_t, *, tile=256):
    b, t, _ = proj.shape
    tq = tk = _row_tile(t, tile)
    nc = kcmp.shape[2]
    n_sel = t // SEL_BLK
    assert t % SEL_BLK == 0 and nc % LANES == 0
    k_spec = pl.BlockSpec((1, 1, t, HEAD_DIM), lambda bi, g, i: (bi, g, 0, 0))
    vt_spec = pl.BlockSpec((1, HEAD_DIM, t), lambda bi, g, i: (bi, g, 0))
    return pl.pallas_call(
        functools.partial(_pnsa_t_kernel, tq=tq, tk=tk, n_sel=n_sel),
        out_shape=jax.ShapeDtypeStruct((b, t, GROUP * GLANES), F32),
        grid=(b, GROUP, t // tq),
        in_specs=[
            pl.BlockSpec((1, tq, GLANES), lambda bi, g, i: (bi, i, g)),
            pl.BlockSpec((1, tq, GLANES), lambda bi, g, i: (bi, i, g)),
            pl.BlockSpec((1, 1, 4 * GROUP, tq), lambda bi, g, i: (bi, g, 0, i)),
            pl.BlockSpec((1, 1, nc, HEAD_DIM), lambda bi, g, i: (bi, g, 0, 0)),
            pl.BlockSpec((1, 1, HEAD_DIM, nc), lambda bi, g, i: (bi, g, 0, 0)),
            k_spec, vt_spec, k_spec, vt_spec,
        ],
        out_specs=pl.BlockSpec((1, tq, GLANES), lambda bi, g, i: (bi, i, g)),
        scratch_shapes=[pltpu.VMEM((HEAD_DIM, GROUP * tq), F32)],
        compiler_params=_cparams(3),
        name="prompt_nsa",
    )(proj, q_rot, gate_t, kcmp, vcmp_t, ksel, vsel_t, kwin, vwin_t)


NEW_PAD = 16
STEP_PAGES = 4


def _rows_tile(x, reps):
    return jnp.concatenate([x] * reps, axis=0)


def _sattn_kernel(*refs, kind, n_new, heads, vdim, out_scale, step_pages):
    it = iter(refs)
    pt_ref = next(it)
    q_ref, kn_ref, vn_ref = (next(it) for _ in range(3))
    k_refs = [next(it) for _ in range(step_pages)]
    v_refs = [next(it) for _ in range(step_pages)]
    if kind == "fox":
        lf_refs = [next(it) for _ in range(step_pages)]
        nc_ref = next(it)
    if kind == "diff":
        lam_ref, sub_ref = next(it), next(it)
    o_ref = next(it)
    qs_ref = next(it)
    if kind == "sb":
        carry_ref, acc_ref = next(it), next(it)
    else:
        m_ref, l_ref, acc_ref = next(it), next(it), next(it)
    if kind == "fox":
        later_ref = next(it)
    del pt_ref
    p = pl.program_id(1)
    rows = n_new * heads
    page = k_refs[0].shape[1]

    @pl.when(p == 0)
    def _():
        qs_ref[...] = (q_ref[0] * (HEAD_DIM ** -0.5)).astype(BF16)
        acc_ref[...] = jnp.zeros(acc_ref.shape, F32)
        s = _nt_dot(qs_ref[...], kn_ref[0].astype(BF16))
        t_idx = lax.broadcasted_iota(jnp.int32, (rows, NEW_PAD), 0) // heads
        s_idx = lax.broadcasted_iota(jnp.int32, (rows, NEW_PAD), 1)
        vn = vn_ref[0].astype(BF16)
        if kind == "sb":
            carry_ref[...] = jnp.zeros(carry_ref.shape, F32)
            _sb_tile(s, s_idx < t_idx, vn, _strict_lower_ones(NEW_PAD), carry_ref, acc_ref)
        else:
            m_ref[...] = jnp.full(m_ref.shape, MASK_VALUE, F32)
            l_ref[...] = jnp.zeros(l_ref.shape, F32)
            if kind == "fox":
                later_ref[...] = jnp.zeros(later_ref.shape, F32)
                s = s - _rows_tile(nc_ref[0], n_new)
            _softmax_tile(s, s_idx <= t_idx, vn, m_ref, l_ref, acc_ref)

    scores = [_nt_dot(qs_ref[...], k_refs[j][0]) for j in range(step_pages)]
    tri = _strict_lower_ones(page)
    if kind == "sb":
        carry = carry_ref[...]
        pv = jnp.zeros(acc_ref.shape, F32)
        for j in range(step_pages):
            sp = _softplus(scores[j])
            log_rest = -sp
            hi, lo = _split2(log_rest)
            after = carry + _dot(hi, tri) + _dot(lo, tri)
            pv = pv + _dot(jnp.exp(scores[j] - sp + after).astype(BF16), v_refs[j][0])
            carry = after[:, 0:1] + log_rest[:, 0:1]
        acc_ref[...] += pv
        carry_ref[...] = carry
    else:
        if kind == "fox":
            later = later_ref[...]
            for j in range(step_pages):
                lf = lf_refs[j][0]
                suffix = _dot_exact_rhs01(lf, tri)
                scores[j] = scores[j] + _rows_tile(suffix + later, n_new)
                later = later + suffix[:, 0:1] + lf[:, 0:1]
            later_ref[...] = later
        s = jnp.concatenate(scores, axis=1)
        m_prev = m_ref[...]
        m_new = jnp.maximum(m_prev, jnp.max(s, axis=1, keepdims=True))
        alpha = jnp.exp(m_prev - m_new)
        pr = jnp.exp(s - m_new)
        l_ref[...] = alpha * l_ref[...] + jnp.sum(pr, axis=1, keepdims=True)
        pb = pr.astype(BF16)
        pv = jnp.zeros(acc_ref.shape, F32)
        for j in range(step_pages):
            pv = pv + _dot(pb[:, j * page:(j + 1) * page], v_refs[j][0])
        acc_ref[...] = alpha * acc_ref[...] + pv
        m_ref[...] = m_new

    @pl.when(p == pl.num_programs(1) - 1)
    def _():
        width = acc_ref.shape[1]
        h_idx = lax.broadcasted_iota(jnp.int32, (heads, width), 0)
        lane = lax.broadcasted_iota(jnp.int32, (heads, width), 1)
        own = lane // vdim == (h_idx * HEAD_DIM) // vdim
        if kind == "sb":
            accn = acc_ref[...]
        else:
            accn = acc_ref[...] / jnp.maximum(l_ref[...], 1e-30)
        if kind == "diff":
            sign = jnp.where(h_idx % 2 == 0, 1.0, -lam_ref[0, 0])
        out_rows = []
        for t in range(n_new):
            blk = accn[t * heads:(t + 1) * heads, :]
            if kind == "diff":
                blk = blk * sign
            out_rows.append(jnp.sum(jnp.where(own, blk, 0.0), axis=0, keepdims=True))
        out = jnp.concatenate(out_rows, axis=0)
        if kind == "diff":
            parts = []
            for hh in range(width // vdim):
                d = out[:, hh * vdim:(hh + 1) * vdim]
                parts.append(d * lax.rsqrt(jnp.mean(d * d, axis=-1, keepdims=True) + NORM_EPS) * sub_ref[...]
                             * out_scale)
            out = jnp.concatenate(parts, axis=1)
        o_ref[0] = out


def _block_diag_rows(q, heads):
    b, t, e = q.shape
    qh = q.reshape(b, t, heads, e // heads)
    eye = jnp.eye(heads, dtype=q.dtype)
    return jnp.einsum("bthd,hk->bthkd", qh, eye).reshape(b, t * heads, e)


def _pad_rows(x, n):
    return jnp.pad(x, ((0, 0), (0, n - x.shape[1]), (0, 0)))


def sample_attn(kind, page_table, q, k_new, v_new, k_cache, v_cache, *, lf_cache_t=None, new_cum_t=None,
                lam=None, subln=None, out_scale=1.0):
    b, n_new, e = q.shape
    heads = e // HEAD_DIM
    rows = n_new * heads
    n_pages = page_table.shape[1]
    page = k_cache.shape[1]
    vdim = LANES if kind == "diff" else HEAD_DIM
    last = n_pages - 1
    sp = STEP_PAGES
    assert n_pages % sp == 0 and k_cache.dtype == BF16 and v_cache.dtype == BF16

    def page_spec(j, r, c):
        return pl.BlockSpec((1, r, c), lambda bi, p, pt: (pt[bi, last - (p * sp + j)], 0, 0))

    ins = ([_block_diag_rows(q, heads), _pad_rows(k_new, NEW_PAD), _pad_rows(v_new, NEW_PAD)]
           + [k_cache] * sp + [v_cache] * sp)
    specs = [
        pl.BlockSpec((1, rows, e), lambda bi, p, pt: (bi, 0, 0)),
        pl.BlockSpec((1, NEW_PAD, e), lambda bi, p, pt: (bi, 0, 0)),
        pl.BlockSpec((1, NEW_PAD, e), lambda bi, p, pt: (bi, 0, 0)),
    ] + [page_spec(j, page, e) for j in range(sp)] * 2
    if kind == "fox":
        ins += [lf_cache_t] * sp + [new_cum_t]
        specs += [page_spec(j, heads, page) for j in range(sp)]
        specs.append(pl.BlockSpec((1, heads, NEW_PAD), lambda bi, p, pt: (bi, 0, 0)))
    if kind == "diff":
        ins += [lam.reshape(1, 1), subln.reshape(1, LANES)]
        specs += [pl.BlockSpec(memory_space=pltpu.SMEM), pl.BlockSpec((1, LANES), lambda bi, p, pt: (0, 0))]
    scratch = [pltpu.VMEM((rows, e), BF16)]
    if kind == "sb":
        scratch += [pltpu.VMEM((rows, 1), F32), pltpu.VMEM((rows, e), F32)]
    else:
        scratch += [pltpu.VMEM((rows, 1), F32), pltpu.VMEM((rows, 1), F32), pltpu.VMEM((rows, e), F32)]
    if kind == "fox":
        scratch.append(pltpu.VMEM((heads, 1), F32))
    return pl.pallas_call(
        functools.partial(_sattn_kernel, kind=kind, n_new=n_new, heads=heads, vdim=vdim, out_scale=out_scale,
                          step_pages=sp),
        out_shape=jax.ShapeDtypeStruct((b, n_new, e), F32),
        grid_spec=pltpu.PrefetchScalarGridSpec(
            num_scalar_prefetch=1,
            grid=(b, n_pages // sp),
            in_specs=specs,
            out_specs=pl.BlockSpec((1, n_new, e), lambda bi, p, pt: (bi, 0, 0)),
            scratch_shapes=scratch,
        ),
        compiler_params=_cparams(2),
        name="sample_attn_" + kind,
    )(page_table, *ins)


GATHER_PAGES = 8


def _gather_groups_kernel(pt_ref, *refs):
    del pt_ref
    page_refs, new_ref, o_ref = refs[:GATHER_PAGES], refs[GATHER_PAGES], refs[GATHER_PAGES + 1]
    page = new_ref.shape[1]
    is_new = pl.program_id(1) == pl.num_programs(1) - 1

    @pl.when(is_new)
    def _():
        o_ref[...] = jnp.zeros(o_ref.shape, F32)
        for g in range(GROUP):
            o_ref[0, g, 0:page, :] = new_ref[0, :, g * HEAD_DIM:(g + 1) * HEAD_DIM]

    @pl.when(jnp.logical_not(is_new))
    def _():
        for j in range(GATHER_PAGES):
            for g in range(GROUP):
                o_ref[0, g, j * page:(j + 1) * page, :] = page_refs[j][0, :, g * HEAD_DIM:(g + 1) * HEAD_DIM]


def gather_groups(cache, page_table, new_rows):
    b, n_pages = page_table.shape
    page = cache.shape[1]
    gp = GATHER_PAGES
    assert n_pages % gp == 0
    steps = n_pages // gp
    new_pad = _pad_rows(new_rows, page)

    def page_spec(j):
        return pl.BlockSpec((1, page, GLANES),
                            lambda bi, p, pt: (pt[bi, jnp.minimum(p, steps - 1) * gp + j], 0, 0))

    return pl.pallas_call(
        _gather_groups_kernel,
        out_shape=jax.ShapeDtypeStruct((b, GROUP, (n_pages + gp) * page, HEAD_DIM), F32),
        grid_spec=pltpu.PrefetchScalarGridSpec(
            num_scalar_prefetch=1,
            grid=(b, steps + 1),
            in_specs=[page_spec(j) for j in range(gp)]
            + [pl.BlockSpec((1, page, GLANES), lambda bi, p, pt: (bi, 0, 0))],
            out_specs=pl.BlockSpec((1, GROUP, gp * page, HEAD_DIM), lambda bi, p, pt: (bi, 0, p, 0)),
        ),
        compiler_params=_cparams(2),
        name="gather_groups",
    )(page_table, *([cache] * gp), new_pad)


def _snsa_kernel(*refs, n_new, pos0, n_sel, step_pages):
    (pt_ref, qc_ref, qr_ref, gate_ref, kc_ref, vc_ref, kwb_ref, vwb_ref, ksn_ref, vsn_ref, kwn_ref,
     vwn_ref) = refs[:12]
    ks_refs = refs[12:12 + step_pages]
    vs_refs = refs[12 + step_pages:12 + 2 * step_pages]
    o_ref, qs_ref, sel_ref, oc_ref, ow_ref, m_ref, l_ref, acc_ref = refs[12 + 2 * step_pages:]
    del pt_ref
    p = pl.program_id(1)
    heads = GROUP * GROUP
    rows = n_new * heads
    scale = HEAD_DIM ** -0.5
    page = ks_refs[0].shape[1]

    @pl.when(p == 0)
    def _():
        t_idx = lax.broadcasted_iota(jnp.int32, (rows, 1), 0) // heads
        q_pos = t_idx + pos0
        nc = kc_ref.shape[1]
        s = _nt_dot((qc_ref[0] * scale).astype(BF16), kc_ref[0].astype(BF16))
        cmp_end = lax.broadcasted_iota(jnp.int32, (rows, nc), 1) * CMP_STRIDE + (CMP_BLK - 1)
        mask = cmp_end <= q_pos
        s = jnp.where(mask, s, MASK_VALUE)
        pc = jnp.where(mask, jnp.exp(s - jnp.max(s, axis=1, keepdims=True)), 0.0)
        pc = pc / jnp.maximum(jnp.sum(pc, axis=1, keepdims=True), 1e-30)
        oc_ref[...] = _dot(pc.astype(BF16), vc_ref[0].astype(BF16))
        i_idx = lax.broadcasted_iota(jnp.int32, (rows, rows), 0)
        r_idx = lax.broadcasted_iota(jnp.int32, (rows, rows), 1)
        group_sum = jnp.where((r_idx // GROUP == i_idx) & (i_idx < n_new * GROUP), 1.0, 0.0).astype(BF16)
        p_sum = _dot_exact_lhs01(group_sum, pc)
        g_pos = lax.broadcasted_iota(jnp.int32, (rows, 1), 0) // GROUP + pos0
        sel = _select_blocks(p_sum, g_pos, n_sel, min(N_SELECT, n_sel))
        spread = jnp.where(i_idx // GROUP == r_idx, 1.0, 0.0).astype(BF16)
        sel_ref[...] = _dot(spread, sel.astype(BF16)).astype(BF16)
        qs_ref[...] = (qr_ref[0] * scale).astype(BF16)
        _reset_softmax(m_ref, l_ref, acc_ref)
        wlen = kwb_ref.shape[1]
        s = _nt_dot(qs_ref[...], kwb_ref[0].astype(BF16))
        dist = t_idx + wlen - lax.broadcasted_iota(jnp.int32, (rows, wlen), 1)
        _softmax_tile(s, (dist >= 0) & (dist < WINDOW), vwb_ref[0].astype(BF16), m_ref, l_ref, acc_ref)
        s = _nt_dot(qs_ref[...], kwn_ref[0].astype(BF16))
        s_idx = lax.broadcasted_iota(jnp.int32, (rows, NEW_PAD), 1)
        _softmax_tile(s, s_idx <= t_idx, vwn_ref[0].astype(BF16), m_ref, l_ref, acc_ref)
        ow_ref[...] = _finish_softmax(l_ref, acc_ref)
        _reset_softmax(m_ref, l_ref, acc_ref)
        s = _nt_dot(qs_ref[...], ksn_ref[0].astype(BF16))
        chosen = _expand_block_mask(sel_ref[...], pos0, NEW_PAD)
        m_new = jnp.where(s_idx <= t_idx, chosen, 0.0) > 0.5
        _softmax_tile(s, m_new, vsn_ref[0].astype(BF16), m_ref, l_ref, acc_ref)

    scores = [_nt_dot(qs_ref[...], ks_refs[j][0]) for j in range(step_pages)]
    chosen = [_expand_block_mask(sel_ref[...], (p * step_pages + j) * page, page) for j in range(step_pages)]
    mask = jnp.concatenate(chosen, axis=1) > 0.5
    s = jnp.where(mask, jnp.concatenate(scores, axis=1), MASK_VALUE)
    m_prev = m_ref[...]
    m_new = jnp.maximum(m_prev, jnp.max(s, axis=1, keepdims=True))
    alpha = jnp.exp(m_prev - m_new)
    pr = jnp.where(mask, jnp.exp(s - m_new), 0.0)
    l_ref[...] = alpha * l_ref[...] + jnp.sum(pr, axis=1, keepdims=True)
    pb = pr.astype(BF16)
    pv = jnp.zeros(acc_ref.shape, F32)
    for j in range(step_pages):
        pv = pv + _dot(pb[:, j * page:(j + 1) * page], vs_refs[j][0])
    acc_ref[...] = alpha * acc_ref[...] + pv
    m_ref[...] = m_new

    @pl.when(p == pl.num_programs(1) - 1)
    def _():
        o_s = _finish_softmax(l_ref, acc_ref)
        gates = jax.nn.sigmoid(gate_ref[0])
        o_ref[0] = gates[:, 0:1] * oc_ref[...] + gates[:, 1:2] * o_s + gates[:, 2:3] * ow_ref[...]


def _group_rows(q):
    b, t, _ = q.shape
    qh = q.reshape(b, t, GROUP, GROUP, HEAD_DIM)
    eye = jnp.eye(GROUP, dtype=q.dtype)
    return jnp.einsum("btgrd,gk->btgrkd", qh, eye).reshape(b, t * GROUP * GROUP, GLANES)


def sample_nsa(page_table, pos0, qc, qr, gate_rows, kcmp, vcmp, kwin_buf, vwin_buf, ks_new, vs_new, kw_new,
               vw_new, ks_cache, vs_cache):
    b, rows, _ = qc.shape
    n_pages = page_table.shape[1]
    page = ks_cache.shape[1]
    n_new = rows // (GROUP * GROUP)
    assert pos0 == n_pages * page and pos0 % SEL_BLK == 0 and n_new <= SEL_BLK
    n_sel = -(-(pos0 + n_new) // SEL_BLK)
    nb = -(-n_sel // LANES) * LANES
    nc = kcmp.shape[1]
    wlen = kwin_buf.shape[1]
    per_b = lambda r, c: pl.BlockSpec((1, r, c), lambda bi, p, pt: (bi, 0, 0))
    sp = STEP_PAGES
    assert n_pages % sp == 0 and ks_cache.dtype == BF16 and vs_cache.dtype == BF16

    def page_spec(j):
        return pl.BlockSpec((1, page, GLANES), lambda bi, p, pt: (pt[bi, p * sp + j], 0, 0))

    return pl.pallas_call(
        functools.partial(_snsa_kernel, n_new=n_new, pos0=pos0, n_sel=n_sel, step_pages=sp),
        out_shape=jax.ShapeDtypeStruct((b, rows, GLANES), F32),
        grid_spec=pltpu.PrefetchScalarGridSpec(
            num_scalar_prefetch=1,
            grid=(b, n_pages // sp),
            in_specs=[per_b(rows, GLANES), per_b(rows, GLANES), per_b(rows, SUBLANES), per_b(nc, GLANES),
                      per_b(nc, GLANES), per_b(wlen, GLANES), per_b(wlen, GLANES)]
            + [per_b(NEW_PAD, GLANES)] * 4 + [page_spec(j) for j in range(sp)] * 2,
            out_specs=per_b(rows, GLANES),
            scratch_shapes=[
                pltpu.VMEM((rows, GLANES), BF16),
                pltpu.VMEM((rows, nb), BF16),
                pltpu.VMEM((rows, GLANES), F32),
                pltpu.VMEM((rows, GLANES), F32),
                pltpu.VMEM((rows, 1), F32),
                pltpu.VMEM((rows, 1), F32),
                pltpu.VMEM((rows, GLANES), F32),
            ],
        ),
        compiler_params=_cparams(2),
        name="sample_nsa",
    )(page_table, qc, qr, gate_rows, kcmp, vcmp, kwin_buf, vwin_buf, ks_new, vs_new, kw_new, vw_new,
      *([ks_cache] * sp), *([vs_cache] * sp))


def sample_nsa_mixer(proj, gate, pos0, cos_t, sin_t, cache, page_table, p):
    b, t, _ = proj.shape
    pool, page = cache["nsa_ksel"].shape[:2]
    flat = lambda c: c.reshape(c.shape[0], c.shape[1], GLANES)
    q_rot = rope_cols(proj, 0, 1024, cos_t, sin_t)
    ks = rope_cols(proj, 1536, 256, cos_t, sin_t)
    kw = rope_cols(proj, 2048, 256, cos_t, sin_t)
    kc, vc = proj[..., 1024:1280], proj[..., 1280:1536]
    vs, vw = proj[..., 1792:2048], proj[..., 2304:2560]

    def summaries(cache_rows, new_rows, idx):
        rows = gather_groups(flat(cache_rows), page_table, new_rows)
        cmp = compress_rows(rows, p["nsa_cmp_pe"][idx], p["nsa_cmp_w1"][idx], p["nsa_cmp_b1"][idx],
                            p["nsa_cmp_w2"][idx])
        cmp = cmp.transpose(0, 2, 1, 3).reshape(b, cmp.shape[2], GLANES)
        return _pad_rows(cmp, -(-cmp.shape[1] // LANES) * LANES)

    kcmp = summaries(cache["nsa_kcmp"], kc, 0)
    vcmp = summaries(cache["nsa_vcmp"], vc, 1)
    heads = GROUP * GROUP
    gate_rows = gate[..., :3 * heads].reshape(b, t, 3, heads).transpose(0, 1, 3, 2).reshape(b, t * heads, 3)
    gate_rows = jnp.pad(gate_rows, ((0, 0), (0, 0), (0, SUBLANES - 3)))
    newp = lambda a: _pad_rows(a, NEW_PAD)
    o_rows = sample_nsa(page_table, pos0, _group_rows(proj[..., :1024]), _group_rows(q_rot), gate_rows, kcmp, vcmp,
                        flat(cache["nsa_kwin"]), flat(cache["nsa_vwin"]), newp(ks), newp(vs), newp(kw), newp(vw),
                        flat(cache["nsa_ksel"]).astype(BF16), flat(cache["nsa_vsel"]).astype(BF16))
    o6 = o_rows.reshape(b, t, GROUP, GROUP, GROUP, HEAD_DIM)
    o = jnp.stack([o6[:, :, g, :, g, :] for g in range(GROUP)], axis=2).reshape(b, t, heads * HEAD_DIM)
    g4 = lambda a: a.reshape(b, t, GROUP, HEAD_DIM)
    wlen = min(WINDOW, pos0 + t)
    kw_all = jnp.concatenate([cache["nsa_kwin"], g4(kw)], axis=1)
    vw_all = jnp.concatenate([cache["nsa_vwin"], g4(vw)], axis=1)
    return o, (g4(kc), g4(vc), g4(ks), g4(vs)), (kw_all[:, -wlen:], vw_all[:, -wlen:])


def _rmsnorm_kernel(x_ref, g_ref, o_ref):
    x = x_ref[...]
    o_ref[...] = x * lax.rsqrt(jnp.mean(x * x, axis=-1, keepdims=True) + NORM_EPS) * g_ref[...]


def rmsnorm_rows(x, gain):
    n, d = x.shape
    tm = _row_tile(n, 1024)
    return pl.pallas_call(
        _rmsnorm_kernel,
        out_shape=jax.ShapeDtypeStruct((n, d), F32),
        grid=(n // tm,),
        in_specs=[pl.BlockSpec((tm, d), lambda i: (i, 0)), pl.BlockSpec((1, d), lambda i: (0, 0))],
        out_specs=pl.BlockSpec((tm, d), lambda i: (i, 0)),
        compiler_params=_cparams(1),
        name="rmsnorm",
    )(x, gain.reshape(1, d))


D_MODEL = 1024
DEPTH = 4
D_FF = 2816
CONV_W = 3
X_HEADS = 4
X_HDIM = 128


def _pad_cols(w, n):
    return jnp.pad(w, ((0, 0), (0, n - w.shape[1])))


def _pad_vec(v, n):
    return jnp.pad(v, (0, n - v.shape[0]))


def _to_groups(x):
    b, t, _ = x.shape
    return x.reshape(b, t, GROUP, HEAD_DIM).transpose(0, 2, 1, 3)


def prepare_weights(p):
    w = {}
    hd = FORGET_HEADS * HEAD_DIM
    w["fox_main"] = p["fox_w_in"][:, :3 * hd].astype(BF16)
    w["fox_gate"] = _pad_cols(p["fox_w_in"][:, 3 * hd:], LANES).astype(BF16)
    w["fox_bias"] = _pad_vec(p["fox_b_f"], LANES)
    w["fox_out"] = p["fox_w_out"].astype(BF16)
    w["sb_in"] = p["sb_w_in"].astype(BF16)
    w["sb_out"] = p["sb_w_out"].astype(BF16)
    nsa_main = hd + 6 * GLANES
    w["nsa_main"] = p["nsa_w_in"][:, :nsa_main].astype(BF16)
    w["nsa_gate"] = _pad_cols(p["nsa_w_in"][:, nsa_main:], LANES).astype(BF16)
    w["nsa_out"] = p["nsa_w_out"].astype(BF16)
    w["diff_in"] = p["diff_w_in"].astype(BF16)
    w["diff_out"] = p["diff_w_out"].astype(BF16)
    w["x_q"] = p["x_w_q"].astype(BF16)
    w["x_o"] = p["x_w_o"].astype(BF16)
    w["ffn_up"] = p["ffn_w_up"].astype(BF16)
    w["ffn_down"] = p["ffn_w_down"].astype(BF16)
    w["conv_pack"] = jnp.concatenate(
        [p["ffn_conv_w"], p["ffn_conv_b"][:, None, :], jnp.zeros((DEPTH, SUBLANES - CONV_W - 1, 2 * D_FF), F32)],
        axis=1)
    return w


def _diff_lambda(p, lambda_init):
    return (jnp.exp(jnp.sum(p["diff_lq1"] * p["diff_lk1"])) - jnp.exp(jnp.sum(p["diff_lq2"] * p["diff_lk2"]))
            + lambda_init)


def _lambda_init(layer):
    return 0.8 - 0.6 * math.exp(-0.3 * layer)


def _layer_tail(x, layer, p, w, mem_k, mem_v, state8):
    b, t, d = x.shape
    x = cross_block(x, p["norm_mem"][layer], w["x_q"][layer], mem_k[layer].reshape(b, -1, X_HEADS * X_HDIM),
                    mem_v[layer].reshape(b, -1, X_HEADS * X_HDIM), w["x_o"][layer])
    x2 = x.reshape(b * t, d)
    u = matmul(x2, w["ffn_up"][layer], gain=p["norm_ffn"][layer]).reshape(b, t, 2 * D_FF)
    act = conv_gate(u, state8, w["conv_pack"][layer])
    x = matmul(act.reshape(b * t, D_FF), w["ffn_down"][layer], res=x2).reshape(b, t, d)
    return x, u


def prompt_trunk(x, mem_k, mem_v, p, w):
    b, t, d = x.shape
    n = b * t
    new = {}
    conv_rows = []
    cos_t, sin_t = rope_tables(jnp.arange(t))
    zero_state = jnp.zeros((b, SUBLANES, 2 * D_FF), F32)
    for layer in range(DEPTH):
        kind = layer % 4
        x2 = x.reshape(n, d)
        gain = p["norm_mix"][layer]
        if kind == 0:
            proj = matmul(x2, w["fox_main"], gain=gain).reshape(b, t, -1)
            gate = matmul(x2, w["fox_gate"], gain=gain).reshape(b, t, LANES)
            logf, _, cum_pieces = logf_cumsum(gate, w["fox_bias"])
            v_t = proj[..., 2048:3072].transpose(0, 2, 1)
            o = prompt_flash("fox", proj, 0, proj, 4, v_t, cp=cum_pieces)
            x = matmul(o.reshape(n, -1), w["fox_out"], res=x2).reshape(b, t, d)
            new["fox_k"] = proj[..., 1024:2048].reshape(b, t, 16, 64)
            new["fox_v"] = proj[..., 2048:3072].reshape(b, t, 16, 64)
            new["fox_logf"] = logf[..., :FORGET_HEADS]
        elif kind == 1:
            proj = matmul(x2, w["sb_in"], gain=gain).reshape(b, t, -1)
            o = prompt_sb(proj, proj[..., 2048:3072].transpose(0, 2, 1))
            x = matmul(o.reshape(n, -1), w["sb_out"], res=x2).reshape(b, t, d)
            new["sb_k"] = proj[..., 1024:2048].reshape(b, t, 16, 64)
            new["sb_v"] = proj[..., 2048:3072].reshape(b, t, 16, 64)
        elif kind == 2:
            proj = matmul(x2, w["nsa_main"], gain=gain).reshape(b, t, -1)
            gate = matmul(x2, w["nsa_gate"], gain=gain).reshape(b, t, LANES)
            q_rot = rope_cols(proj, 0, 1024, cos_t, sin_t)
            kc, vc = proj[..., 1024:1280], proj[..., 1280:1536]
            vs, vw = proj[..., 1792:2048], proj[..., 2304:2560]
            ks = rope_cols(proj, 1536, 256, cos_t, sin_t)
            kw = rope_cols(proj, 2048, 256, cos_t, sin_t)
            tp = -(-t // SEL_BLK) * SEL_BLK
            tpad = ((0, 0), (0, 0), (0, tp - t), (0, 0))
            kcmp = compress_rows(jnp.pad(_to_groups(kc), tpad), p["nsa_cmp_pe"][0], p["nsa_cmp_w1"][0],
                                 p["nsa_cmp_b1"][0], p["nsa_cmp_w2"][0])
            vcmp = compress_rows(jnp.pad(_to_groups(vc), tpad), p["nsa_cmp_pe"][1], p["nsa_cmp_w1"][1],
                                 p["nsa_cmp_b1"][1], p["nsa_cmp_w2"][1])
            ncp = -(-kcmp.shape[2] // LANES) * LANES
            cpad = ((0, 0), (0, 0), (0, ncp - kcmp.shape[2]), (0, 0))
            gate_t = gate[..., :3 * GROUP * GROUP].reshape(b, t, 3, GROUP, GROUP).transpose(0, 3, 2, 4, 1)
            gate_t = jnp.pad(gate_t.reshape(b, GROUP, 3 * GROUP, t), ((0, 0), (0, 0), (0, GROUP), (0, 0)))
            o = prompt_nsa(proj, q_rot, gate_t, jnp.pad(kcmp, cpad), jnp.pad(vcmp, cpad).transpose(0, 1, 3, 2),
                           _to_groups(ks), vs.transpose(0, 2, 1), _to_groups(kw), vw.transpose(0, 2, 1))
            x = matmul(o.reshape(n, -1), w["nsa_out"], res=x2).reshape(b, t, d)
            g4 = lambda a: a.reshape(b, t, GROUP, HEAD_DIM)
            new["nsa_kcmp"], new["nsa_vcmp"], new["nsa_ksel"], new["nsa_vsel"] = g4(kc), g4(vc), g4(ks), g4(vs)
            wlen = min(WINDOW, t)
            new["nsa_kwin"], new["nsa_vwin"] = g4(kw)[:, t - wlen:], g4(vw)[:, t - wlen:]
        else:
            proj = matmul(x2, w["diff_in"], gain=gain).reshape(b, t, -1)
            qk = rope_cols(proj, 0, 2048, cos_t, sin_t)
            li = _lambda_init(layer)
            o = prompt_flash("diff", qk, 0, qk, 4, proj[..., 2048:3072].transpose(0, 2, 1),
                             lam=_diff_lambda(p, li), subln=p["diff_subln"], out_scale=1.0 - li)
            x = matmul(o.reshape(n, -1), w["diff_out"], res=x2).reshape(b, t, d)
            new["diff_k"] = qk[..., 1024:2048].reshape(b, t, 16, 64)
            new["diff_v"] = proj[..., 2048:3072].reshape(b, t, 8, 128)
        x, u = _layer_tail(x, layer, p, w, mem_k, mem_v, zero_state)
        conv_rows.append(u[:, t - (CONV_W - 1):])
    new["ffn_conv"] = jnp.stack(conv_rows)
    y = rmsnorm_rows(x.reshape(n, d), p["norm_final"]).reshape(b, t, d)
    return y, new


def sample_trunk(x, pos0, mem_k, mem_v, conv_state, cache, page_table, p, w):
    b, t, d = x.shape
    n = b * t
    new = {}
    conv_rows = []
    cos_t, sin_t = rope_tables(pos0 + jnp.arange(t))
    pool = cache["fox_k"].shape[0]
    page = cache["fox_k"].shape[1]
    flat = lambda c: c.reshape(pool, page, -1).astype(BF16)
    for layer in range(DEPTH):
        kind = layer % 4
        x2 = x.reshape(n, d)
        gain = p["norm_mix"][layer]
        if kind == 0:
            proj = matmul(x2, w["fox_main"], gain=gain).reshape(b, t, -1)
            gate = matmul(x2, w["fox_gate"], gain=gain).reshape(b, t, LANES)
            logf, cum, _ = logf_cumsum(gate, w["fox_bias"])
            new_cum_t = _pad_rows(cum[..., :FORGET_HEADS], NEW_PAD).transpose(0, 2, 1)
            o = sample_attn("fox", page_table, proj[..., :1024], proj[..., 1024:2048], proj[..., 2048:3072],
                            flat(cache["fox_k"]), flat(cache["fox_v"]),
                            lf_cache_t=cache["fox_logf"].transpose(0, 2, 1), new_cum_t=new_cum_t)
            x = matmul(o.reshape(n, -1), w["fox_out"], res=x2).reshape(b, t, d)
            new["fox_k"] = proj[..., 1024:2048].reshape(b, t, 16, 64)
            new["fox_v"] = proj[..., 2048:3072].reshape(b, t, 16, 64)
            new["fox_logf"] = logf[..., :FORGET_HEADS]
        elif kind == 1:
            proj = matmul(x2, w["sb_in"], gain=gain).reshape(b, t, -1)
            o = sample_attn("sb", page_table, proj[..., :1024], proj[..., 1024:2048], proj[..., 2048:3072],
                            flat(cache["sb_k"]), flat(cache["sb_v"]))
            x = matmul(o.reshape(n, -1), w["sb_out"], res=x2).reshape(b, t, d)
            new["sb_k"] = proj[..., 1024:2048].reshape(b, t, 16, 64)
            new["sb_v"] = proj[..., 2048:3072].reshape(b, t, 16, 64)
        elif kind == 2:
            proj = matmul(x2, w["nsa_main"], gain=gain).reshape(b, t, -1)
            gate = matmul(x2, w["nsa_gate"], gain=gain).reshape(b, t, LANES)
            o, rows, win = sample_nsa_mixer(proj, gate, pos0, cos_t, sin_t, cache, page_table, p)
            x = matmul(o.reshape(n, -1), w["nsa_out"], res=x2).reshape(b, t, d)
            new["nsa_kcmp"], new["nsa_vcmp"], new["nsa_ksel"], new["nsa_vsel"] = rows
            new["nsa_kwin"], new["nsa_vwin"] = win
        else:
            proj = matmul(x2, w["diff_in"], gain=gain).reshape(b, t, -1)
            qk = rope_cols(proj, 0, 2048, cos_t, sin_t)
            li = _lambda_init(layer)
            o = sample_attn("diff", page_table, qk[..., :1024], qk[..., 1024:2048], proj[..., 2048:3072],
                            flat(cache["diff_k"]), flat(cache["diff_v"]),
                            lam=_diff_lambda(p, li), subln=p["diff_subln"], out_scale=1.0 - li)
            x = matmul(o.reshape(n, -1), w["diff_out"], res=x2).reshape(b, t, d)
            new["diff_k"] = qk[..., 1024:2048].reshape(b, t, 16, 64)
            new["diff_v"] = proj[..., 2048:3072].reshape(b, t, 8, 128)
        state8 = jnp.pad(conv_state[layer], ((0, 0), (SUBLANES - (CONV_W - 1), 0), (0, 0)))
        x, u = _layer_tail(x, layer, p, w, mem_k, mem_v, state8)
        ext = jnp.concatenate([conv_state[layer], u], axis=1)
        conv_rows.append(ext[:, -(CONV_W - 1):])
    new["ffn_conv"] = jnp.stack(conv_rows)
    y = rmsnorm_rows(x.reshape(n, d), p["norm_final"]).reshape(b, t, d)
    return y, new


def memory_kv(mem, mem_norm, w_kv):
    b, m, d = mem.shape
    e = X_HEADS * X_HDIM
    kv = jnp.stack([matmul(mem.reshape(b * m, d), w_kv[layer].astype(BF16), gain=mem_norm[layer])
                    for layer in range(DEPTH)])
    xk = kv[..., :e].reshape(DEPTH, b, m, X_HEADS, X_HDIM)
    xv = kv[..., e:].reshape(DEPTH, b, m, X_HEADS, X_HDIM)
    return xk, xv


def kernel(x_prompt, x_sample, cache_fox_k, cache_fox_v, cache_fox_logf, cache_sb_k, cache_sb_v,
           cache_nsa_kcmp, cache_nsa_vcmp, cache_nsa_ksel, cache_nsa_vsel, state_nsa_kwin, state_nsa_vwin,
           cache_diff_k, cache_diff_v, cache_mem_k, cache_mem_v, state_ffn_conv, page_table, mem_prompt,
           norm_mix, norm_mem, norm_ffn, norm_final, fox_w_in, fox_b_f, fox_w_out, sb_w_in, sb_w_out,
           nsa_w_in, nsa_cmp_pe, nsa_cmp_w1, nsa_cmp_b1, nsa_cmp_w2, nsa_w_out,
           diff_w_in, diff_lq1, diff_lk1, diff_lq2, diff_lk2, diff_subln, diff_w_out,
           mem_norm, x_w_q, x_w_kv, x_w_o, ffn_w_up, ffn_conv_w, ffn_conv_b, ffn_w_down):
    p = {
        "norm_mix": norm_mix, "norm_mem": norm_mem, "norm_ffn": norm_ffn, "norm_final": norm_final,
        "fox_w_in": fox_w_in, "fox_b_f": fox_b_f, "fox_w_out": fox_w_out,
        "sb_w_in": sb_w_in, "sb_w_out": sb_w_out,
        "nsa_w_in": nsa_w_in, "nsa_cmp_pe": nsa_cmp_pe, "nsa_cmp_w1": nsa_cmp_w1, "nsa_cmp_b1": nsa_cmp_b1,
        "nsa_cmp_w2": nsa_cmp_w2, "nsa_w_out": nsa_w_out,
        "diff_w_in": diff_w_in, "diff_lq1": diff_lq1, "diff_lk1": diff_lk1, "diff_lq2": diff_lq2,
        "diff_lk2": diff_lk2, "diff_subln": diff_subln, "diff_w_out": diff_w_out,
        "x_w_q": x_w_q, "x_w_o": x_w_o,
        "ffn_w_up": ffn_w_up, "ffn_conv_w": ffn_conv_w, "ffn_conv_b": ffn_conv_b, "ffn_w_down": ffn_w_down,
    }
    w = prepare_weights(p)
    mem_k_p, mem_v_p = memory_kv(mem_prompt, mem_norm, x_w_kv)
    y_prompt, sp = prompt_trunk(x_prompt, mem_k_p, mem_v_p, p, w)
    cache = {
        "fox_k": cache_fox_k, "fox_v": cache_fox_v, "fox_logf": cache_fox_logf,
        "sb_k": cache_sb_k, "sb_v": cache_sb_v,
        "nsa_kcmp": cache_nsa_kcmp, "nsa_vcmp": cache_nsa_vcmp, "nsa_ksel": cache_nsa_ksel,
        "nsa_vsel": cache_nsa_vsel, "nsa_kwin": state_nsa_kwin, "nsa_vwin": state_nsa_vwin,
        "diff_k": cache_diff_k, "diff_v": cache_diff_v,
    }
    past_len = page_table.shape[1] * cache_fox_k.shape[1]
    y_sample, ss = sample_trunk(x_sample, past_len, cache_mem_k, cache_mem_v, state_ffn_conv, cache, page_table,
                                p, w)
    return (y_prompt, y_sample,
            sp["fox_k"], sp["fox_v"], sp["fox_logf"], sp["sb_k"], sp["sb_v"],
            sp["nsa_kcmp"], sp["nsa_vcmp"], sp["nsa_ksel"], sp["nsa_vsel"], sp["nsa_kwin"], sp["nsa_vwin"],
            sp["diff_k"], sp["diff_v"], mem_k_p, mem_v_p, sp["ffn_conv"],
            ss["fox_k"], ss["fox_v"], ss["fox_logf"], ss["sb_k"], ss["sb_v"],
            ss["nsa_kcmp"], ss["nsa_vcmp"], ss["nsa_ksel"], ss["nsa_vsel"], ss["nsa_kwin"], ss["nsa_vwin"],
            ss["diff_k"], ss["diff_v"], ss["ffn_conv"])
```

```python
import functools
import math

import jax
import jax.numpy as jnp
from jax import lax
from jax.experimental import pallas as pl
from jax.experimental.pallas import tpu as pltpu

F32 = jnp.float32
BF16 = jnp.bfloat16

V7X_VMEM_BYTES = 64 * 1024 * 1024
LANES = 128
SUBLANES = 8
VMEM_LIMIT = V7X_VMEM_BYTES - 8 * 1024 * 1024

NORM_EPS = 1e-6
MASK_VALUE = -1e30
ROPE_THETA = 10000.0
FORGET_HEADS = 16
HEAD_DIM = 64
GROUP = 4
GLANES = GROUP * HEAD_DIM


def _cparams(n_axes):
    return pltpu.CompilerParams(dimension_semantics=("arbitrary",) * n_axes, vmem_limit_bytes=VMEM_LIMIT)


def _nt_dot(a, b):
    return lax.dot_general(a, b, (((1,), (1,)), ((), ())), preferred_element_type=F32)


def _dot(a, b):
    return jnp.dot(a, b, preferred_element_type=F32)


def _split3(x):
    hi = x.astype(BF16)
    r1 = x - hi.astype(F32)
    mid = r1.astype(BF16)
    lo = (r1 - mid.astype(F32)).astype(BF16)
    return hi, mid, lo


def _dot_exact_rhs01(x, m01):
    hi, mid, lo = _split3(x)
    return _dot(hi, m01) + _dot(mid, m01) + _dot(lo, m01)


def _dot_exact_lhs01(m01, x):
    hi, mid, lo = _split3(x)
    return _dot(m01, hi) + _dot(m01, mid) + _dot(m01, lo)


def _softplus(z):
    return jnp.maximum(z, 0.0) + jnp.log(1.0 + jnp.exp(-jnp.abs(z)))


def _log_sigmoid(z):
    return jnp.minimum(z, 0.0) - jnp.log1p(jnp.exp(-jnp.abs(z)))


def _mm_kernel(*refs, prologue, has_res, stage):
    it = iter(refs)
    x_ref = next(it)
    if prologue == "norm":
        g_ref = next(it)
    elif prologue == "add_gelu":
        x2_ref, pb_ref = next(it), next(it)
    w_ref = next(it)
    r_ref = next(it) if has_res else None
    o_ref = next(it)
    xs_ref = next(it) if stage else None

    if stage:
        @pl.when(pl.program_id(1) == 0)
        def _():
            x = x_ref[...].astype(F32)
            if prologue == "norm":
                x = x * lax.rsqrt(jnp.mean(x * x, axis=-1, keepdims=True) + NORM_EPS)
                x = x * g_ref[...]
            elif prologue == "add_gelu":
                x = jax.nn.gelu(x + x2_ref[...] + pb_ref[...])
            xs_ref[...] = x.astype(BF16)
        xb = xs_ref[...]
    else:
        xb = x_ref[...]
    y = _dot(xb, w_ref[...])
    if has_res:
        y = y + r_ref[...]
    o_ref[...] = y.astype(o_ref.dtype)


def _row_tile(n, cap):
    t = min(n, cap)
    while n % t:
        t //= 2
    return t


def matmul(x, w, *, gain=None, add=None, pre_bias=None, res=None, out_dtype=F32, tm_cap=1024, tn_cap=512):
    n, k = x.shape
    e = w.shape[1]
    tm = _row_tile(n, tm_cap)
    tn = _row_tile(e, tn_cap)
    assert n % tm == 0 and e % tn == 0 and tm % SUBLANES == 0 and tn % LANES == 0, (n, e, tm, tn)
    prologue = "norm" if gain is not None else ("add_gelu" if add is not None else None)
    stage = prologue is not None or x.dtype != BF16
    ins = [x]
    specs = [pl.BlockSpec((tm, k), lambda i, j: (i, 0))]
    if prologue == "norm":
        ins.append(gain.reshape(1, k).astype(F32))
        specs.append(pl.BlockSpec((1, k), lambda i, j: (0, 0)))
    elif prologue == "add_gelu":
        ins += [add, pre_bias.reshape(1, k).astype(F32)]
        specs += [pl.BlockSpec((tm, k), lambda i, j: (i, 0)), pl.BlockSpec((1, k), lambda i, j: (0, 0))]
    ins.append(w)
    specs.append(pl.BlockSpec((k, tn), lambda i, j: (0, j)))
    if res is not None:
        ins.append(res)
        specs.append(pl.BlockSpec((tm, tn), lambda i, j: (i, j)))
    kern = functools.partial(_mm_kernel, prologue=prologue, has_res=res is not None, stage=stage)
    return pl.pallas_call(
        kern,
        out_shape=jax.ShapeDtypeStruct((n, e), out_dtype),
        grid=(n // tm, e // tn),
        in_specs=specs,
        out_specs=pl.BlockSpec((tm, tn), lambda i, j: (i, j)),
        scratch_shapes=[pltpu.VMEM((tm, k), BF16)] if stage else [],
        compiler_params=_cparams(2),
        name="matmul",
    )(*ins)


def _conv_gate_kernel(uv_ref, ug_ref, hv_ref, hg_ref, sv_ref, sg_ref, wv_ref, wg_ref, o_ref):
    first = pl.program_id(1) == 0

    def conv(u_ref, h_ref, s_ref, w_ref):
        u = u_ref[0]
        prev = jnp.where(first, s_ref[0], h_ref[0])
        p2 = prev[6:7, :]
        p1 = prev[7:8, :]
        row = lax.broadcasted_iota(jnp.int32, u.shape, 0)
        um1 = jnp.where(row == 0, p1, pltpu.roll(u, 1, 0))
        um2 = jnp.where(row == 0, p2, jnp.where(row == 1, p1, pltpu.roll(u, 2, 0)))
        w = w_ref[...]
        return w[3:4, :] + w[0:1, :] * um2 + w[1:2, :] * um1 + w[2:3, :] * u

    val = conv(uv_ref, hv_ref, sv_ref, wv_ref)
    gate = conv(ug_ref, hg_ref, sg_ref, wg_ref)
    o_ref[0] = (gate * jax.nn.sigmoid(gate) * val).astype(o_ref.dtype)


def conv_gate(u, state8, wpack):
    b, t, f2 = u.shape
    f = f2 // 2
    tf = f // 2
    assert tf % LANES == 0
    nf = f // tf
    tt = _row_tile(t, 512)
    hb = tt // SUBLANES

    def halo(i):
        return jnp.maximum(i * hb - 1, 0)

    return pl.pallas_call(
        _conv_gate_kernel,
        out_shape=jax.ShapeDtypeStruct((b, t, f), BF16),
        grid=(b, t // tt, nf),
        in_specs=[
            pl.BlockSpec((1, tt, tf), lambda bi, i, j: (bi, i, j)),
            pl.BlockSpec((1, tt, tf), lambda bi, i, j: (bi, i, nf + j)),
            pl.BlockSpec((1, SUBLANES, tf), lambda bi, i, j: (bi, halo(i), j)),
            pl.BlockSpec((1, SUBLANES, tf), lambda bi, i, j: (bi, halo(i), nf + j)),
            pl.BlockSpec((1, SUBLANES, tf), lambda bi, i, j: (bi, 0, j)),
            pl.BlockSpec((1, SUBLANES, tf), lambda bi, i, j: (bi, 0, nf + j)),
            pl.BlockSpec((SUBLANES, tf), lambda bi, i, j: (0, j)),
            pl.BlockSpec((SUBLANES, tf), lambda bi, i, j: (0, nf + j)),
        ],
        out_specs=pl.BlockSpec((1, tt, tf), lambda bi, i, j: (bi, i, j)),
        compiler_params=_cparams(3),
        name="conv_gate",
    )(u, u, u, u, state8, state8, wpack, wpack)


def _cross_kernel(x_ref, g_ref, wq_ref, mk_ref, mv_ref, wo_ref, o_ref, *, heads, hdim):
    x = x_ref[0]
    h = x * lax.rsqrt(jnp.mean(x * x, axis=-1, keepdims=True) + NORM_EPS) * g_ref[...]
    q = _dot(h.astype(BF16), wq_ref[...]) * (hdim ** -0.5)
    outs = []
    for hh in range(heads):
        sl = slice(hh * hdim, (hh + 1) * hdim)
        s = _nt_dot(q[:, sl].astype(BF16), mk_ref[0, :, sl].astype(BF16))
        p = jnp.exp(s - jnp.max(s, axis=-1, keepdims=True))
        p = p / jnp.sum(p, axis=-1, keepdims=True)
        outs.append(_dot(p.astype(BF16), mv_ref[0, :, sl].astype(BF16)))
    o = jnp.concatenate(outs, axis=-1)
    o_ref[0] = x + _dot(o.astype(BF16), wo_ref[...])


def cross_block(x, gain, wq, mk, mv, wo, *, heads=4, hdim=128):
    b, t, d = x.shape
    m = mk.shape[1]
    e = heads * hdim
    tt = _row_tile(t, 512)
    return pl.pallas_call(
        functools.partial(_cross_kernel, heads=heads, hdim=hdim),
        out_shape=jax.ShapeDtypeStruct((b, t, d), F32),
        grid=(b, t // tt),
        in_specs=[
            pl.BlockSpec((1, tt, d), lambda bi, i: (bi, i, 0)),
            pl.BlockSpec((1, d), lambda bi, i: (0, 0)),
            pl.BlockSpec((d, e), lambda bi, i: (0, 0)),
            pl.BlockSpec((1, m, e), lambda bi, i: (bi, 0, 0)),
            pl.BlockSpec((1, m, e), lambda bi, i: (bi, 0, 0)),
            pl.BlockSpec((e, d), lambda bi, i: (0, 0)),
        ],
        out_specs=pl.BlockSpec((1, tt, d), lambda bi, i: (bi, i, 0)),
        compiler_params=_cparams(2),
        name="cross_block",
    )(x, gain.reshape(1, d), wq, mk, mv, wo)


def rope_tables(pos):
    half = HEAD_DIM // 2
    inv = ROPE_THETA ** (-jnp.arange(half, dtype=F32) / half)
    ang = pos.astype(F32)[:, None] * inv[None, :]
    cos = jnp.cos(ang)
    sin = jnp.sin(ang)
    cos_t = jnp.tile(jnp.concatenate([cos, cos], axis=-1), (1, GROUP))
    sin_t = jnp.tile(jnp.concatenate([-sin, sin], axis=-1), (1, GROUP))
    return cos_t, sin_t


def _rope_apply(x, cos_t, sin_t):
    half = HEAD_DIM // 2
    lane = lax.broadcasted_iota(jnp.int32, x.shape, 1)
    first = (lane % HEAD_DIM) < half
    n = x.shape[1]
    swapped = jnp.where(first, pltpu.roll(x, n - half, 1), pltpu.roll(x, half, 1))
    return x * cos_t + swapped * sin_t


def _rope_kernel(x_ref, c_ref, s_ref, o_ref):
    o_ref[0] = _rope_apply(x_ref[0], c_ref[...], s_ref[...])


def rope_cols(x, col0, width, cos_t, sin_t):
    b, t, _ = x.shape
    assert col0 % GLANES == 0 and width % GLANES == 0
    tt = _row_tile(t, 512)
    c0 = col0 // GLANES
    return pl.pallas_call(
        _rope_kernel,
        out_shape=jax.ShapeDtypeStruct((b, t, width), F32),
        grid=(b, t // tt, width // GLANES),
        in_specs=[
            pl.BlockSpec((1, tt, GLANES), lambda bi, i, j: (bi, i, c0 + j)),
            pl.BlockSpec((tt, GLANES), lambda bi, i, j: (i, 0)),
            pl.BlockSpec((tt, GLANES), lambda bi, i, j: (i, 0)),
        ],
        out_specs=pl.BlockSpec((1, tt, GLANES), lambda bi, i, j: (bi, i, j)),
        compiler_params=_cparams(3),
        name="rope",
    )(x, cos_t, sin_t)


def _logf_cumsum_kernel(g_ref, b_ref, lf_ref, c_ref, cp_ref, *, chunk):
    t = g_ref.shape[1]
    row = lax.broadcasted_iota(jnp.int32, (chunk, chunk), 0)
    col = lax.broadcasted_iota(jnp.int32, (chunk, chunk), 1)
    tri = jnp.where(col <= row, 1.0, 0.0).astype(BF16)
    head_lane = lax.broadcasted_iota(jnp.int32, (chunk, g_ref.shape[2]), 1) < FORGET_HEADS

    def body(i, carry):
        sl = pl.ds(pl.multiple_of(i * chunk, chunk), chunk)
        lf = _log_sigmoid(g_ref[0, sl, :] + b_ref[...])
        lf_ref[0, sl, :] = lf
        c = _dot_exact_lhs01(tri, lf) + carry
        c_ref[0, sl, :] = c
        hi, mid, lo = _split3(jnp.where(head_lane, c, 0.0))
        placed = (hi.astype(F32) + pltpu.roll(mid.astype(F32), FORGET_HEADS, 1)
                  + pltpu.roll(lo.astype(F32), 2 * FORGET_HEADS, 1))
        cp_ref[0, sl, :] = placed.astype(BF16)
        return c[chunk - 1:chunk, :]

    lax.fori_loop(0, t // chunk, body, jnp.zeros((1, g_ref.shape[2]), F32))


def logf_cumsum(gate, bias):
    b, t, n = gate.shape
    chunk = _row_tile(t, 256)
    spec = pl.BlockSpec((1, t, n), lambda bi: (bi, 0, 0))
    return pl.pallas_call(
        functools.partial(_logf_cumsum_kernel, chunk=chunk),
        out_shape=(jax.ShapeDtypeStruct((b, t, n), F32), jax.ShapeDtypeStruct((b, t, n), F32),
                   jax.ShapeDtypeStruct((b, t, n), BF16)),
        grid=(b,),
        in_specs=[spec, pl.BlockSpec((1, n), lambda bi: (0, 0))],
        out_specs=(spec, spec, spec),
        compiler_params=_cparams(1),
        name="logf_cumsum",
    )(gate, bias.reshape(1, n))


def _stack_heads(q):
    lane = lax.broadcasted_iota(jnp.int32, q.shape, 1) // HEAD_DIM
    return jnp.concatenate([jnp.where(lane == r, q, 0.0).astype(BF16) for r in range(GROUP)], axis=0)


def _causal_mask(tq, tk, q0, k0, strict):
    row = lax.broadcasted_iota(jnp.int32, (GROUP * tq, tk), 0) % tq + q0
    col = lax.broadcasted_iota(jnp.int32, (GROUP * tq, tk), 1) + k0
    return col < row if strict else col <= row


def _softmax_tile(s, mask, v, m_ref, l_ref, acc_ref):
    if mask is not None:
        s = jnp.where(mask, s, MASK_VALUE)
    m_prev = m_ref[...]
    m_new = jnp.maximum(m_prev, jnp.max(s, axis=1, keepdims=True))
    alpha = jnp.exp(m_prev - m_new)
    p = jnp.exp(s - m_new)
    if mask is not None:
        p = jnp.where(mask, p, 0.0)
    l_ref[...] = alpha * l_ref[...] + jnp.sum(p, axis=1, keepdims=True)
    acc_ref[...] = alpha * acc_ref[...] + _dot(p.astype(BF16), v)
    m_ref[...] = m_new


def _pflash_kernel(*refs, kind, tq, tk, out_scale):
    if kind == "fox":
        q_ref, k_ref, v_ref, ck_ref, o_ref, m_ref, l_ref, acc_ref = refs
    else:
        q_ref, k_ref, v_ref, lam_ref, sub_ref, o_ref, m_ref, l_ref, acc_ref = refs
    qi = pl.program_id(2)
    q4 = _stack_heads(q_ref[0] * (HEAD_DIM ** -0.5))
    m_ref[...] = jnp.full(m_ref.shape, MASK_VALUE, F32)
    l_ref[...] = jnp.zeros(l_ref.shape, F32)
    acc_ref[...] = jnp.zeros(acc_ref.shape, F32)

    def tile(kt, masked):
        ks = pl.ds(pl.multiple_of(kt * tk, tk), tk)
        s = _nt_dot(q4, k_ref[0, ks, :].astype(BF16))
        if kind == "fox":
            ck = ck_ref[0, 0, :, ks]
            s = s - jnp.concatenate(
                [jnp.broadcast_to(ck[r:r + 1, :], (tq, tk)) for r in range(GROUP)], axis=0)
        mask = _causal_mask(tq, tk, qi * tq, kt * tk, False) if masked else None
        _softmax_tile(s, mask, v_ref[0, ks, :].astype(BF16), m_ref, l_ref, acc_ref)

    def body(kt, c):
        tile(kt, False)
        return c

    lax.fori_loop(0, qi, body, 0)
    tile(qi, True)

    inv = 1.0 / jnp.maximum(l_ref[...], 1e-30)
    lane = lax.broadcasted_iota(jnp.int32, (tq, GLANES), 1)
    blk = [acc_ref[r * tq:(r + 1) * tq, :] * inv[r * tq:(r + 1) * tq, :] for r in range(GROUP)]
    if kind == "fox":
        out = jnp.zeros((tq, GLANES), F32)
        for r in range(GROUP):
            out = jnp.where(lane // HEAD_DIM == r, blk[r], out)
        o_ref[0] = out
    else:
        lam = lam_ref[0, 0]
        halves = []
        for hh in range(2):
            d = (blk[2 * hh] - lam * blk[2 * hh + 1])[:, hh * LANES:(hh + 1) * LANES]
            y = d * lax.rsqrt(jnp.mean(d * d, axis=-1, keepdims=True) + NORM_EPS) * sub_ref[...]
            halves.append(y * out_scale)
        o_ref[0] = jnp.concatenate(halves, axis=1)


def prompt_flash(kind, qa, qcol, ka, kcol, va, vcol, *, ck=None, lam=None, subln=None, out_scale=1.0, tile=256):
    b, t, _ = qa.shape
    tq = tk = _row_tile(t, tile)
    ngroups = 4
    ins = [qa, ka, va]
    specs = [
        pl.BlockSpec((1, tq, GLANES), lambda bi, g, i: (bi, i, qcol + g)),
        pl.BlockSpec((1, t, GLANES), lambda bi, g, i: (bi, 0, kcol + g)),
        pl.BlockSpec((1, t, GLANES), lambda bi, g, i: (bi, 0, vcol + g)),
    ]
    if kind == "fox":
        ins.append(ck)
        specs.append(pl.BlockSpec((1, 1, SUBLANES, t), lambda bi, g, i: (bi, g, 0, 0)))
    else:
        ins += [lam.reshape(1, 1), subln.reshape(1, LANES)]
        specs += [pl.BlockSpec(memory_space=pltpu.SMEM), pl.BlockSpec((1, LANES), lambda bi, g, i: (0, 0))]
    return pl.pallas_call(
        functools.partial(_pflash_kernel, kind=kind, tq=tq, tk=tk, out_scale=out_scale),
        out_shape=jax.ShapeDtypeStruct((b, t, ngroups * GLANES), F32),
        grid=(b, ngroups, t // tq),
        in_specs=specs,
        out_specs=pl.BlockSpec((1, tq, GLANES), lambda bi, g, i: (bi, i, g)),
        scratch_shapes=[
            pltpu.VMEM((GROUP * tq, 1), F32),
            pltpu.VMEM((GROUP * tq, 1), F32),
            pltpu.VMEM((GROUP * tq, GLANES), F32),
        ],
        compiler_params=_cparams(3),
        name="prompt_flash_" + kind,
    )(*ins)


def _strict_lower_ones(n):
    row = lax.broadcasted_iota(jnp.int32, (n, n), 0)
    col = lax.broadcasted_iota(jnp.int32, (n, n), 1)
    return jnp.where(row > col, 1.0, 0.0).astype(BF16)


def _sb_tile(z, mask, v, tri, carry_ref, acc_ref):
    sp = _softplus(z)
    log_rest = -sp
    if mask is not None:
        log_rest = jnp.where(mask, log_rest, 0.0)
    hi, lo = _split2(log_rest)
    after = carry_ref[...] + _dot(hi, tri) + _dot(lo, tri)
    a = jnp.exp(z - sp + after)
    if mask is not None:
        a = jnp.where(mask, a, 0.0)
    acc_ref[...] += _dot(a.astype(BF16), v)
    carry_ref[...] = after[:, 0:1] + log_rest[:, 0:1]


def _psb_kernel(q_ref, k_ref, v_ref, o_ref, carry_ref, acc_ref, *, tq, tk):
    qi = pl.program_id(2)
    q4 = _stack_heads(q_ref[0] * (HEAD_DIM ** -0.5))
    carry_ref[...] = jnp.zeros(carry_ref.shape, F32)
    acc_ref[...] = jnp.zeros(acc_ref.shape, F32)
    tri = _strict_lower_ones(tk)

    def tile(kt, masked):
        ks = pl.ds(pl.multiple_of(kt * tk, tk), tk)
        z = _nt_dot(q4, k_ref[0, ks, :].astype(BF16))
        mask = _causal_mask(tq, tk, qi * tq, kt * tk, True) if masked else None
        _sb_tile(z, mask, v_ref[0, ks, :].astype(BF16), tri, carry_ref, acc_ref)

    tile(qi, True)

    def body(i, c):
        tile(qi - 1 - i, False)
        return c

    lax.fori_loop(0, qi, body, 0)
    lane = lax.broadcasted_iota(jnp.int32, (tq, GLANES), 1)
    out = jnp.zeros((tq, GLANES), F32)
    for r in range(GROUP):
        out = jnp.where(lane // HEAD_DIM == r, acc_ref[r * tq:(r + 1) * tq, :], out)
    o_ref[0] = out


def prompt_sb(proj, *, tile=256):
    b, t, _ = proj.shape
    tq = tk = _row_tile(t, tile)
    ngroups = 4
    return pl.pallas_call(
        functools.partial(_psb_kernel, tq=tq, tk=tk),
        out_shape=jax.ShapeDtypeStruct((b, t, ngroups * GLANES), F32),
        grid=(b, ngroups, t // tq),
        in_specs=[
            pl.BlockSpec((1, tq, GLANES), lambda bi, g, i: (bi, i, g)),
            pl.BlockSpec((1, t, GLANES), lambda bi, g, i: (bi, 0, ngroups + g)),
            pl.BlockSpec((1, t, GLANES), lambda bi, g, i: (bi, 0, 2 * ngroups + g)),
        ],
        out_specs=pl.BlockSpec((1, tq, GLANES), lambda bi, g, i: (bi, i, g)),
        scratch_shapes=[
            pltpu.VMEM((GROUP * tq, 1), F32),
            pltpu.VMEM((GROUP * tq, GLANES), F32),
        ],
        compiler_params=_cparams(3),
        name="prompt_sb",
    )(proj, proj, proj)


CMP_STRIDE = 16
CMP_BLK = 32
SEL_BLK = 64
N_SELECT = 16
WINDOW = 512
SEL_FORCE = 1e4


def _stack_group_heads(q):
    return jnp.concatenate([q[:, r * HEAD_DIM:(r + 1) * HEAD_DIM] for r in range(GROUP)], axis=0).astype(BF16)


def _unstack_group_heads(o, t):
    return jnp.concatenate([o[r * t:(r + 1) * t, :] for r in range(GROUP)], axis=1)


def _select_blocks(p_sum, q_pos, n_sel, n_top):
    t, nc = p_sum.shape
    nb = -(-n_sel // LANES) * LANES
    c_idx = lax.broadcasted_iota(jnp.int32, (nc, nb), 0) * CMP_STRIDE
    j_idx = lax.broadcasted_iota(jnp.int32, (nc, nb), 1)
    overlap = (c_idx < j_idx * SEL_BLK + SEL_BLK) & (c_idx + CMP_BLK > j_idx * SEL_BLK) & (j_idx < n_sel)
    imp = _dot_exact_rhs01(p_sum, jnp.where(overlap, 1.0, 0.0).astype(BF16))
    blk = lax.broadcasted_iota(jnp.int32, (t, nb), 1)
    cur = q_pos // SEL_BLK
    forced = (blk == 0) | (blk == cur) | (blk == cur - 1)
    valid = blk * SEL_BLK <= q_pos
    score = jnp.where(forced, SEL_FORCE, jnp.where(valid, imp, -1.0))
    score = jnp.where(blk < n_sel, score, -3e38)
    rows = -(-n_sel // SUBLANES) * SUBLANES
    sc_t = score.T[:rows, :]
    j_iota = lax.broadcasted_iota(jnp.int32, sc_t.shape, 0)
    rank = jnp.zeros(sc_t.shape, F32)
    for jp in range(n_sel):
        other = sc_t[jp:jp + 1, :]
        ge = jnp.where(other >= sc_t, 1.0, 0.0)
        gt = jnp.where(other > sc_t, 1.0, 0.0)
        rank = rank + jnp.where(j_iota > jp, ge, gt)
    sel_t = jnp.where(rank < n_top, 1.0, 0.0)
    if rows < nb:
        sel_t = jnp.concatenate([sel_t, jnp.zeros((nb - rows, t), F32)], axis=0)
    return sel_t.T


def _expand_block_mask(sel, k0, tk):
    nb = sel.shape[1]
    j = lax.broadcasted_iota(jnp.int32, (nb, tk), 0)
    key = lax.broadcasted_iota(jnp.int32, (nb, tk), 1) + k0
    e = jnp.where(j == key // SEL_BLK, 1.0, 0.0).astype(BF16)
    return _dot(sel.astype(BF16), e)


def _finish_softmax(l_ref, acc_ref):
    return acc_ref[...] / jnp.maximum(l_ref[...], 1e-30)


def _reset_softmax(m_ref, l_ref, acc_ref):
    m_ref[...] = jnp.full(m_ref.shape, MASK_VALUE, F32)
    l_ref[...] = jnp.zeros(l_ref.shape, F32)
    acc_ref[...] = jnp.zeros(acc_ref.shape, F32)


def _expand_gates(gates, branch, g, width):
    c = lax.broadcasted_iota(jnp.int32, (LANES, width), 0)
    lane = lax.broadcasted_iota(jnp.int32, (LANES, width), 1)
    e = jnp.where(c == branch * (GROUP * GROUP) + g * GROUP + lane // HEAD_DIM, 1.0, 0.0).astype(BF16)
    return _dot_exact_rhs01(gates, e)


def _pnsa_kernel(qc_ref, qr_ref, gate_ref, kc_ref, vc_ref, ks_ref, vs_ref, kw_ref, vw_ref, o_ref,
                 m_ref, l_ref, acc_ref, *, tq, tk, n_sel):
    g = pl.program_id(1)
    qi = pl.program_id(2)
    scale = HEAD_DIM ** -0.5
    q0 = qi * tq
    rows4 = GROUP * tq
    q_pos4 = lax.broadcasted_iota(jnp.int32, (rows4, 1), 0) % tq + q0

    qc = _stack_group_heads(qc_ref[0] * scale)
    nc = kc_ref.shape[2]
    s = _nt_dot(qc, kc_ref[0, 0].astype(BF16))
    cmp_end = lax.broadcasted_iota(jnp.int32, (rows4, nc), 1) * CMP_STRIDE + (CMP_BLK - 1)
    mask = cmp_end <= q_pos4
    s = jnp.where(mask, s, MASK_VALUE)
    p = jnp.where(mask, jnp.exp(s - jnp.max(s, axis=1, keepdims=True)), 0.0)
    p = p / jnp.maximum(jnp.sum(p, axis=1, keepdims=True), 1e-30)
    o_c = _dot(p.astype(BF16), vc_ref[0, 0].astype(BF16))
    p_sum = p[0:tq] + p[tq:2 * tq] + p[2 * tq:3 * tq] + p[3 * tq:4 * tq]
    sel = _select_blocks(p_sum, q_pos4[0:tq], n_sel, min(N_SELECT, n_sel))

    qr = _stack_group_heads(qr_ref[0] * scale)

    _reset_softmax(m_ref, l_ref, acc_ref)

    def sel_tile(kt, c):
        ks = pl.ds(pl.multiple_of(kt * tk, tk), tk)
        s = _nt_dot(qr, ks_ref[0, 0, ks, :].astype(BF16))
        chosen = _expand_block_mask(sel, kt * tk, tk)
        kpos = lax.broadcasted_iota(jnp.int32, (tq, tk), 1) + kt * tk
        m1 = jnp.where(kpos <= q_pos4[0:tq], chosen, 0.0)
        m4 = jnp.concatenate([m1] * GROUP, axis=0) > 0.5
        _softmax_tile(s, m4, vs_ref[0, 0, ks, :].astype(BF16), m_ref, l_ref, acc_ref)
        return c

    lax.fori_loop(0, qi + 1, sel_tile, 0)
    o_s = _finish_softmax(l_ref, acc_ref)

    _reset_softmax(m_ref, l_ref, acc_ref)

    def win_tile(kt, c):
        ks = pl.ds(pl.multiple_of(kt * tk, tk), tk)
        s = _nt_dot(qr, kw_ref[0, 0, ks, :].astype(BF16))
        dist = q_pos4 - (lax.broadcasted_iota(jnp.int32, (rows4, tk), 1) + kt * tk)
        m4 = (dist >= 0) & (dist < WINDOW)
        _softmax_tile(s, m4, vw_ref[0, 0, ks, :].astype(BF16), m_ref, l_ref, acc_ref)
        return c

    kt_lo = jnp.maximum(q0 - (WINDOW - 1), 0) // tk
    lax.fori_loop(kt_lo, qi + 1, win_tile, 0)
    o_w = _finish_softmax(l_ref, acc_ref)

    gates = jax.nn.sigmoid(gate_ref[0])
    out = (_expand_gates(gates, 0, g, GLANES) * _unstack_group_heads(o_c, tq)
           + _expand_gates(gates, 1, g, GLANES) * _unstack_group_heads(o_s, tq)
           + _expand_gates(gates, 2, g, GLANES) * _unstack_group_heads(o_w, tq))
    o_ref[0] = out


def prompt_nsa(proj, q_rot, gate, kcmp, vcmp, ksel, vsel, kwin, vwin, *, tile=256):
    b, t, _ = proj.shape
    tq = tk = _row_tile(t, tile)
    nc = kcmp.shape[2]
    n_sel = t // SEL_BLK
    assert t % SEL_BLK == 0 and n_sel <= LANES and nc % LANES == 0
    kv_spec = pl.BlockSpec((1, 1, t, HEAD_DIM), lambda bi, g, i: (bi, g, 0, 0))
    cmp_spec = pl.BlockSpec((1, 1, nc, HEAD_DIM), lambda bi, g, i: (bi, g, 0, 0))
    return pl.pallas_call(
        functools.partial(_pnsa_kernel, tq=tq, tk=tk, n_sel=n_sel),
        out_shape=jax.ShapeDtypeStruct((b, t, GROUP * GLANES), F32),
        grid=(b, GROUP, t // tq),
        in_specs=[
            pl.BlockSpec((1, tq, GLANES), lambda bi, g, i: (bi, i, g)),
            pl.BlockSpec((1, tq, GLANES), lambda bi, g, i: (bi, i, g)),
            pl.BlockSpec((1, tq, LANES), lambda bi, g, i: (bi, i, 0)),
            cmp_spec, cmp_spec, kv_spec, kv_spec, kv_spec, kv_spec,
        ],
        out_specs=pl.BlockSpec((1, tq, GLANES), lambda bi, g, i: (bi, i, g)),
        scratch_shapes=[
            pltpu.VMEM((GROUP * tq, 1), F32),
            pltpu.VMEM((GROUP * tq, 1), F32),
            pltpu.VMEM((GROUP * tq, HEAD_DIM), F32),
        ],
        compiler_params=_cparams(3),
        name="prompt_nsa",
    )(proj, q_rot, gate, kcmp, vcmp, ksel, vsel, kwin, vwin)


def compress_rows(rows, pe, w1, b1, w2):
    b, g, tp, d = rows.shape
    n = tp // CMP_STRIDE
    hid = w1.shape[-1]
    a = rows.reshape(b * g * n, CMP_STRIDE * d)
    w1f = w1.reshape(CMP_BLK * d, hid)
    w1cat = jnp.concatenate([w1f[:CMP_STRIDE * d], w1f[CMP_STRIDE * d:]], axis=1).astype(BF16)
    h = matmul(a, w1cat).reshape(b, g, n, 2 * hid)
    h_first = h[..., :hid]
    h_second = jnp.concatenate([h[:, :, 1:, hid:], jnp.zeros((b, g, 1, hid), F32)], axis=2)
    bias = b1 + jnp.einsum("k,kh->h", pe.reshape(-1), w1f, precision=lax.Precision.HIGHEST)
    w2p = jnp.pad(w2, ((0, 0), (0, LANES - d))).astype(BF16)
    out = matmul(h_first.reshape(b * g * n, hid), w2p, add=h_second.reshape(b * g * n, hid), pre_bias=bias)
    return out[:, :d].reshape(b, g, n, d)


def _split2(x):
    hi = x.astype(BF16)
    lo = (x - hi.astype(F32)).astype(BF16)
    return hi, lo


def _masked_q_t(q_t):
    row = lax.broadcasted_iota(jnp.int32, q_t.shape, 0) // HEAD_DIM
    return jnp.concatenate([jnp.where(row == r, q_t, 0.0).astype(BF16) for r in range(GROUP)], axis=1)


def _causal_mask_t(tk, tq, k0, q0, strict):
    key = lax.broadcasted_iota(jnp.int32, (tk, GROUP * tq), 0) + k0
    qry = lax.broadcasted_iota(jnp.int32, (tk, GROUP * tq), 1) % tq + q0
    return key < qry if strict else key <= qry


def _softmax_step_t(s, mask, m, l):
    if mask is not None:
        s = jnp.where(mask, s, MASK_VALUE)
    m_new = jnp.maximum(m, jnp.max(s, axis=0, keepdims=True))
    alpha = jnp.exp(m - m_new)
    p = jnp.exp(s - m_new)
    if mask is not None:
        p = jnp.where(mask, p, 0.0)
    return p, alpha, m_new, alpha * l + jnp.sum(p, axis=0, keepdims=True)


def _pflash_t_kernel(*refs, kind, tq, tk, vrows, out_scale):
    if kind == "fox":
        q_ref, k_ref, vt_ref, cp_ref, o_ref, acc_ref = refs
    else:
        q_ref, k_ref, vt_ref, lam_ref, sub_ref, o_ref, acc_ref = refs
    g = pl.program_id(1)
    qi = pl.program_id(2)
    n = GROUP * tq
    q4t = _masked_q_t((q_ref[0] * (HEAD_DIM ** -0.5)).T)
    if kind == "fox":
        piece_row = lax.broadcasted_iota(jnp.int32, (LANES, n), 0)
        head = lax.broadcasted_iota(jnp.int32, (LANES, n), 1) // tq + g * GROUP
        cind = jnp.where((piece_row < 3 * FORGET_HEADS) & (piece_row % FORGET_HEADS == head), -1.0, 0.0).astype(BF16)
    acc_ref[...] = jnp.zeros(acc_ref.shape, F32)

    def tile(kt, masked, m, l):
        ks = pl.ds(pl.multiple_of(kt * tk, tk), tk)
        s = _dot(k_ref[0, ks, :].astype(BF16), q4t)
        if kind == "fox":
            s = s + _dot(cp_ref[0, ks, :], cind)
        mask = _causal_mask_t(tk, tq, kt * tk, qi * tq, False) if masked else None
        p, alpha, m, l = _softmax_step_t(s, mask, m, l)
        pb = p.astype(BF16)
        for r in range(GROUP):
            v0 = (r * HEAD_DIM // vrows) * vrows
            cols = slice(r * tq, (r + 1) * tq)
            acc_ref[r] = alpha[:, cols] * acc_ref[r] + _dot(vt_ref[0, v0:v0 + vrows, ks].astype(BF16), pb[:, cols])
        return m, l

    init = (jnp.full((1, n), MASK_VALUE, F32), jnp.zeros((1, n), F32))
    m, l = lax.fori_loop(0, qi, lambda kt, c: tile(kt, False, *c), init)
    m, l = tile(qi, True, m, l)

    inv = 1.0 / jnp.maximum(l, 1e-30)
    blk = [acc_ref[r] * inv[:, r * tq:(r + 1) * tq] for r in range(GROUP)]
    if kind == "fox":
        o_t = jnp.concatenate(blk, axis=0)
    else:
        lam = lam_ref[0, 0]
        halves = []
        for hh in range(2):
            d = blk[2 * hh] - lam * blk[2 * hh + 1]
            y = d * lax.rsqrt(jnp.mean(d * d, axis=0, keepdims=True) + NORM_EPS) * sub_ref[...]
            halves.append(y * out_scale)
        o_t = jnp.concatenate(halves, axis=0)
    o_ref[0] = o_t.T


def prompt_flash(kind, qa, qcol, ka, kcol, v_t, *, cp=None, lam=None, subln=None, out_scale=1.0, tile=256):
    b, t, _ = qa.shape
    tq = tk = _row_tile(t, tile)
    vrows = HEAD_DIM if kind == "fox" else LANES
    ins = [qa, ka, v_t]
    specs = [
        pl.BlockSpec((1, tq, GLANES), lambda bi, g, i: (bi, i, qcol + g)),
        pl.BlockSpec((1, t, GLANES), lambda bi, g, i: (bi, 0, kcol + g)),
        pl.BlockSpec((1, GLANES, t), lambda bi, g, i: (bi, g, 0)),
    ]
    if kind == "fox":
        ins.append(cp)
        specs.append(pl.BlockSpec((1, t, LANES), lambda bi, g, i: (bi, 0, 0)))
    else:
        ins += [lam.reshape(1, 1), subln.reshape(LANES, 1)]
        specs += [pl.BlockSpec(memory_space=pltpu.SMEM), pl.BlockSpec((LANES, 1), lambda bi, g, i: (0, 0))]
    return pl.pallas_call(
        functools.partial(_pflash_t_kernel, kind=kind, tq=tq, tk=tk, vrows=vrows, out_scale=out_scale),
        out_shape=jax.ShapeDtypeStruct((b, t, GROUP * GLANES), F32),
        grid=(b, GROUP, t // tq),
        in_specs=specs,
        out_specs=pl.BlockSpec((1, tq, GLANES), lambda bi, g, i: (bi, i, g)),
        scratch_shapes=[pltpu.VMEM((GROUP, vrows, tq), F32)],
        compiler_params=_cparams(3),
        name="prompt_flash_" + kind,
    )(*ins)


def _psb_t_kernel(q_ref, k_ref, vt_ref, o_ref, acc_ref, *, tq, tk):
    qi = pl.program_id(2)
    n = GROUP * tq
    q4t = _masked_q_t((q_ref[0] * (HEAD_DIM ** -0.5)).T)
    acc_ref[...] = jnp.zeros(acc_ref.shape, F32)
    row = lax.broadcasted_iota(jnp.int32, (tk, tk), 0)
    col = lax.broadcasted_iota(jnp.int32, (tk, tk), 1)
    later = jnp.where(col > row, 1.0, 0.0).astype(BF16)

    def tile(kt, masked, carry):
        ks = pl.ds(pl.multiple_of(kt * tk, tk), tk)
        z = _dot(k_ref[0, ks, :].astype(BF16), q4t)
        sp = _softplus(z)
        log_rest = -sp
        mask = _causal_mask_t(tk, tq, kt * tk, qi * tq, True) if masked else None
        if masked:
            log_rest = jnp.where(mask, log_rest, 0.0)
        hi, lo = _split2(log_rest)
        after = carry + _dot(later, hi) + _dot(later, lo)
        a = jnp.exp(z - sp + after)
        if masked:
            a = jnp.where(mask, a, 0.0)
        ab = a.astype(BF16)
        for r in range(GROUP):
            cols = slice(r * tq, (r + 1) * tq)
            acc_ref[r] += _dot(vt_ref[0, r * HEAD_DIM:(r + 1) * HEAD_DIM, ks].astype(BF16), ab[:, cols])
        return carry + jnp.sum(log_rest, axis=0, keepdims=True)

    carry = tile(qi, True, jnp.zeros((1, n), F32))
    lax.fori_loop(0, qi, lambda i, c: tile(qi - 1 - i, False, c), carry)
    o_ref[0] = jnp.concatenate([acc_ref[r] for r in range(GROUP)], axis=0).T


def prompt_sb(proj, v_t, *, tile=256):
    b, t, _ = proj.shape
    tq = tk = _row_tile(t, tile)
    return pl.pallas_call(
        functools.partial(_psb_t_kernel, tq=tq, tk=tk),
        out_shape=jax.ShapeDtypeStruct((b, t, GROUP * GLANES), F32),
        grid=(b, GROUP, t // tq),
        in_specs=[
            pl.BlockSpec((1, tq, GLANES), lambda bi, g, i: (bi, i, g)),
            pl.BlockSpec((1, t, GLANES), lambda bi, g, i: (bi, 0, GROUP + g)),
            pl.BlockSpec((1, GLANES, t), lambda bi, g, i: (bi, g, 0)),
        ],
        out_specs=pl.BlockSpec((1, tq, GLANES), lambda bi, g, i: (bi, i, g)),
        scratch_shapes=[pltpu.VMEM((GROUP, HEAD_DIM, tq), F32)],
        compiler_params=_cparams(3),
        name="prompt_sb",
    )(proj, proj, v_t)


def _group_q_t(q):
    q_t = q.T
    return jnp.concatenate([q_t[r * HEAD_DIM:(r + 1) * HEAD_DIM, :] for r in range(GROUP)], axis=1).astype(BF16)


def _select_blocks_t(p_sum_t, q_pos, n_sel, n_top):
    nc, t = p_sum_t.shape
    nb = -(-n_sel // LANES) * LANES
    j_idx = lax.broadcasted_iota(jnp.int32, (nb, nc), 0)
    c_idx = lax.broadcasted_iota(jnp.int32, (nb, nc), 1) * CMP_STRIDE
    overlap = (c_idx < j_idx * SEL_BLK + SEL_BLK) & (c_idx + CMP_BLK > j_idx * SEL_BLK) & (j_idx < n_sel)
    imp = _dot_exact_lhs01(jnp.where(overlap, 1.0, 0.0).astype(BF16), p_sum_t)
    rows = -(-n_sel // SUBLANES) * SUBLANES
    imp = imp[:rows]
    blk = lax.broadcasted_iota(jnp.int32, (rows, t), 0)
    cur = q_pos // SEL_BLK
    forced = (blk == 0) | (blk == cur) | (blk == cur - 1)
    valid = blk * SEL_BLK <= q_pos
    score = jnp.where(forced, SEL_FORCE, jnp.where(valid, imp, -1.0))
    score = jnp.where(blk < n_sel, score, -3e38)
    rank = jnp.zeros(score.shape, F32)
    for jp in range(n_sel):
        other = score[jp:jp + 1, :]
        ge = jnp.where(other >= score, 1.0, 0.0)
        gt = jnp.where(other > score, 1.0, 0.0)
        rank = rank + jnp.where(blk > jp, ge, gt)
    sel = jnp.where(rank < n_top, 1.0, 0.0)
    if rows < nb:
        sel = jnp.concatenate([sel, jnp.zeros((nb - rows, t), F32)], axis=0)
    return sel


def _pnsa_t_kernel(qc_ref, qr_ref, gate_ref, kc_ref, vct_ref, ks_ref, vst_ref, kw_ref, vwt_ref, o_ref, acc_ref,
                   *, tq, tk, n_sel):
    qi = pl.program_id(2)
    scale = HEAD_DIM ** -0.5
    q0 = qi * tq
    n = GROUP * tq
    q_pos = lax.broadcasted_iota(jnp.int32, (1, n), 1) % tq + q0

    qc = _group_q_t(qc_ref[0] * scale)
    nc = kc_ref.shape[2]
    s = _dot(kc_ref[0, 0].astype(BF16), qc)
    cmp_end = lax.broadcasted_iota(jnp.int32, (nc, n), 0) * CMP_STRIDE + (CMP_BLK - 1)
    mask = cmp_end <= q_pos
    s = jnp.where(mask, s, MASK_VALUE)
    p = jnp.where(mask, jnp.exp(s - jnp.max(s, axis=0, keepdims=True)), 0.0)
    p = p / jnp.maximum(jnp.sum(p, axis=0, keepdims=True), 1e-30)
    o_c = _dot(vct_ref[0, 0].astype(BF16), p.astype(BF16))
    p_sum = p[:, 0:tq] + p[:, tq:2 * tq] + p[:, 2 * tq:3 * tq] + p[:, 3 * tq:4 * tq]
    sel = _select_blocks_t(p_sum, q_pos[:, 0:tq], n_sel, min(N_SELECT, n_sel)).astype(BF16)
    nb = sel.shape[0]

    qr = _group_q_t(qr_ref[0] * scale)
    init = (jnp.full((1, n), MASK_VALUE, F32), jnp.zeros((1, n), F32))
    acc_ref[...] = jnp.zeros(acc_ref.shape, F32)

    def sel_tile(kt, c):
        m, l = c
        ks = pl.ds(pl.multiple_of(kt * tk, tk), tk)
        s = _dot(ks_ref[0, 0, ks, :].astype(BF16), qr)
        key = lax.broadcasted_iota(jnp.int32, (tk, nb), 0) + kt * tk
        blk = lax.broadcasted_iota(jnp.int32, (tk, nb), 1)
        chosen = _dot(jnp.where(blk == key // SEL_BLK, 1.0, 0.0).astype(BF16), sel)
        kpos = lax.broadcasted_iota(jnp.int32, (tk, tq), 0) + kt * tk
        m1 = jnp.where(kpos <= q_pos[:, 0:tq], chosen, 0.0)
        m4 = jnp.concatenate([m1] * GROUP, axis=1) > 0.5
        p, alpha, m, l = _softmax_step_t(s, m4, m, l)
        acc_ref[...] = alpha * acc_ref[...] + _dot(vst_ref[0, :, ks].astype(BF16), p.astype(BF16))
        return m, l

    m, l = lax.fori_loop(0, qi + 1, sel_tile, init)
    o_s = acc_ref[...] / jnp.maximum(l, 1e-30)
    acc_ref[...] = jnp.zeros(acc_ref.shape, F32)

    def win_tile(kt, c):
        m, l = c
        ks = pl.ds(pl.multiple_of(kt * tk, tk), tk)
        s = _dot(kw_ref[0, 0, ks, :].astype(BF16), qr)
        dist = q_pos - (lax.broadcasted_iota(jnp.int32, (tk, n), 0) + kt * tk)
        m4 = (dist >= 0) & (dist < WINDOW)
        p, alpha, m, l = _softmax_step_t(s, m4, m, l)
        acc_ref[...] = alpha * acc_ref[...] + _dot(vwt_ref[0, :, ks].astype(BF16), p.astype(BF16))
        return m, l

    kt_lo = jnp.maximum(q0 - (WINDOW - 1), 0) // tk
    m, l = lax.fori_loop(kt_lo, qi + 1, win_tile, init)
    o_w = acc_ref[...] / jnp.maximum(l, 1e-30)

    gates = jax.nn.sigmoid(gate_ref[0, 0])
    outs = []
    for r in range(GROUP):
        cols = slice(r * tq, (r + 1) * tq)
        outs.append(gates[r:r + 1, :] * o_c[:, cols] + gates[GROUP + r:GROUP + r + 1, :] * o_s[:, cols]
                    + gates[2 * GROUP + r:2 * GROUP + r + 1, :] * o_w[:, cols])
    o_ref[0] = jnp.concatenate(outs, axis=0).T


def prompt_nsa(proj, q_rot, gate_t, kcmp, vcmp_t, ksel, vsel_t, kwin, vwin_t, *, tile=256):
    b, t, _ = proj.shape
    tq = tk = _row_tile(t, tile)
    nc = kcmp.shape[2]
    n_sel = t // SEL_BLK
    assert t % SEL_BLK == 0 and nc % LANES == 0
    k_spec = pl.BlockSpec((1, 1, t, HEAD_DIM), lambda bi, g, i: (bi, g, 0, 0))
    vt_spec = pl.BlockSpec((1, HEAD_DIM, t), lambda bi, g, i: (bi, g, 0))
    return pl.pallas_call(
        functools.partial(_pnsa_t_kernel, tq=tq, tk=tk, n_sel=n_sel),
        out_shape=jax.ShapeDtypeStruct((b, t, GROUP * GLANES), F32),
        grid=(b, GROUP, t // tq),
        in_specs=[
            pl.BlockSpec((1, tq, GLANES), lambda bi, g, i: (bi, i, g)),
            pl.BlockSpec((1, tq, GLANES), lambda bi, g, i: (bi, i, g)),
            pl.BlockSpec((1, 1, 4 * GROUP, tq), lambda bi, g, i: (bi, g, 0, i)),
            pl.BlockSpec((1, 1, nc, HEAD_DIM), lambda bi, g, i: (bi, g, 0, 0)),
            pl.BlockSpec((1, 1, HEAD_DIM, nc), lambda bi, g, i: (bi, g, 0, 0)),
            k_spec, vt_spec, k_spec, vt_spec,
        ],
        out_specs=pl.BlockSpec((1, tq, GLANES), lambda bi, g, i: (bi, i, g)),
        scratch_shapes=[pltpu.VMEM((HEAD_DIM, GROUP * tq), F32)],
        compiler_params=_cparams(3),
        name="prompt_nsa",
    )(proj, q_rot, gate_t, kcmp, vcmp_t, ksel, vsel_t, kwin, vwin_t)


NEW_PAD = 16
STEP_PAGES = 4


def _rows_tile(x, reps):
    return jnp.concatenate([x] * reps, axis=0)


def _sattn_kernel(*refs, kind, n_new, heads, vdim, out_scale, step_pages):
    it = iter(refs)
    pt_ref = next(it)
    q_ref, kn_ref, vn_ref = (next(it) for _ in range(3))
    k_refs = [next(it) for _ in range(step_pages)]
    v_refs = [next(it) for _ in range(step_pages)]
    if kind == "fox":
        lf_refs = [next(it) for _ in range(step_pages)]
        nc_ref = next(it)
    if kind == "diff":
        lam_ref, sub_ref = next(it), next(it)
    o_ref = next(it)
    qs_ref = next(it)
    if kind == "sb":
        carry_ref, acc_ref = next(it), next(it)
    else:
        m_ref, l_ref, acc_ref = next(it), next(it), next(it)
    if kind == "fox":
        later_ref = next(it)
    del pt_ref
    p = pl.program_id(1)
    rows = n_new * heads
    page = k_refs[0].shape[1]

    @pl.when(p == 0)
    def _():
        qs_ref[...] = (q_ref[0] * (HEAD_DIM ** -0.5)).astype(BF16)
        acc_ref[...] = jnp.zeros(acc_ref.shape, F32)
        s = _nt_dot(qs_ref[...], kn_ref[0].astype(BF16))
        t_idx = lax.broadcasted_iota(jnp.int32, (rows, NEW_PAD), 0) // heads
        s_idx = lax.broadcasted_iota(jnp.int32, (rows, NEW_PAD), 1)
        vn = vn_ref[0].astype(BF16)
        if kind == "sb":
            carry_ref[...] = jnp.zeros(carry_ref.shape, F32)
            _sb_tile(s, s_idx < t_idx, vn, _strict_lower_ones(NEW_PAD), carry_ref, acc_ref)
        else:
            m_ref[...] = jnp.full(m_ref.shape, MASK_VALUE, F32)
            l_ref[...] = jnp.zeros(l_ref.shape, F32)
            if kind == "fox":
                later_ref[...] = jnp.zeros(later_ref.shape, F32)
                s = s - _rows_tile(nc_ref[0], n_new)
            _softmax_tile(s, s_idx <= t_idx, vn, m_ref, l_ref, acc_ref)

    scores = [_nt_dot(qs_ref[...], k_refs[j][0]) for j in range(step_pages)]
    tri = _strict_lower_ones(page)
    if kind == "sb":
        carry = carry_ref[...]
        pv = jnp.zeros(acc_ref.shape, F32)
        for j in range(step_pages):
            sp = _softplus(scores[j])
            log_rest = -sp
            hi, lo = _split2(log_rest)
            after = carry + _dot(hi, tri) + _dot(lo, tri)
            pv = pv + _dot(jnp.exp(scores[j] - sp + after).astype(BF16), v_refs[j][0])
            carry = after[:, 0:1] + log_rest[:, 0:1]
        acc_ref[...] += pv
        carry_ref[...] = carry
    else:
        if kind == "fox":
            later = later_ref[...]
            for j in range(step_pages):
                lf = lf_refs[j][0]
                suffix = _dot_exact_rhs01(lf, tri)
                scores[j] = scores[j] + _rows_tile(suffix + later, n_new)
                later = later + suffix[:, 0:1] + lf[:, 0:1]
            later_ref[...] = later
        s = jnp.concatenate(scores, axis=1)
        m_prev = m_ref[...]
        m_new = jnp.maximum(m_prev, jnp.max(s, axis=1, keepdims=True))
        alpha = jnp.exp(m_prev - m_new)
        pr = jnp.exp(s - m_new)
        l_ref[...] = alpha * l_ref[...] + jnp.sum(pr, axis=1, keepdims=True)
        pb = pr.astype(BF16)
        pv = jnp.zeros(acc_ref.shape, F32)
        for j in range(step_pages):
            pv = pv + _dot(pb[:, j * page:(j + 1) * page], v_refs[j][0])
        acc_ref[...] = alpha * acc_ref[...] + pv
        m_ref[...] = m_new

    @pl.when(p == pl.num_programs(1) - 1)
    def _():
        width = acc_ref.shape[1]
        h_idx = lax.broadcasted_iota(jnp.int32, (heads, width), 0)
        lane = lax.broadcasted_iota(jnp.int32, (heads, width), 1)
        own = lane // vdim == (h_idx * HEAD_DIM) // vdim
        if kind == "sb":
            accn = acc_ref[...]
        else:
            accn = acc_ref[...] / jnp.maximum(l_ref[...], 1e-30)
        if kind == "diff":
            sign = jnp.where(h_idx % 2 == 0, 1.0, -lam_ref[0, 0])
        out_rows = []
        for t in range(n_new):
            blk = accn[t * heads:(t + 1) * heads, :]
            if kind == "diff":
                blk = blk * sign
            out_rows.append(jnp.sum(jnp.where(own, blk, 0.0), axis=0, keepdims=True))
        out = jnp.concatenate(out_rows, axis=0)
        if kind == "diff":
            parts = []
            for hh in range(width // vdim):
                d = out[:, hh * vdim:(hh + 1) * vdim]
                parts.append(d * lax.rsqrt(jnp.mean(d * d, axis=-1, keepdims=True) + NORM_EPS) * sub_ref[...]
                             * out_scale)
            out = jnp.concatenate(parts, axis=1)
        o_ref[0] = out


def _block_diag_rows(q, heads):
    b, t, e = q.shape
    qh = q.reshape(b, t, heads, e // heads)
    eye = jnp.eye(heads, dtype=q.dtype)
    return jnp.einsum("bthd,hk->bthkd", qh, eye).reshape(b, t * heads, e)


def _pad_rows(x, n):
    return jnp.pad(x, ((0, 0), (0, n - x.shape[1]), (0, 0)))


def sample_attn(kind, page_table, q, k_new, v_new, k_cache, v_cache, *, lf_cache_t=None, new_cum_t=None,
                lam=None, subln=None, out_scale=1.0):
    b, n_new, e = q.shape
    heads = e // HEAD_DIM
    rows = n_new * heads
    n_pages = page_table.shape[1]
    page = k_cache.shape[1]
    vdim = LANES if kind == "diff" else HEAD_DIM
    last = n_pages - 1
    sp = STEP_PAGES
    assert n_pages % sp == 0 and k_cache.dtype == BF16 and v_cache.dtype == BF16

    def page_spec(j, r, c):
        return pl.BlockSpec((1, r, c), lambda bi, p, pt: (pt[bi, last - (p * sp + j)], 0, 0))

    ins = ([_block_diag_rows(q, heads), _pad_rows(k_new, NEW_PAD), _pad_rows(v_new, NEW_PAD)]
           + [k_cache] * sp + [v_cache] * sp)
    specs = [
        pl.BlockSpec((1, rows, e), lambda bi, p, pt: (bi, 0, 0)),
        pl.BlockSpec((1, NEW_PAD, e), lambda bi, p, pt: (bi, 0, 0)),
        pl.BlockSpec((1, NEW_PAD, e), lambda bi, p, pt: (bi, 0, 0)),
    ] + [page_spec(j, page, e) for j in range(sp)] * 2
    if kind == "fox":
        ins += [lf_cache_t] * sp + [new_cum_t]
        specs += [page_spec(j, heads, page) for j in range(sp)]
        specs.append(pl.BlockSpec((1, heads, NEW_PAD), lambda bi, p, pt: (bi, 0, 0)))
    if kind == "diff":
        ins += [lam.reshape(1, 1), subln.reshape(1, LANES)]
        specs += [pl.BlockSpec(memory_space=pltpu.SMEM), pl.BlockSpec((1, LANES), lambda bi, p, pt: (0, 0))]
    scratch = [pltpu.VMEM((rows, e), BF16)]
    if kind == "sb":
        scratch += [pltpu.VMEM((rows, 1), F32), pltpu.VMEM((rows, e), F32)]
    else:
        scratch += [pltpu.VMEM((rows, 1), F32), pltpu.VMEM((rows, 1), F32), pltpu.VMEM((rows, e), F32)]
    if kind == "fox":
        scratch.append(pltpu.VMEM((heads, 1), F32))
    return pl.pallas_call(
        functools.partial(_sattn_kernel, kind=kind, n_new=n_new, heads=heads, vdim=vdim, out_scale=out_scale,
                          step_pages=sp),
        out_shape=jax.ShapeDtypeStruct((b, n_new, e), F32),
        grid_spec=pltpu.PrefetchScalarGridSpec(
            num_scalar_prefetch=1,
            grid=(b, n_pages // sp),
            in_specs=specs,
            out_specs=pl.BlockSpec((1, n_new, e), lambda bi, p, pt: (bi, 0, 0)),
            scratch_shapes=scratch,
        ),
        compiler_params=_cparams(2),
        name="sample_attn_" + kind,
    )(page_table, *ins)


def _rows_repeat(x, reps):
    return jnp.concatenate([jnp.broadcast_to(x[i:i + 1], (reps, x.shape[1])) for i in range(x.shape[0])], axis=0)


def _sattn_native_kernel(*refs, kind, n_new, heads, v_heads, out_scale, step_pages, page, values_transposed):
    it = iter(refs)
    pt_ref = next(it)
    q_ref, kn_ref, vn_ref = (next(it) for _ in range(3))
    k_refs = [next(it) for _ in range(step_pages)]
    v_refs = [next(it) for _ in range(step_pages)]
    if kind == "fox":
        lf_refs = [next(it) for _ in range(step_pages)]
        nc_ref = next(it)
    if kind == "diff":
        lam_ref, sub_ref = next(it), next(it)
    o_ref = next(it)
    qs_ref = next(it)
    if kind == "sb":
        carry_ref, acc_ref = next(it), next(it)
    else:
        m_ref, l_ref, acc_ref = next(it), next(it), next(it)
    if kind == "fox":
        later_ref = next(it)
    del pt_ref
    p = pl.program_id(1)
    rows = n_new * heads
    vdim = acc_ref.shape[1]
    per_v = heads // v_heads

    def head_scores(key_of_head, keys_transposed):
        dot = _dot if keys_transposed else _nt_dot
        return jnp.concatenate([dot(qs_ref[h].astype(BF16), key_of_head(h)) for h in range(heads)], axis=0)

    def weighted_values(pr, c0, width, value_of_head, values_transposed):
        dot = _nt_dot if values_transposed else _dot
        outs = []
        for vh in range(v_heads):
            r0 = vh * per_v * n_new
            outs.append(dot(pr[r0:r0 + per_v * n_new, c0:c0 + width].astype(BF16), value_of_head(vh)))
        return jnp.concatenate(outs, axis=0)

    @pl.when(p == 0)
    def _():
        q = q_ref[0] * (HEAD_DIM ** -0.5)
        for h in range(heads):
            qs_ref[h] = q[:, h * HEAD_DIM:(h + 1) * HEAD_DIM]
        acc_ref[...] = jnp.zeros(acc_ref.shape, F32)
        kn = kn_ref[0].astype(BF16)
        vn = vn_ref[0].astype(BF16)
        s = head_scores(lambda h: kn[:, h * HEAD_DIM:(h + 1) * HEAD_DIM], False)
        t_idx = lax.broadcasted_iota(jnp.int32, (rows, NEW_PAD), 0) % n_new
        s_idx = lax.broadcasted_iota(jnp.int32, (rows, NEW_PAD), 1)
        new_value = lambda vh: vn[:, vh * vdim:(vh + 1) * vdim]
        if kind == "sb":
            mask = s_idx < t_idx
            sp = _softplus(s)
            log_rest = jnp.where(mask, -sp, 0.0)
            hi, lo = _split2(log_rest)
            tri = _strict_lower_ones(NEW_PAD)
            after = _dot(hi, tri) + _dot(lo, tri)
            a = jnp.where(mask, jnp.exp(s - sp + after), 0.0)
            acc_ref[...] = weighted_values(a, 0, NEW_PAD, new_value, False)
            carry_ref[...] = after[:, 0:1] + log_rest[:, 0:1]
        else:
            if kind == "fox":
                later_ref[...] = jnp.zeros(later_ref.shape, F32)
                s = s - _rows_repeat(nc_ref[0], n_new)
            mask = s_idx <= t_idx
            s = jnp.where(mask, s, MASK_VALUE)
            m_new = jnp.max(s, axis=1, keepdims=True)
            pr = jnp.where(mask, jnp.exp(s - m_new), 0.0)
            m_ref[...] = m_new
            l_ref[...] = jnp.sum(pr, axis=1, keepdims=True)
            acc_ref[...] = weighted_values(pr, 0, NEW_PAD, new_value, False)

    def page_keys(j):
        return lambda h: k_refs[j][0, h].astype(BF16)

    def page_values(j):
        if values_transposed:
            return lambda vh: v_refs[j][0, vh].astype(BF16)
        return lambda vh: v_refs[j][0, pl.ds(vh, page, stride=v_heads), :].astype(BF16)

    scores = [head_scores(page_keys(j), True) for j in range(step_pages)]
    tri = _strict_lower_ones(page)
    if kind == "sb":
        carry = carry_ref[...]
        pv = jnp.zeros(acc_ref.shape, F32)
        for j in range(step_pages):
            sp = _softplus(scores[j])
            log_rest = -sp
            hi, lo = _split2(log_rest)
            after = carry + _dot(hi, tri) + _dot(lo, tri)
            pv = pv + weighted_values(jnp.exp(scores[j] - sp + after), 0, page, page_values(j), values_transposed)
            carry = after[:, 0:1] + log_rest[:, 0:1]
        acc_ref[...] += pv
        carry_ref[...] = carry
    else:
        if kind == "fox":
            later = later_ref[...]
            for j in range(step_pages):
                lf = lf_refs[j][0]
                suffix = _dot_exact_rhs01(lf, tri)
                scores[j] = scores[j] + _rows_repeat(suffix + later, n_new)
                later = later + suffix[:, 0:1] + lf[:, 0:1]
            later_ref[...] = later
        s = jnp.concatenate(scores, axis=1)
        m_prev = m_ref[...]
        m_new = jnp.maximum(m_prev, jnp.max(s, axis=1, keepdims=True))
        alpha = jnp.exp(m_prev - m_new)
        pr = jnp.exp(s - m_new)
        l_ref[...] = alpha * l_ref[...] + jnp.sum(pr, axis=1, keepdims=True)
        pv = jnp.zeros(acc_ref.shape, F32)
        for j in range(step_pages):
            pv = pv + weighted_values(pr, j * page, page, page_values(j), values_transposed)
        acc_ref[...] = alpha * acc_ref[...] + pv
        m_ref[...] = m_new

    @pl.when(p == pl.num_programs(1) - 1)
    def _():
        if kind == "sb":
            accn = acc_ref[...]
        else:
            accn = acc_ref[...] / jnp.maximum(l_ref[...], 1e-30)
        blk = lambda h: accn[h * n_new:(h + 1) * n_new, :]
        if kind == "diff":
            lam = lam_ref[0, 0]
            parts = []
            for vh in range(v_heads):
                d = blk(2 * vh) - lam * blk(2 * vh + 1)
                parts.append(d * lax.rsqrt(jnp.mean(d * d, axis=-1, keepdims=True) + NORM_EPS) * sub_ref[...]
                             * out_scale)
            o_ref[0] = jnp.concatenate(parts, axis=1)
        else:
            o_ref[0] = jnp.concatenate([blk(h) for h in range(heads)], axis=1)


def sample_attn_native(kind, page_table, q, k_new, v_new, k_cache, v_cache, *, lf_cache_t=None, new_cum_t=None,
                       lam=None, subln=None, out_scale=1.0):
    b, n_new, e = q.shape
    pool, page, heads, _ = k_cache.shape
    v_heads, vdim = v_cache.shape[2:]
    rows = n_new * heads
    n_pages = page_table.shape[1]
    last = n_pages - 1
    sp = STEP_PAGES
    assert n_pages % sp == 0
    values_transposed = vdim < LANES
    assert values_transposed or v_heads % SUBLANES == 0

    def page_spec(j, *blk):
        zeros = (0,) * len(blk)
        return pl.BlockSpec((1,) + blk, lambda bi, p, pt: (pt[bi, last - (p * sp + j)],) + zeros)

    k_t = k_cache.transpose(0, 2, 3, 1)
    if values_transposed:
        v_in, v_blk = v_cache.transpose(0, 2, 3, 1), (v_heads, vdim, page)
    else:
        v_in, v_blk = v_cache.reshape(pool, page * v_heads, vdim), (page * v_heads, vdim)
    ins = [q, _pad_rows(k_new, NEW_PAD), _pad_rows(v_new, NEW_PAD)] + [k_t] * sp + [v_in] * sp
    specs = ([pl.BlockSpec((1, n_new, e), lambda bi, p, pt: (bi, 0, 0))]
             + [pl.BlockSpec((1, NEW_PAD, e), lambda bi, p, pt: (bi, 0, 0))] * 2
             + [page_spec(j, heads, HEAD_DIM, page) for j in range(sp)]
             + [page_spec(j, *v_blk) for j in range(sp)])
    if kind == "fox":
        ins += [lf_cache_t] * sp + [new_cum_t]
        specs += [page_spec(j, heads, page) for j in range(sp)]
        specs.append(pl.BlockSpec((1, heads, NEW_PAD), lambda bi, p, pt: (bi, 0, 0)))
    if kind == "diff":
        ins += [lam.reshape(1, 1), subln.reshape(1, LANES)]
        specs += [pl.BlockSpec(memory_space=pltpu.SMEM), pl.BlockSpec((1, LANES), lambda bi, p, pt: (0, 0))]
    scratch = [pltpu.VMEM((heads, n_new, HEAD_DIM), F32)]
    if kind == "sb":
        scratch += [pltpu.VMEM((rows, 1), F32), pltpu.VMEM((rows, vdim), F32)]
    else:
        scratch += [pltpu.VMEM((rows, 1), F32), pltpu.VMEM((rows, 1), F32), pltpu.VMEM((rows, vdim), F32)]
    if kind == "fox":
        scratch.append(pltpu.VMEM((heads, 1), F32))
    return pl.pallas_call(
        functools.partial(_sattn_native_kernel, kind=kind, n_new=n_new, heads=heads, v_heads=v_heads,
                          out_scale=out_scale, step_pages=sp, page=page, values_transposed=values_transposed),
        out_shape=jax.ShapeDtypeStruct((b, n_new, e), F32),
        grid_spec=pltpu.PrefetchScalarGridSpec(
            num_scalar_prefetch=1,
            grid=(b, n_pages // sp),
            in_specs=specs,
            out_specs=pl.BlockSpec((1, n_new, e), lambda bi, p, pt: (bi, 0, 0)),
            scratch_shapes=scratch,
        ),
        compiler_params=_cparams(2),
        name="sample_attn_" + kind,
    )(page_table, *ins)


GATHER_PAGES = 8


def _gather_groups_kernel(pt_ref, *refs):
    del pt_ref
    page_refs, new_ref, o_ref = refs[:GATHER_PAGES], refs[GATHER_PAGES], refs[GATHER_PAGES + 1]
    page = new_ref.shape[1]
    is_new = pl.program_id(1) == pl.num_programs(1) - 1

    @pl.when(is_new)
    def _():
        o_ref[...] = jnp.zeros(o_ref.shape, F32)
        for g in range(GROUP):
            o_ref[0, g, 0:page, :] = new_ref[0, :, g * HEAD_DIM:(g + 1) * HEAD_DIM]

    @pl.when(jnp.logical_not(is_new))
    def _():
        for j in range(GATHER_PAGES):
            for g in range(GROUP):
                o_ref[0, g, j * page:(j + 1) * page, :] = page_refs[j][0, g].T


def gather_groups(cache_t, page_table, new_rows):
    b, n_pages = page_table.shape
    page = cache_t.shape[3]
    gp = GATHER_PAGES
    assert n_pages % gp == 0
    steps = n_pages // gp
    new_pad = _pad_rows(new_rows, page)

    def page_spec(j):
        return pl.BlockSpec((1, GROUP, HEAD_DIM, page),
                            lambda bi, p, pt: (pt[bi, jnp.minimum(p, steps - 1) * gp + j], 0, 0, 0))

    return pl.pallas_call(
        _gather_groups_kernel,
        out_shape=jax.ShapeDtypeStruct((b, GROUP, (n_pages + gp) * page, HEAD_DIM), F32),
        grid_spec=pltpu.PrefetchScalarGridSpec(
            num_scalar_prefetch=1,
            grid=(b, steps + 1),
            in_specs=[page_spec(j) for j in range(gp)]
            + [pl.BlockSpec((1, page, GLANES), lambda bi, p, pt: (bi, 0, 0))],
            out_specs=pl.BlockSpec((1, GROUP, gp * page, HEAD_DIM), lambda bi, p, pt: (bi, 0, p, 0)),
        ),
        compiler_params=_cparams(2),
        name="gather_groups",
    )(page_table, *([cache_t] * gp), new_pad)


def _snsa_kernel(*refs, n_new, pos0, n_sel, step_pages):
    (pt_ref, qc_ref, qr_ref, gate_ref, kc_ref, vc_ref, kwb_ref, vwb_ref, ksn_ref, vsn_ref, kwn_ref,
     vwn_ref) = refs[:12]
    ks_refs = refs[12:12 + step_pages]
    vs_refs = refs[12 + step_pages:12 + 2 * step_pages]
    o_ref, qs_ref, qg_ref, sel_ref, oc_ref, ow_ref, m_ref, l_ref, acc_ref, accs_ref = refs[12 + 2 * step_pages:]
    del pt_ref
    p = pl.program_id(1)
    heads = GROUP * GROUP
    rows = n_new * heads
    grows = GROUP * n_new
    scale = HEAD_DIM ** -0.5
    page = ks_refs[0].shape[3]

    def own_lanes(x):
        return jnp.concatenate([x[g * grows:(g + 1) * grows, g * HEAD_DIM:(g + 1) * HEAD_DIM]
                                for g in range(GROUP)], axis=0)

    def group_scores(key_of_group, keys_transposed):
        dot = _dot if keys_transposed else _nt_dot
        return jnp.concatenate([dot(qg_ref[g].astype(BF16), key_of_group(g)) for g in range(GROUP)], axis=0)

    def group_values(pr, c0, width, value_of_group, values_transposed):
        dot = _nt_dot if values_transposed else _dot
        return jnp.concatenate([dot(pr[g * grows:(g + 1) * grows, c0:c0 + width].astype(BF16), value_of_group(g))
                                for g in range(GROUP)], axis=0)

    @pl.when(p == 0)
    def _():
        t_idx = lax.broadcasted_iota(jnp.int32, (rows, 1), 0) % n_new
        q_pos = t_idx + pos0
        nc = kc_ref.shape[1]
        s = _nt_dot((qc_ref[0] * scale).astype(BF16), kc_ref[0].astype(BF16))
        cmp_end = lax.broadcasted_iota(jnp.int32, (rows, nc), 1) * CMP_STRIDE + (CMP_BLK - 1)
        mask = cmp_end <= q_pos
        s = jnp.where(mask, s, MASK_VALUE)
        pc = jnp.where(mask, jnp.exp(s - jnp.max(s, axis=1, keepdims=True)), 0.0)
        pc = pc / jnp.maximum(jnp.sum(pc, axis=1, keepdims=True), 1e-30)
        oc_ref[...] = own_lanes(_dot(pc.astype(BF16), vc_ref[0].astype(BF16)))
        i_idx = lax.broadcasted_iota(jnp.int32, (rows, rows), 0)
        r_idx = lax.broadcasted_iota(jnp.int32, (rows, rows), 1)
        same = (r_idx // grows == i_idx // n_new) & (r_idx % n_new == i_idx % n_new)
        group_sum = jnp.where(same & (i_idx < grows), 1.0, 0.0).astype(BF16)
        p_sum = _dot_exact_lhs01(group_sum, pc)
        g_pos = lax.broadcasted_iota(jnp.int32, (rows, 1), 0) % n_new + pos0
        sel = _select_blocks(p_sum, g_pos, n_sel, min(N_SELECT, n_sel))
        spread = jnp.where(r_idx == (i_idx // grows) * n_new + i_idx % n_new, 1.0, 0.0).astype(BF16)
        sel_ref[...] = _dot(spread, sel.astype(BF16)).astype(BF16)
        qr = qr_ref[0] * scale
        qs_ref[...] = qr.astype(BF16)
        for g in range(GROUP):
            qg_ref[g] = qr[g * grows:(g + 1) * grows, g * HEAD_DIM:(g + 1) * HEAD_DIM]
        _reset_softmax(m_ref, l_ref, acc_ref)
        wlen = kwb_ref.shape[1]
        s = _nt_dot(qs_ref[...], kwb_ref[0].astype(BF16))
        dist = t_idx + wlen - lax.broadcasted_iota(jnp.int32, (rows, wlen), 1)
        _softmax_tile(s, (dist >= 0) & (dist < WINDOW), vwb_ref[0].astype(BF16), m_ref, l_ref, acc_ref)
        s = _nt_dot(qs_ref[...], kwn_ref[0].astype(BF16))
        s_idx = lax.broadcasted_iota(jnp.int32, (rows, NEW_PAD), 1)
        _softmax_tile(s, s_idx <= t_idx, vwn_ref[0].astype(BF16), m_ref, l_ref, acc_ref)
        ow_ref[...] = own_lanes(_finish_softmax(l_ref, acc_ref))
        ksn = ksn_ref[0].astype(BF16)
        vsn = vsn_ref[0].astype(BF16)
        s = group_scores(lambda g: ksn[:, g * HEAD_DIM:(g + 1) * HEAD_DIM], False)
        chosen = _expand_block_mask(sel_ref[...], pos0, NEW_PAD)
        mask = jnp.where(s_idx <= t_idx, chosen, 0.0) > 0.5
        s = jnp.where(mask, s, MASK_VALUE)
        m_new = jnp.max(s, axis=1, keepdims=True)
        pr = jnp.where(mask, jnp.exp(s - m_new), 0.0)
        m_ref[...] = m_new
        l_ref[...] = jnp.sum(pr, axis=1, keepdims=True)
        accs_ref[...] = group_values(pr, 0, NEW_PAD, lambda g: vsn[:, g * HEAD_DIM:(g + 1) * HEAD_DIM], False)

    scores = [group_scores(lambda g, j=j: ks_refs[j][0, g].astype(BF16), True) for j in range(step_pages)]
    chosen = [_expand_block_mask(sel_ref[...], (p * step_pages + j) * page, page) for j in range(step_pages)]
    mask = jnp.concatenate(chosen, axis=1) > 0.5
    s = jnp.where(mask, jnp.concatenate(scores, axis=1), MASK_VALUE)
    m_prev = m_ref[...]
    m_new = jnp.maximum(m_prev, jnp.max(s, axis=1, keepdims=True))
    alpha = jnp.exp(m_prev - m_new)
    pr = jnp.where(mask, jnp.exp(s - m_new), 0.0)
    l_ref[...] = alpha * l_ref[...] + jnp.sum(pr, axis=1, keepdims=True)
    pv = jnp.zeros(accs_ref.shape, F32)
    for j in range(step_pages):
        pv = pv + group_values(pr, j * page, page, lambda g, j=j: vs_refs[j][0, g].astype(BF16), True)
    accs_ref[...] = alpha * accs_ref[...] + pv
    m_ref[...] = m_new

    @pl.when(p == pl.num_programs(1) - 1)
    def _():
        o_s = _finish_softmax(l_ref, accs_ref)
        gates = jax.nn.sigmoid(gate_ref[0])
        o_ref[0] = gates[:, 0:1] * oc_ref[...] + gates[:, 1:2] * o_s + gates[:, 2:3] * ow_ref[...]


def _group_rows(q):
    b, t, _ = q.shape
    qh = q.reshape(b, t, GROUP, GROUP, HEAD_DIM)
    eye = jnp.eye(GROUP, dtype=q.dtype)
    return jnp.einsum("btgrd,gk->bgrtkd", qh, eye).reshape(b, t * GROUP * GROUP, GLANES)


def sample_nsa(page_table, pos0, qc, qr, gate_rows, kcmp, vcmp, kwin_buf, vwin_buf, ks_new, vs_new, kw_new,
               vw_new, ks_cache, vs_cache):
    b, rows, _ = qc.shape
    n_pages = page_table.shape[1]
    page = ks_cache.shape[3]
    n_new = rows // (GROUP * GROUP)
    assert pos0 == n_pages * page and pos0 % SEL_BLK == 0 and n_new <= SEL_BLK
    n_sel = -(-(pos0 + n_new) // SEL_BLK)
    nb = -(-n_sel // LANES) * LANES
    nc = kcmp.shape[1]
    wlen = kwin_buf.shape[1]
    per_b = lambda r, c: pl.BlockSpec((1, r, c), lambda bi, p, pt: (bi, 0, 0))
    sp = STEP_PAGES
    assert n_pages % sp == 0

    def page_spec(j):
        return pl.BlockSpec((1, GROUP, HEAD_DIM, page), lambda bi, p, pt: (pt[bi, p * sp + j], 0, 0, 0))

    return pl.pallas_call(
        functools.partial(_snsa_kernel, n_new=n_new, pos0=pos0, n_sel=n_sel, step_pages=sp),
        out_shape=jax.ShapeDtypeStruct((b, rows, HEAD_DIM), F32),
        grid_spec=pltpu.PrefetchScalarGridSpec(
            num_scalar_prefetch=1,
            grid=(b, n_pages // sp),
            in_specs=[per_b(rows, GLANES), per_b(rows, GLANES), per_b(rows, SUBLANES), per_b(nc, GLANES),
                      per_b(nc, GLANES), per_b(wlen, GLANES), per_b(wlen, GLANES)]
            + [per_b(NEW_PAD, GLANES)] * 4 + [page_spec(j) for j in range(sp)] * 2,
            out_specs=per_b(rows, HEAD_DIM),
            scratch_shapes=[
                pltpu.VMEM((rows, GLANES), BF16),
                pltpu.VMEM((GROUP, rows // GROUP, HEAD_DIM), F32),
                pltpu.VMEM((rows, nb), BF16),
                pltpu.VMEM((rows, HEAD_DIM), F32),
                pltpu.VMEM((rows, HEAD_DIM), F32),
                pltpu.VMEM((rows, 1), F32),
                pltpu.VMEM((rows, 1), F32),
                pltpu.VMEM((rows, GLANES), F32),
                pltpu.VMEM((rows, HEAD_DIM), F32),
            ],
        ),
        compiler_params=_cparams(2),
        name="sample_nsa",
    )(page_table, qc, qr, gate_rows, kcmp, vcmp, kwin_buf, vwin_buf, ks_new, vs_new, kw_new, vw_new,
      *([ks_cache] * sp), *([vs_cache] * sp))


def sample_nsa_mixer(proj, gate, pos0, cos_t, sin_t, cache, page_table, p):
    b, t, _ = proj.shape
    pool, page = cache["nsa_ksel"].shape[:2]
    flat = lambda c: c.reshape(c.shape[0], c.shape[1], GLANES)
    q_rot = rope_cols(proj, 0, 1024, cos_t, sin_t)
    ks = rope_cols(proj, 1536, 256, cos_t, sin_t)
    kw = rope_cols(proj, 2048, 256, cos_t, sin_t)
    kc, vc = proj[..., 1024:1280], proj[..., 1280:1536]
    vs, vw = proj[..., 1792:2048], proj[..., 2304:2560]

    def summaries(cache_rows, new_rows, idx):
        rows = gather_groups(cache_rows.transpose(0, 2, 3, 1), page_table, new_rows)
        cmp = compress_rows(rows, p["nsa_cmp_pe"][idx], p["nsa_cmp_w1"][idx], p["nsa_cmp_b1"][idx],
                            p["nsa_cmp_w2"][idx])
        cmp = cmp.transpose(0, 2, 1, 3).reshape(b, cmp.shape[2], GLANES)
        return _pad_rows(cmp, -(-cmp.shape[1] // LANES) * LANES)

    kcmp = summaries(cache["nsa_kcmp"], kc, 0)
    vcmp = summaries(cache["nsa_vcmp"], vc, 1)
    heads = GROUP * GROUP
    gate_rows = gate[..., :3 * heads].reshape(b, t, 3, heads).transpose(0, 3, 1, 2).reshape(b, heads * t, 3)
    gate_rows = jnp.pad(gate_rows, ((0, 0), (0, 0), (0, SUBLANES - 3)))
    newp = lambda a: _pad_rows(a, NEW_PAD)
    o_rows = sample_nsa(page_table, pos0, _group_rows(proj[..., :1024]), _group_rows(q_rot), gate_rows, kcmp, vcmp,
                        flat(cache["nsa_kwin"]), flat(cache["nsa_vwin"]), newp(ks), newp(vs), newp(kw), newp(vw),
                        cache["nsa_ksel"].transpose(0, 2, 3, 1), cache["nsa_vsel"].transpose(0, 2, 3, 1))
    o = o_rows.reshape(b, heads, t, HEAD_DIM).transpose(0, 2, 1, 3).reshape(b, t, heads * HEAD_DIM)
    g4 = lambda a: a.reshape(b, t, GROUP, HEAD_DIM)
    wlen = min(WINDOW, pos0 + t)
    kw_all = jnp.concatenate([cache["nsa_kwin"], g4(kw)], axis=1)
    vw_all = jnp.concatenate([cache["nsa_vwin"], g4(vw)], axis=1)
    return o, (g4(kc), g4(vc), g4(ks), g4(vs)), (kw_all[:, -wlen:], vw_all[:, -wlen:])


def _rmsnorm_kernel(x_ref, g_ref, o_ref):
    x = x_ref[...]
    o_ref[...] = x * lax.rsqrt(jnp.mean(x * x, axis=-1, keepdims=True) + NORM_EPS) * g_ref[...]


def rmsnorm_rows(x, gain):
    n, d = x.shape
    tm = _row_tile(n, 1024)
    return pl.pallas_call(
        _rmsnorm_kernel,
        out_shape=jax.ShapeDtypeStruct((n, d), F32),
        grid=(n // tm,),
        in_specs=[pl.BlockSpec((tm, d), lambda i: (i, 0)), pl.BlockSpec((1, d), lambda i: (0, 0))],
        out_specs=pl.BlockSpec((tm, d), lambda i: (i, 0)),
        compiler_params=_cparams(1),
        name="rmsnorm",
    )(x, gain.reshape(1, d))


D_MODEL = 1024
DEPTH = 4
D_FF = 2816
CONV_W = 3
X_HEADS = 4
X_HDIM = 128


def _pad_cols(w, n):
    return jnp.pad(w, ((0, 0), (0, n - w.shape[1])))


def _pad_vec(v, n):
    return jnp.pad(v, (0, n - v.shape[0]))


def _to_groups(x):
    b, t, _ = x.shape
    return x.reshape(b, t, GROUP, HEAD_DIM).transpose(0, 2, 1, 3)


def prepare_weights(p):
    w = {}
    hd = FORGET_HEADS * HEAD_DIM
    w["fox_main"] = p["fox_w_in"][:, :3 * hd].astype(BF16)
    w["fox_gate"] = _pad_cols(p["fox_w_in"][:, 3 * hd:], LANES).astype(BF16)
    w["fox_bias"] = _pad_vec(p["fox_b_f"], LANES)
    w["fox_out"] = p["fox_w_out"].astype(BF16)
    w["sb_in"] = p["sb_w_in"].astype(BF16)
    w["sb_out"] = p["sb_w_out"].astype(BF16)
    nsa_main = hd + 6 * GLANES
    w["nsa_main"] = p["nsa_w_in"][:, :nsa_main].astype(BF16)
    w["nsa_gate"] = _pad_cols(p["nsa_w_in"][:, nsa_main:], LANES).astype(BF16)
    w["nsa_out"] = p["nsa_w_out"].astype(BF16)
    w["diff_in"] = p["diff_w_in"].astype(BF16)
    w["diff_out"] = p["diff_w_out"].astype(BF16)
    w["x_q"] = p["x_w_q"].astype(BF16)
    w["x_o"] = p["x_w_o"].astype(BF16)
    w["ffn_up"] = p["ffn_w_up"].astype(BF16)
    w["ffn_down"] = p["ffn_w_down"].astype(BF16)
    w["conv_pack"] = jnp.concatenate(
        [p["ffn_conv_w"], p["ffn_conv_b"][:, None, :], jnp.zeros((DEPTH, SUBLANES - CONV_W - 1, 2 * D_FF), F32)],
        axis=1)
    return w


def _diff_lambda(p, lambda_init):
    return (jnp.exp(jnp.sum(p["diff_lq1"] * p["diff_lk1"])) - jnp.exp(jnp.sum(p["diff_lq2"] * p["diff_lk2"]))
            + lambda_init)


def _lambda_init(layer):
    return 0.8 - 0.6 * math.exp(-0.3 * layer)


def _layer_tail(x, layer, p, w, mem_k, mem_v, state8):
    b, t, d = x.shape
    x = cross_block(x, p["norm_mem"][layer], w["x_q"][layer], mem_k[layer].reshape(b, -1, X_HEADS * X_HDIM),
                    mem_v[layer].reshape(b, -1, X_HEADS * X_HDIM), w["x_o"][layer])
    x2 = x.reshape(b * t, d)
    u = matmul(x2, w["ffn_up"][layer], gain=p["norm_ffn"][layer]).reshape(b, t, 2 * D_FF)
    act = conv_gate(u, state8, w["conv_pack"][layer])
    x = matmul(act.reshape(b * t, D_FF), w["ffn_down"][layer], res=x2).reshape(b, t, d)
    return x, u


def prompt_trunk(x, mem_k, mem_v, p, w):
    b, t, d = x.shape
    n = b * t
    new = {}
    conv_rows = []
    cos_t, sin_t = rope_tables(jnp.arange(t))
    zero_state = jnp.zeros((b, SUBLANES, 2 * D_FF), F32)
    for layer in range(DEPTH):
        kind = layer % 4
        x2 = x.reshape(n, d)
        gain = p["norm_mix"][layer]
        if kind == 0:
            proj = matmul(x2, w["fox_main"], gain=gain).reshape(b, t, -1)
            gate = matmul(x2, w["fox_gate"], gain=gain).reshape(b, t, LANES)
            logf, _, cum_pieces = logf_cumsum(gate, w["fox_bias"])
            v_t = proj[..., 2048:3072].transpose(0, 2, 1)
            o = prompt_flash("fox", proj, 0, proj, 4, v_t, cp=cum_pieces)
            x = matmul(o.reshape(n, -1), w["fox_out"], res=x2).reshape(b, t, d)
            new["fox_k"] = proj[..., 1024:2048].reshape(b, t, 16, 64)
            new["fox_v"] = proj[..., 2048:3072].reshape(b, t, 16, 64)
            new["fox_logf"] = logf[..., :FORGET_HEADS]
        elif kind == 1:
            proj = matmul(x2, w["sb_in"], gain=gain).reshape(b, t, -1)
            o = prompt_sb(proj, proj[..., 2048:3072].transpose(0, 2, 1))
            x = matmul(o.reshape(n, -1), w["sb_out"], res=x2).reshape(b, t, d)
            new["sb_k"] = proj[..., 1024:2048].reshape(b, t, 16, 64)
            new["sb_v"] = proj[..., 2048:3072].reshape(b, t, 16, 64)
        elif kind == 2:
            proj = matmul(x2, w["nsa_main"], gain=gain).reshape(b, t, -1)
            gate = matmul(x2, w["nsa_gate"], gain=gain).reshape(b, t, LANES)
            q_rot = rope_cols(proj, 0, 1024, cos_t, sin_t)
            kc, vc = proj[..., 1024:1280], proj[..., 1280:1536]
            vs, vw = proj[..., 1792:2048], proj[..., 2304:2560]
            ks = rope_cols(proj, 1536, 256, cos_t, sin_t)
            kw = rope_cols(proj, 2048, 256, cos_t, sin_t)
            tp = -(-t // SEL_BLK) * SEL_BLK
            tpad = ((0, 0), (0, 0), (0, tp - t), (0, 0))
            kcmp = compress_rows(jnp.pad(_to_groups(kc), tpad), p["nsa_cmp_pe"][0], p["nsa_cmp_w1"][0],
                                 p["nsa_cmp_b1"][0], p["nsa_cmp_w2"][0])
            vcmp = compress_rows(jnp.pad(_to_groups(vc), tpad), p["nsa_cmp_pe"][1], p["nsa_cmp_w1"][1],
                                 p["nsa_cmp_b1"][1], p["nsa_cmp_w2"][1])
            ncp = -(-kcmp.shape[2] // LANES) * LANES
            cpad = ((0, 0), (0, 0), (0, ncp - kcmp.shape[2]), (0, 0))
            gate_t = gate[..., :3 * GROUP * GROUP].reshape(b, t, 3, GROUP, GROUP).transpose(0, 3, 2, 4, 1)
            gate_t = jnp.pad(gate_t.reshape(b, GROUP, 3 * GROUP, t), ((0, 0), (0, 0), (0, GROUP), (0, 0)))
            o = prompt_nsa(proj, q_rot, gate_t, jnp.pad(kcmp, cpad), jnp.pad(vcmp, cpad).transpose(0, 1, 3, 2),
                           _to_groups(ks), vs.transpose(0, 2, 1), _to_groups(kw), vw.transpose(0, 2, 1))
            x = matmul(o.reshape(n, -1), w["nsa_out"], res=x2).reshape(b, t, d)
            g4 = lambda a: a.reshape(b, t, GROUP, HEAD_DIM)
            new["nsa_kcmp"], new["nsa_vcmp"], new["nsa_ksel"], new["nsa_vsel"] = g4(kc), g4(vc), g4(ks), g4(vs)
            wlen = min(WINDOW, t)
            new["nsa_kwin"], new["nsa_vwin"] = g4(kw)[:, t - wlen:], g4(vw)[:, t - wlen:]
        else:
            proj = matmul(x2, w["diff_in"], gain=gain).reshape(b, t, -1)
            qk = rope_cols(proj, 0, 2048, cos_t, sin_t)
            li = _lambda_init(layer)
            o = prompt_flash("diff", qk, 0, qk, 4, proj[..., 2048:3072].transpose(0, 2, 1),
                             lam=_diff_lambda(p, li), subln=p["diff_subln"], out_scale=1.0 - li)
            x = matmul(o.reshape(n, -1), w["diff_out"], res=x2).reshape(b, t, d)
            new["diff_k"] = qk[..., 1024:2048].reshape(b, t, 16, 64)
            new["diff_v"] = proj[..., 2048:3072].reshape(b, t, 8, 128)
        x, u = _layer_tail(x, layer, p, w, mem_k, mem_v, zero_state)
        conv_rows.append(u[:, t - (CONV_W - 1):])
    new["ffn_conv"] = jnp.stack(conv_rows)
    y = rmsnorm_rows(x.reshape(n, d), p["norm_final"]).reshape(b, t, d)
    return y, new


def sample_trunk(x, pos0, mem_k, mem_v, conv_state, cache, page_table, p, w):
    b, t, d = x.shape
    n = b * t
    new = {}
    conv_rows = []
    cos_t, sin_t = rope_tables(pos0 + jnp.arange(t))
    pool = cache["fox_k"].shape[0]
    page = cache["fox_k"].shape[1]
    flat = lambda c: c.reshape(pool, page, -1).astype(BF16)
    for layer in range(DEPTH):
        kind = layer % 4
        x2 = x.reshape(n, d)
        gain = p["norm_mix"][layer]
        if kind == 0:
            proj = matmul(x2, w["fox_main"], gain=gain).reshape(b, t, -1)
            gate = matmul(x2, w["fox_gate"], gain=gain).reshape(b, t, LANES)
            logf, cum, _ = logf_cumsum(gate, w["fox_bias"])
            new_cum_t = _pad_rows(cum[..., :FORGET_HEADS], NEW_PAD).transpose(0, 2, 1)
            o = sample_attn_native("fox", page_table, proj[..., :1024], proj[..., 1024:2048], proj[..., 2048:3072],
                                   cache["fox_k"], cache["fox_v"],
                                   lf_cache_t=cache["fox_logf"].transpose(0, 2, 1), new_cum_t=new_cum_t)
            x = matmul(o.reshape(n, -1), w["fox_out"], res=x2).reshape(b, t, d)
            new["fox_k"] = proj[..., 1024:2048].reshape(b, t, 16, 64)
            new["fox_v"] = proj[..., 2048:3072].reshape(b, t, 16, 64)
            new["fox_logf"] = logf[..., :FORGET_HEADS]
        elif kind == 1:
            proj = matmul(x2, w["sb_in"], gain=gain).reshape(b, t, -1)
            o = sample_attn_native("sb", page_table, proj[..., :1024], proj[..., 1024:2048], proj[..., 2048:3072],
                                   cache["sb_k"], cache["sb_v"])
            x = matmul(o.reshape(n, -1), w["sb_out"], res=x2).reshape(b, t, d)
            new["sb_k"] = proj[..., 1024:2048].reshape(b, t, 16, 64)
            new["sb_v"] = proj[..., 2048:3072].reshape(b, t, 16, 64)
        elif kind == 2:
            proj = matmul(x2, w["nsa_main"], gain=gain).reshape(b, t, -1)
            gate = matmul(x2, w["nsa_gate"], gain=gain).reshape(b, t, LANES)
            o, rows, win = sample_nsa_mixer(proj, gate, pos0, cos_t, sin_t, cache, page_table, p)
            x = matmul(o.reshape(n, -1), w["nsa_out"], res=x2).reshape(b, t, d)
            new["nsa_kcmp"], new["nsa_vcmp"], new["nsa_ksel"], new["nsa_vsel"] = rows
            new["nsa_kwin"], new["nsa_vwin"] = win
        else:
            proj = matmul(x2, w["diff_in"], gain=gain).reshape(b, t, -1)
            qk = rope_cols(proj, 0, 2048, cos_t, sin_t)
            li = _lambda_init(layer)
            o = sample_attn_native("diff", page_table, qk[..., :1024], qk[..., 1024:2048], proj[..., 2048:3072],
                                   cache["diff_k"], cache["diff_v"],
                                   lam=_diff_lambda(p, li), subln=p["diff_subln"], out_scale=1.0 - li)
            x = matmul(o.reshape(n, -1), w["diff_out"], res=x2).reshape(b, t, d)
            new["diff_k"] = qk[..., 1024:2048].reshape(b, t, 16, 64)
            new["diff_v"] = proj[..., 2048:3072].reshape(b, t, 8, 128)
        state8 = jnp.pad(conv_state[layer], ((0, 0), (SUBLANES - (CONV_W - 1), 0), (0, 0)))
        x, u = _layer_tail(x, layer, p, w, mem_k, mem_v, state8)
        ext = jnp.concatenate([conv_state[layer], u], axis=1)
        conv_rows.append(ext[:, -(CONV_W - 1):])
    new["ffn_conv"] = jnp.stack(conv_rows)
    y = rmsnorm_rows(x.reshape(n, d), p["norm_final"]).reshape(b, t, d)
    return y, new


def memory_kv(mem, mem_norm, w_kv):
    b, m, d = mem.shape
    e = X_HEADS * X_HDIM
    kv = jnp.stack([matmul(mem.reshape(b * m, d), w_kv[layer].astype(BF16), gain=mem_norm[layer])
                    for layer in range(DEPTH)])
    xk = kv[..., :e].reshape(DEPTH, b, m, X_HEADS, X_HDIM)
    xv = kv[..., e:].reshape(DEPTH, b, m, X_HEADS, X_HDIM)
    return xk, xv


def kernel(x_prompt, x_sample, cache_fox_k, cache_fox_v, cache_fox_logf, cache_sb_k, cache_sb_v,
           cache_nsa_kcmp, cache_nsa_vcmp, cache_nsa_ksel, cache_nsa_vsel, state_nsa_kwin, state_nsa_vwin,
           cache_diff_k, cache_diff_v, cache_mem_k, cache_mem_v, state_ffn_conv, page_table, mem_prompt,
           norm_mix, norm_mem, norm_ffn, norm_final, fox_w_in, fox_b_f, fox_w_out, sb_w_in, sb_w_out,
           nsa_w_in, nsa_cmp_pe, nsa_cmp_w1, nsa_cmp_b1, nsa_cmp_w2, nsa_w_out,
           diff_w_in, diff_lq1, diff_lk1, diff_lq2, diff_lk2, diff_subln, diff_w_out,
           mem_norm, x_w_q, x_w_kv, x_w_o, ffn_w_up, ffn_conv_w, ffn_conv_b, ffn_w_down):
    p = {
        "norm_mix": norm_mix, "norm_mem": norm_mem, "norm_ffn": norm_ffn, "norm_final": norm_final,
        "fox_w_in": fox_w_in, "fox_b_f": fox_b_f, "fox_w_out": fox_w_out,
        "sb_w_in": sb_w_in, "sb_w_out": sb_w_out,
        "nsa_w_in": nsa_w_in, "nsa_cmp_pe": nsa_cmp_pe, "nsa_cmp_w1": nsa_cmp_w1, "nsa_cmp_b1": nsa_cmp_b1,
        "nsa_cmp_w2": nsa_cmp_w2, "nsa_w_out": nsa_w_out,
        "diff_w_in": diff_w_in, "diff_lq1": diff_lq1, "diff_lk1": diff_lk1, "diff_lq2": diff_lq2,
        "diff_lk2": diff_lk2, "diff_subln": diff_subln, "diff_w_out": diff_w_out,
        "x_w_q": x_w_q, "x_w_o": x_w_o,
        "ffn_w_up": ffn_w_up, "ffn_conv_w": ffn_conv_w, "ffn_conv_b": ffn_conv_b, "ffn_w_down": ffn_w_down,
    }
    w = prepare_weights(p)
    mem_k_p, mem_v_p = memory_kv(mem_prompt, mem_norm, x_w_kv)
    y_prompt, sp = prompt_trunk(x_prompt, mem_k_p, mem_v_p, p, w)
    cache = {
        "fox_k": cache_fox_k, "fox_v": cache_fox_v, "fox_logf": cache_fox_logf,
        "sb_k": cache_sb_k, "sb_v": cache_sb_v,
        "nsa_kcmp": cache_nsa_kcmp, "nsa_vcmp": cache_nsa_vcmp, "nsa_ksel": cache_nsa_ksel,
        "nsa_vsel": cache_nsa_vsel, "nsa_kwin": state_nsa_kwin, "nsa_vwin": state_nsa_vwin,
        "diff_k": cache_diff_k, "diff_v": cache_diff_v,
    }
    past_len = page_table.shape[1] * cache_fox_k.shape[1]
    y_sample, ss = sample_trunk(x_sample, past_len, cache_mem_k, cache_mem_v, state_ffn_conv, cache, page_table,
                                p, w)
    return (y_prompt, y_sample,
            sp["fox_k"], sp["fox_v"], sp["fox_logf"], sp["sb_k"], sp["sb_v"],
            sp["nsa_kcmp"], sp["nsa_vcmp"], sp["nsa_ksel"], sp["nsa_vsel"], sp["nsa_kwin"], sp["nsa_vwin"],
            sp["diff_k"], sp["diff_v"], mem_k_p, mem_v_p, sp["ffn_conv"],
            ss["fox_k"], ss["fox_v"], ss["fox_logf"], ss["sb_k"], ss["sb_v"],
            ss["nsa_kcmp"], ss["nsa_vcmp"], ss["nsa_ksel"], ss["nsa_vsel"], ss["nsa_kwin"], ss["nsa_vwin"],
            ss["diff_k"], ss["diff_v"], ss["ffn_conv"])
```

```python
import functools
import math

import jax
import jax.numpy as jnp
from jax import lax
from jax.experimental import pallas as pl
from jax.experimental.pallas import tpu as pltpu

F32 = jnp.float32
BF16 = jnp.bfloat16

V7X_VMEM_BYTES = 64 * 1024 * 1024
LANES = 128
SUBLANES = 8
VMEM_LIMIT = V7X_VMEM_BYTES - 8 * 1024 * 1024

NORM_EPS = 1e-6
MASK_VALUE = -1e30
ROPE_THETA = 10000.0
FORGET_HEADS = 16
HEAD_DIM = 64
GROUP = 4
GLANES = GROUP * HEAD_DIM


def _cparams(n_axes):
    return pltpu.CompilerParams(dimension_semantics=("arbitrary",) * n_axes, vmem_limit_bytes=VMEM_LIMIT)


def _nt_dot(a, b):
    return lax.dot_general(a, b, (((1,), (1,)), ((), ())), preferred_element_type=F32)


def _dot(a, b):
    return jnp.dot(a, b, preferred_element_type=F32)


def _split3(x):
    hi = x.astype(BF16)
    r1 = x - hi.astype(F32)
    mid = r1.astype(BF16)
    lo = (r1 - mid.astype(F32)).astype(BF16)
    return hi, mid, lo


def _split2(x):
    hi = x.astype(BF16)
    lo = (x - hi.astype(F32)).astype(BF16)
    return hi, lo


def _dot_exact_rhs01(x, m01):
    hi, mid, lo = _split3(x)
    return _dot(hi, m01) + _dot(mid, m01) + _dot(lo, m01)


def _dot_exact_lhs01(m01, x):
    hi, mid, lo = _split3(x)
    return _dot(m01, hi) + _dot(m01, mid) + _dot(m01, lo)


def _softplus(z):
    return jnp.maximum(z, 0.0) + jnp.log(1.0 + jnp.exp(-jnp.abs(z)))


def _log_sigmoid(z):
    return jnp.minimum(z, 0.0) - jnp.log1p(jnp.exp(-jnp.abs(z)))


def _mm_kernel(*refs, prologue, has_res, stage):
    it = iter(refs)
    x_ref = next(it)
    if prologue == "norm":
        g_ref = next(it)
    elif prologue == "add_gelu":
        x2_ref, pb_ref = next(it), next(it)
    w_ref = next(it)
    r_ref = next(it) if has_res else None
    o_ref = next(it)
    xs_ref = next(it) if stage else None

    if stage:
        @pl.when(pl.program_id(1) == 0)
        def _():
            x = x_ref[...].astype(F32)
            if prologue == "norm":
                x = x * lax.rsqrt(jnp.mean(x * x, axis=-1, keepdims=True) + NORM_EPS)
                x = x * g_ref[...]
            elif prologue == "add_gelu":
                x = jax.nn.gelu(x + x2_ref[...] + pb_ref[...])
            xs_ref[...] = x.astype(BF16)
        xb = xs_ref[...]
    else:
        xb = x_ref[...]
    y = _dot(xb, w_ref[...])
    if has_res:
        y = y + r_ref[...]
    o_ref[...] = y.astype(o_ref.dtype)


def _row_tile(n, cap):
    t = min(n, cap)
    while n % t:
        t //= 2
    return t


def matmul(x, w, *, gain=None, add=None, pre_bias=None, res=None, out_dtype=F32, tm_cap=1024, tn_cap=512):
    n, k = x.shape
    e = w.shape[1]
    tm = _row_tile(n, tm_cap)
    tn = _row_tile(e, tn_cap)
    assert n % tm == 0 and e % tn == 0 and tm % SUBLANES == 0 and tn % LANES == 0, (n, e, tm, tn)
    prologue = "norm" if gain is not None else ("add_gelu" if add is not None else None)
    stage = prologue is not None or x.dtype != BF16
    ins = [x]
    specs = [pl.BlockSpec((tm, k), lambda i, j: (i, 0))]
    if prologue == "norm":
        ins.append(gain.reshape(1, k).astype(F32))
        specs.append(pl.BlockSpec((1, k), lambda i, j: (0, 0)))
    elif prologue == "add_gelu":
        ins += [add, pre_bias.reshape(1, k).astype(F32)]
        specs += [pl.BlockSpec((tm, k), lambda i, j: (i, 0)), pl.BlockSpec((1, k), lambda i, j: (0, 0))]
    ins.append(w)
    specs.append(pl.BlockSpec((k, tn), lambda i, j: (0, j)))
    if res is not None:
        ins.append(res)
        specs.append(pl.BlockSpec((tm, tn), lambda i, j: (i, j)))
    kern = functools.partial(_mm_kernel, prologue=prologue, has_res=res is not None, stage=stage)
    return pl.pallas_call(
        kern,
        out_shape=jax.ShapeDtypeStruct((n, e), out_dtype),
        grid=(n // tm, e // tn),
        in_specs=specs,
        out_specs=pl.BlockSpec((tm, tn), lambda i, j: (i, j)),
        scratch_shapes=[pltpu.VMEM((tm, k), BF16)] if stage else [],
        compiler_params=_cparams(2),
        name="matmul",
    )(*ins)


def _rmsnorm_kernel(x_ref, g_ref, o_ref):
    x = x_ref[...]
    y = x * lax.rsqrt(jnp.mean(x * x, axis=-1, keepdims=True) + NORM_EPS) * g_ref[...]
    o_ref[...] = y.astype(o_ref.dtype)


def rmsnorm_rows(x, gain, out_dtype=F32):
    n, d = x.shape
    tm = _row_tile(n, 1024)
    return pl.pallas_call(
        _rmsnorm_kernel,
        out_shape=jax.ShapeDtypeStruct((n, d), out_dtype),
        grid=(n // tm,),
        in_specs=[pl.BlockSpec((tm, d), lambda i: (i, 0)), pl.BlockSpec((1, d), lambda i: (0, 0))],
        out_specs=pl.BlockSpec((tm, d), lambda i: (i, 0)),
        compiler_params=_cparams(1),
        name="rmsnorm",
    )(x, gain.reshape(1, d))


def _conv_gate_kernel(uv_ref, ug_ref, hv_ref, hg_ref, sv_ref, sg_ref, wv_ref, wg_ref, o_ref):
    first = pl.program_id(1) == 0

    def conv(u_ref, h_ref, s_ref, w_ref):
        u = u_ref[0]
        prev = jnp.where(first, s_ref[0], h_ref[0])
        p2 = prev[6:7, :]
        p1 = prev[7:8, :]
        row = lax.broadcasted_iota(jnp.int32, u.shape, 0)
        um1 = jnp.where(row == 0, p1, pltpu.roll(u, 1, 0))
        um2 = jnp.where(row == 0, p2, jnp.where(row == 1, p1, pltpu.roll(u, 2, 0)))
        w = w_ref[...]
        return w[3:4, :] + w[0:1, :] * um2 + w[1:2, :] * um1 + w[2:3, :] * u

    val = conv(uv_ref, hv_ref, sv_ref, wv_ref)
    gate = conv(ug_ref, hg_ref, sg_ref, wg_ref)
    o_ref[0] = (gate * jax.nn.sigmoid(gate) * val).astype(o_ref.dtype)


def conv_gate(u, state8, wpack):
    b, t, f2 = u.shape
    f = f2 // 2
    tf = f // 2
    assert tf % LANES == 0
    nf = f // tf
    tt = _row_tile(t, 512)
    hb = tt // SUBLANES

    def halo(i):
        return jnp.maximum(i * hb - 1, 0)

    return pl.pallas_call(
        _conv_gate_kernel,
        out_shape=jax.ShapeDtypeStruct((b, t, f), BF16),
        grid=(b, t // tt, nf),
        in_specs=[
            pl.BlockSpec((1, tt, tf), lambda bi, i, j: (bi, i, j)),
            pl.BlockSpec((1, tt, tf), lambda bi, i, j: (bi, i, nf + j)),
            pl.BlockSpec((1, SUBLANES, tf), lambda bi, i, j: (bi, halo(i), j)),
            pl.BlockSpec((1, SUBLANES, tf), lambda bi, i, j: (bi, halo(i), nf + j)),
            pl.BlockSpec((1, SUBLANES, tf), lambda bi, i, j: (bi, 0, j)),
            pl.BlockSpec((1, SUBLANES, tf), lambda bi, i, j: (bi, 0, nf + j)),
            pl.BlockSpec((SUBLANES, tf), lambda bi, i, j: (0, j)),
            pl.BlockSpec((SUBLANES, tf), lambda bi, i, j: (0, nf + j)),
        ],
        out_specs=pl.BlockSpec((1, tt, tf), lambda bi, i, j: (bi, i, j)),
        compiler_params=_cparams(3),
        name="conv_gate",
    )(u, u, u, u, state8, state8, wpack, wpack)


def _cross_kernel(x_ref, g_ref, wq_ref, mk_ref, mv_ref, wo_ref, o_ref, *, heads, hdim):
    x = x_ref[0]
    h = x * lax.rsqrt(jnp.mean(x * x, axis=-1, keepdims=True) + NORM_EPS) * g_ref[...]
    q = _dot(h.astype(BF16), wq_ref[...]) * (hdim ** -0.5)
    outs = []
    for hh in range(heads):
        sl = slice(hh * hdim, (hh + 1) * hdim)
        s = _nt_dot(q[:, sl].astype(BF16), mk_ref[0, :, sl].astype(BF16))
        p = jnp.exp(s - jnp.max(s, axis=-1, keepdims=True))
        p = p / jnp.sum(p, axis=-1, keepdims=True)
        outs.append(_dot(p.astype(BF16), mv_ref[0, :, sl].astype(BF16)))
    o = jnp.concatenate(outs, axis=-1)
    o_ref[0] = x + _dot(o.astype(BF16), wo_ref[...])


def cross_block(x, gain, wq, mk, mv, wo, *, heads=4, hdim=128):
    b, t, d = x.shape
    m = mk.shape[1]
    e = heads * hdim
    tt = _row_tile(t, 512)
    return pl.pallas_call(
        functools.partial(_cross_kernel, heads=heads, hdim=hdim),
        out_shape=jax.ShapeDtypeStruct((b, t, d), F32),
        grid=(b, t // tt),
        in_specs=[
            pl.BlockSpec((1, tt, d), lambda bi, i: (bi, i, 0)),
            pl.BlockSpec((1, d), lambda bi, i: (0, 0)),
            pl.BlockSpec((d, e), lambda bi, i: (0, 0)),
            pl.BlockSpec((1, m, e), lambda bi, i: (bi, 0, 0)),
            pl.BlockSpec((1, m, e), lambda bi, i: (bi, 0, 0)),
            pl.BlockSpec((e, d), lambda bi, i: (0, 0)),
        ],
        out_specs=pl.BlockSpec((1, tt, d), lambda bi, i: (bi, i, 0)),
        compiler_params=_cparams(2),
        name="cross_block",
    )(x, gain.reshape(1, d), wq, mk, mv, wo)


def rope_tables(pos):
    half = HEAD_DIM // 2
    inv = ROPE_THETA ** (-jnp.arange(half, dtype=F32) / half)
    ang = pos.astype(F32)[:, None] * inv[None, :]
    cos = jnp.cos(ang)
    sin = jnp.sin(ang)
    cos_t = jnp.tile(jnp.concatenate([cos, cos], axis=-1), (1, GROUP))
    sin_t = jnp.tile(jnp.concatenate([-sin, sin], axis=-1), (1, GROUP))
    return cos_t, sin_t


def _rope_apply(x, cos_t, sin_t):
    half = HEAD_DIM // 2
    lane = lax.broadcasted_iota(jnp.int32, x.shape, 1)
    first = (lane % HEAD_DIM) < half
    n = x.shape[1]
    swapped = jnp.where(first, pltpu.roll(x, n - half, 1), pltpu.roll(x, half, 1))
    return x * cos_t + swapped * sin_t


def _rope_kernel(x_ref, c_ref, s_ref, o_ref):
    o_ref[0] = _rope_apply(x_ref[0], c_ref[...], s_ref[...])


def rope_cols(x, col0, width, cos_t, sin_t):
    b, t, _ = x.shape
    assert col0 % GLANES == 0 and width % GLANES == 0
    tt = _row_tile(t, 512)
    c0 = col0 // GLANES
    return pl.pallas_call(
        _rope_kernel,
        out_shape=jax.ShapeDtypeStruct((b, t, width), F32),
        grid=(b, t // tt, width // GLANES),
        in_specs=[
            pl.BlockSpec((1, tt, GLANES), lambda bi, i, j: (bi, i, c0 + j)),
            pl.BlockSpec((tt, GLANES), lambda bi, i, j: (i, 0)),
            pl.BlockSpec((tt, GLANES), lambda bi, i, j: (i, 0)),
        ],
        out_specs=pl.BlockSpec((1, tt, GLANES), lambda bi, i, j: (bi, i, j)),
        compiler_params=_cparams(3),
        name="rope",
    )(x, cos_t, sin_t)


def _logf_cumsum_kernel(g_ref, b_ref, lf_ref, c_ref, cp_ref, *, chunk):
    t = g_ref.shape[1]
    row = lax.broadcasted_iota(jnp.int32, (chunk, chunk), 0)
    col = lax.broadcasted_iota(jnp.int32, (chunk, chunk), 1)
    tri = jnp.where(col <= row, 1.0, 0.0).astype(BF16)
    head_lane = lax.broadcasted_iota(jnp.int32, (chunk, g_ref.shape[2]), 1) < FORGET_HEADS

    def body(i, carry):
        sl = pl.ds(pl.multiple_of(i * chunk, chunk), chunk)
        lf = _log_sigmoid(g_ref[0, sl, :] + b_ref[...])
        lf_ref[0, sl, :] = lf
        c = _dot_exact_lhs01(tri, lf) + carry
        c_ref[0, sl, :] = c
        hi, mid, lo = _split3(jnp.where(head_lane, c, 0.0))
        placed = (hi.astype(F32) + pltpu.roll(mid.astype(F32), FORGET_HEADS, 1)
                  + pltpu.roll(lo.astype(F32), 2 * FORGET_HEADS, 1))
        cp_ref[0, sl, :] = placed.astype(BF16)
        return c[chunk - 1:chunk, :]

    lax.fori_loop(0, t // chunk, body, jnp.zeros((1, g_ref.shape[2]), F32))


def logf_cumsum(gate, bias):
    b, t, n = gate.shape
    chunk = _row_tile(t, 256)
    spec = pl.BlockSpec((1, t, n), lambda bi: (bi, 0, 0))
    return pl.pallas_call(
        functools.partial(_logf_cumsum_kernel, chunk=chunk),
        out_shape=(jax.ShapeDtypeStruct((b, t, n), F32), jax.ShapeDtypeStruct((b, t, n), F32),
                   jax.ShapeDtypeStruct((b, t, n), BF16)),
        grid=(b,),
        in_specs=[spec, pl.BlockSpec((1, n), lambda bi: (0, 0))],
        out_specs=(spec, spec, spec),
        compiler_params=_cparams(1),
        name="logf_cumsum",
    )(gate, bias.reshape(1, n))


CMP_STRIDE = 16
CMP_BLK = 32
SEL_BLK = 64
N_SELECT = 16
WINDOW = 512
SEL_FORCE = 1e4


def compress_rows(rows, pe, w1, b1, w2):
    b, g, tp, d = rows.shape
    n = tp // CMP_STRIDE
    hid = w1.shape[-1]
    a = rows.reshape(b * g * n, CMP_STRIDE * d)
    w1f = w1.reshape(CMP_BLK * d, hid)
    w1cat = jnp.concatenate([w1f[:CMP_STRIDE * d], w1f[CMP_STRIDE * d:]], axis=1).astype(BF16)
    h = matmul(a, w1cat).reshape(b, g, n, 2 * hid)
    h_first = h[..., :hid]
    h_second = jnp.concatenate([h[:, :, 1:, hid:], jnp.zeros((b, g, 1, hid), F32)], axis=2)
    bias = b1 + jnp.einsum("k,kh->h", pe.reshape(-1), w1f, precision=lax.Precision.HIGHEST)
    w2p = jnp.pad(w2, ((0, 0), (0, LANES - d))).astype(BF16)
    out = matmul(h_first.reshape(b * g * n, hid), w2p, add=h_second.reshape(b * g * n, hid), pre_bias=bias)
    return out[:, :d].reshape(b, g, n, d)


KEY_TILES = 2


def _masked_q_t(q_t):
    row = lax.broadcasted_iota(jnp.int32, q_t.shape, 0) // HEAD_DIM
    return jnp.concatenate([jnp.where(row == r, q_t, 0.0).astype(BF16) for r in range(GROUP)], axis=1)


def _causal_mask_t(tk, tq, k0, q0, strict):
    key = lax.broadcasted_iota(jnp.int32, (tk, GROUP * tq), 0) + k0
    qry = lax.broadcasted_iota(jnp.int32, (tk, GROUP * tq), 1) % tq + q0
    return key < qry if strict else key <= qry


def _softmax_step_t(s, mask, m, l):
    if mask is not None:
        s = jnp.where(mask, s, MASK_VALUE)
    m_new = jnp.maximum(m, jnp.max(s, axis=0, keepdims=True))
    alpha = jnp.exp(m - m_new)
    p = jnp.exp(s - m_new)
    if mask is not None:
        p = jnp.where(mask, p, 0.0)
    return p, alpha, m_new, alpha * l + jnp.sum(p, axis=0, keepdims=True)


def _pflash_t_kernel(*refs, kind, tq, tk, vrows, out_scale):
    if kind == "fox":
        q_ref, k_ref, vt_ref, cp_ref, o_ref, acc_ref = refs
    else:
        q_ref, k_ref, vt_ref, lam_ref, sub_ref, o_ref, acc_ref = refs
    g = pl.program_id(1)
    qi = pl.program_id(2)
    n = GROUP * tq
    q4t = _masked_q_t((q_ref[0] * (HEAD_DIM ** -0.5)).T)
    if kind == "fox":
        piece_row = lax.broadcasted_iota(jnp.int32, (LANES, n), 0)
        head = lax.broadcasted_iota(jnp.int32, (LANES, n), 1) // tq + g * GROUP
        cind = jnp.where((piece_row < 3 * FORGET_HEADS) & (piece_row % FORGET_HEADS == head), -1.0, 0.0).astype(BF16)
    acc_ref[...] = jnp.zeros(acc_ref.shape, F32)

    def step(kt, masked, m, l):
        spans, scores, masks = [], [], []
        for u in range(KEY_TILES):
            k0 = (kt * KEY_TILES + u) * tk
            ks = pl.ds(pl.multiple_of(k0, tk), tk)
            s = _dot(k_ref[0, ks, :].astype(BF16), q4t)
            if kind == "fox":
                s = s + _dot(cp_ref[0, ks, :], cind)
            mask = _causal_mask_t(tk, tq, k0, qi * tq, False) if masked else None
            if masked:
                s = jnp.where(mask, s, MASK_VALUE)
            spans.append(ks)
            scores.append(s)
            masks.append(mask)
        m_new = m
        for s in scores:
            m_new = jnp.maximum(m_new, jnp.max(s, axis=0, keepdims=True))
        alpha = jnp.exp(m - m_new)
        l = alpha * l
        probs = []
        for s, mask in zip(scores, masks):
            p = jnp.exp(s - m_new)
            if masked:
                p = jnp.where(mask, p, 0.0)
            l = l + jnp.sum(p, axis=0, keepdims=True)
            probs.append(p.astype(BF16))
        for r in range(GROUP):
            v0 = (r * HEAD_DIM // vrows) * vrows
            cols = slice(r * tq, (r + 1) * tq)
            pv = _dot(vt_ref[0, v0:v0 + vrows, spans[0]].astype(BF16), probs[0][:, cols])
            for u in range(1, KEY_TILES):
                pv = pv + _dot(vt_ref[0, v0:v0 + vrows, spans[u]].astype(BF16), probs[u][:, cols])
            acc_ref[r] = alpha[:, cols] * acc_ref[r] + pv
        return m_new, l

    init = (jnp.full((1, n), MASK_VALUE, F32), jnp.zeros((1, n), F32))
    full_steps = qi // KEY_TILES
    m, l = lax.fori_loop(0, full_steps, lambda kt, c: step(kt, False, *c), init)
    m, l = step(full_steps, True, m, l)

    inv = 1.0 / jnp.maximum(l, 1e-30)
    blk = [acc_ref[r] * inv[:, r * tq:(r + 1) * tq] for r in range(GROUP)]
    if kind == "fox":
        o_t = jnp.concatenate(blk, axis=0)
    else:
        lam = lam_ref[0, 0]
        halves = []
        for hh in range(2):
            d = blk[2 * hh] - lam * blk[2 * hh + 1]
            y = d * lax.rsqrt(jnp.mean(d * d, axis=0, keepdims=True) + NORM_EPS) * sub_ref[...]
            halves.append(y * out_scale)
        o_t = jnp.concatenate(halves, axis=0)
    o_ref[0] = o_t.T.astype(o_ref.dtype)


def prompt_flash(kind, qa, qcol, ka, kcol, v_t, *, cp=None, lam=None, subln=None, out_scale=1.0, tile=256):
    b, t, _ = qa.shape
    tq = tk = _row_tile(t, tile)
    assert (t // tk) % KEY_TILES == 0
    vrows = HEAD_DIM if kind == "fox" else LANES
    ins = [qa, ka, v_t]
    specs = [
        pl.BlockSpec((1, tq, GLANES), lambda bi, g, i: (bi, i, qcol + g)),
        pl.BlockSpec((1, t, GLANES), lambda bi, g, i: (bi, 0, kcol + g)),
        pl.BlockSpec((1, GLANES, t), lambda bi, g, i: (bi, g, 0)),
    ]
    if kind == "fox":
        ins.append(cp)
        specs.append(pl.BlockSpec((1, t, LANES), lambda bi, g, i: (bi, 0, 0)))
    else:
        ins += [lam.reshape(1, 1), subln.reshape(LANES, 1)]
        specs += [pl.BlockSpec(memory_space=pltpu.SMEM), pl.BlockSpec((LANES, 1), lambda bi, g, i: (0, 0))]
    return pl.pallas_call(
        functools.partial(_pflash_t_kernel, kind=kind, tq=tq, tk=tk, vrows=vrows, out_scale=out_scale),
        out_shape=jax.ShapeDtypeStruct((b, t, GROUP * GLANES), BF16),
        grid=(b, GROUP, t // tq),
        in_specs=specs,
        out_specs=pl.BlockSpec((1, tq, GLANES), lambda bi, g, i: (bi, i, g)),
        scratch_shapes=[pltpu.VMEM((GROUP, vrows, tq), F32)],
        compiler_params=_cparams(3),
        name="prompt_flash_" + kind,
    )(*ins)


def _psb_t_kernel(q_ref, k_ref, vt_ref, o_ref, acc_ref, *, tq, tk):
    qi = pl.program_id(2)
    n = GROUP * tq
    q4t = _masked_q_t((q_ref[0] * (HEAD_DIM ** -0.5)).T)
    acc_ref[...] = jnp.zeros(acc_ref.shape, F32)
    row = lax.broadcasted_iota(jnp.int32, (tk, tk), 0)
    col = lax.broadcasted_iota(jnp.int32, (tk, tk), 1)
    later = jnp.where(col > row, 1.0, 0.0).astype(BF16)

    def step(kt, masked, carry):
        weights = []
        for u in reversed(range(KEY_TILES)):
            k0 = (kt * KEY_TILES + u) * tk
            ks = pl.ds(pl.multiple_of(k0, tk), tk)
            z = _dot(k_ref[0, ks, :].astype(BF16), q4t)
            sp = _softplus(z)
            log_rest = -sp
            mask = _causal_mask_t(tk, tq, k0, qi * tq, True) if masked else None
            if masked:
                log_rest = jnp.where(mask, log_rest, 0.0)
            hi, lo = _split2(log_rest)
            after = carry + _dot(later, hi) + _dot(later, lo)
            a = jnp.exp(z - sp + after)
            if masked:
                a = jnp.where(mask, a, 0.0)
            weights.append((ks, a.astype(BF16)))
            carry = carry + jnp.sum(log_rest, axis=0, keepdims=True)
        for r in range(GROUP):
            cols = slice(r * tq, (r + 1) * tq)
            rows = slice(r * HEAD_DIM, (r + 1) * HEAD_DIM)
            pv = _dot(vt_ref[0, rows, weights[0][0]].astype(BF16), weights[0][1][:, cols])
            for ks, ab in weights[1:]:
                pv = pv + _dot(vt_ref[0, rows, ks].astype(BF16), ab[:, cols])
            acc_ref[r] += pv
        return carry

    full_steps = qi // KEY_TILES
    carry = step(full_steps, True, jnp.zeros((1, n), F32))
    lax.fori_loop(0, full_steps, lambda i, c: step(full_steps - 1 - i, False, c), carry)
    o_ref[0] = jnp.concatenate([acc_ref[r] for r in range(GROUP)], axis=0).T.astype(o_ref.dtype)


def prompt_sb(proj, v_t, *, tile=256):
    b, t, _ = proj.shape
    tq = tk = _row_tile(t, tile)
    assert (t // tk) % KEY_TILES == 0
    return pl.pallas_call(
        functools.partial(_psb_t_kernel, tq=tq, tk=tk),
        out_shape=jax.ShapeDtypeStruct((b, t, GROUP * GLANES), BF16),
        grid=(b, GROUP, t // tq),
        in_specs=[
            pl.BlockSpec((1, tq, GLANES), lambda bi, g, i: (bi, i, g)),
            pl.BlockSpec((1, t, GLANES), lambda bi, g, i: (bi, 0, GROUP + g)),
            pl.BlockSpec((1, GLANES, t), lambda bi, g, i: (bi, g, 0)),
        ],
        out_specs=pl.BlockSpec((1, tq, GLANES), lambda bi, g, i: (bi, i, g)),
        scratch_shapes=[pltpu.VMEM((GROUP, HEAD_DIM, tq), F32)],
        compiler_params=_cparams(3),
        name="prompt_sb",
    )(proj, proj, v_t)


def _group_q_t(q):
    q_t = q.T
    return jnp.concatenate([q_t[r * HEAD_DIM:(r + 1) * HEAD_DIM, :] for r in range(GROUP)], axis=1).astype(BF16)


def _select_blocks_t(p_sum_t, q_pos, n_sel, n_top):
    nc, t = p_sum_t.shape
    nb = -(-n_sel // LANES) * LANES
    j_idx = lax.broadcasted_iota(jnp.int32, (nb, nc), 0)
    c_idx = lax.broadcasted_iota(jnp.int32, (nb, nc), 1) * CMP_STRIDE
    overlap = (c_idx < j_idx * SEL_BLK + SEL_BLK) & (c_idx + CMP_BLK > j_idx * SEL_BLK) & (j_idx < n_sel)
    imp = _dot_exact_lhs01(jnp.where(overlap, 1.0, 0.0).astype(BF16), p_sum_t)
    rows = -(-n_sel // SUBLANES) * SUBLANES
    imp = imp[:rows]
    blk = lax.broadcasted_iota(jnp.int32, (rows, t), 0)
    cur = q_pos // SEL_BLK
    forced = (blk == 0) | (blk == cur) | (blk == cur - 1)
    valid = blk * SEL_BLK <= q_pos
    score = jnp.where(forced, SEL_FORCE, jnp.where(valid, imp, -1.0))
    score = jnp.where(blk < n_sel, score, -3e38)
    rank = jnp.zeros(score.shape, F32)
    for jp in range(n_sel):
        other = score[jp:jp + 1, :]
        ge = jnp.where(other >= score, 1.0, 0.0)
        gt = jnp.where(other > score, 1.0, 0.0)
        rank = rank + jnp.where(blk > jp, ge, gt)
    sel = jnp.where(rank < n_top, 1.0, 0.0)
    if rows < nb:
        sel = jnp.concatenate([sel, jnp.zeros((nb - rows, t), F32)], axis=0)
    return sel


def _pnsa_t_kernel(qc_ref, qr_ref, gate_ref, kc_ref, vct_ref, ks_ref, vst_ref, kw_ref, vwt_ref, o_ref, acc_ref,
                   *, tq, tk, n_sel):
    qi = pl.program_id(2)
    scale = HEAD_DIM ** -0.5
    q0 = qi * tq
    n = GROUP * tq
    q_pos = lax.broadcasted_iota(jnp.int32, (1, n), 1) % tq + q0

    qc = _group_q_t(qc_ref[0] * scale)
    nc = kc_ref.shape[2]
    s = _dot(kc_ref[0, 0].astype(BF16), qc)
    cmp_end = lax.broadcasted_iota(jnp.int32, (nc, n), 0) * CMP_STRIDE + (CMP_BLK - 1)
    mask = cmp_end <= q_pos
    s = jnp.where(mask, s, MASK_VALUE)
    p = jnp.where(mask, jnp.exp(s - jnp.max(s, axis=0, keepdims=True)), 0.0)
    p = p / jnp.maximum(jnp.sum(p, axis=0, keepdims=True), 1e-30)
    o_c = _dot(vct_ref[0, 0].astype(BF16), p.astype(BF16))
    p_sum = p[:, 0:tq] + p[:, tq:2 * tq] + p[:, 2 * tq:3 * tq] + p[:, 3 * tq:4 * tq]
    sel = _select_blocks_t(p_sum, q_pos[:, 0:tq], n_sel, min(N_SELECT, n_sel)).astype(BF16)
    nb = sel.shape[0]

    qr = _group_q_t(qr_ref[0] * scale)
    init = (jnp.full((1, n), MASK_VALUE, F32), jnp.zeros((1, n), F32))
    acc_ref[...] = jnp.zeros(acc_ref.shape, F32)

    def sel_step(kt, c):
        m, l = c
        spans, scores, masks = [], [], []
        for u in range(KEY_TILES):
            k0 = (kt * KEY_TILES + u) * tk
            ks = pl.ds(pl.multiple_of(k0, tk), tk)
            s = _dot(ks_ref[0, 0, ks, :].astype(BF16), qr)
            key = lax.broadcasted_iota(jnp.int32, (tk, nb), 0) + k0
            blk = lax.broadcasted_iota(jnp.int32, (tk, nb), 1)
            chosen = _dot(jnp.where(blk == key // SEL_BLK, 1.0, 0.0).astype(BF16), sel)
            kpos = lax.broadcasted_iota(jnp.int32, (tk, tq), 0) + k0
            m1 = jnp.where(kpos <= q_pos[:, 0:tq], chosen, 0.0)
            mask = jnp.concatenate([m1] * GROUP, axis=1) > 0.5
            spans.append(ks)
            scores.append(jnp.where(mask, s, MASK_VALUE))
            masks.append(mask)
        m_new = m
        for s in scores:
            m_new = jnp.maximum(m_new, jnp.max(s, axis=0, keepdims=True))
        alpha = jnp.exp(m - m_new)
        l = alpha * l
        pv = jnp.zeros(acc_ref.shape, F32)
        for ks, s, mask in zip(spans, scores, masks):
            p = jnp.where(mask, jnp.exp(s - m_new), 0.0)
            l = l + jnp.sum(p, axis=0, keepdims=True)
            pv = pv + _dot(vst_ref[0, :, ks].astype(BF16), p.astype(BF16))
        acc_ref[...] = alpha * acc_ref[...] + pv
        return m_new, l

    m, l = lax.fori_loop(0, qi // KEY_TILES + 1, sel_step, init)
    o_s = acc_ref[...] / jnp.maximum(l, 1e-30)
    acc_ref[...] = jnp.zeros(acc_ref.shape, F32)

    def win_tile(kt, c):
        m, l = c
        ks = pl.ds(pl.multiple_of(kt * tk, tk), tk)
        s = _dot(kw_ref[0, 0, ks, :].astype(BF16), qr)
        dist = q_pos - (lax.broadcasted_iota(jnp.int32, (tk, n), 0) + kt * tk)
        m4 = (dist >= 0) & (dist < WINDOW)
        p, alpha, m, l = _softmax_step_t(s, m4, m, l)
        acc_ref[...] = alpha * acc_ref[...] + _dot(vwt_ref[0, :, ks].astype(BF16), p.astype(BF16))
        return m, l

    kt_lo = jnp.maximum(q0 - (WINDOW - 1), 0) // tk
    m, l = lax.fori_loop(kt_lo, qi + 1, win_tile, init)
    o_w = acc_ref[...] / jnp.maximum(l, 1e-30)

    gates = jax.nn.sigmoid(gate_ref[0, 0])
    outs = []
    for r in range(GROUP):
        cols = slice(r * tq, (r + 1) * tq)
        outs.append(gates[r:r + 1, :] * o_c[:, cols] + gates[GROUP + r:GROUP + r + 1, :] * o_s[:, cols]
                    + gates[2 * GROUP + r:2 * GROUP + r + 1, :] * o_w[:, cols])
    o_ref[0] = jnp.concatenate(outs, axis=0).T.astype(o_ref.dtype)


def prompt_nsa(proj, q_rot, gate_t, kcmp, vcmp_t, ksel, vsel_t, kwin, vwin_t, *, tile=256):
    b, t, _ = proj.shape
    tq = tk = _row_tile(t, tile)
    assert (t // tk) % KEY_TILES == 0
    nc = kcmp.shape[2]
    n_sel = t // SEL_BLK
    assert t % SEL_BLK == 0 and nc % LANES == 0
    k_spec = pl.BlockSpec((1, 1, t, HEAD_DIM), lambda bi, g, i: (bi, g, 0, 0))
    vt_spec = pl.BlockSpec((1, HEAD_DIM, t), lambda bi, g, i: (bi, g, 0))
    return pl.pallas_call(
        functools.partial(_pnsa_t_kernel, tq=tq, tk=tk, n_sel=n_sel),
        out_shape=jax.ShapeDtypeStruct((b, t, GROUP * GLANES), BF16),
        grid=(b, GROUP, t // tq),
        in_specs=[
            pl.BlockSpec((1, tq, GLANES), lambda bi, g, i: (bi, i, g)),
            pl.BlockSpec((1, tq, GLANES), lambda bi, g, i: (bi, i, g)),
            pl.BlockSpec((1, 1, 4 * GROUP, tq), lambda bi, g, i: (bi, g, 0, i)),
            pl.BlockSpec((1, 1, nc, HEAD_DIM), lambda bi, g, i: (bi, g, 0, 0)),
            pl.BlockSpec((1, 1, HEAD_DIM, nc), lambda bi, g, i: (bi, g, 0, 0)),
            k_spec, vt_spec, k_spec, vt_spec,
        ],
        out_specs=pl.BlockSpec((1, tq, GLANES), lambda bi, g, i: (bi, i, g)),
        scratch_shapes=[pltpu.VMEM((HEAD_DIM, GROUP * tq), F32)],
        compiler_params=_cparams(3),
        name="prompt_nsa",
    )(proj, q_rot, gate_t, kcmp, vcmp_t, ksel, vsel_t, kwin, vwin_t)


NEW_PAD = 16
STEP_PAGES = 8


def _pad_rows(x, n):
    return jnp.pad(x, ((0, 0), (0, n - x.shape[1]), (0, 0)))


def _rows_repeat(x, reps):
    return jnp.concatenate([jnp.broadcast_to(x[i:i + 1], (reps, x.shape[1])) for i in range(x.shape[0])], axis=0)


def _strict_lower_ones(n):
    row = lax.broadcasted_iota(jnp.int32, (n, n), 0)
    col = lax.broadcasted_iota(jnp.int32, (n, n), 1)
    return jnp.where(row > col, 1.0, 0.0).astype(BF16)


def _sattn_native_kernel(*refs, kind, n_new, heads, v_heads, out_scale, step_pages, page, values_transposed):
    it = iter(refs)
    pt_ref = next(it)
    q_ref, kn_ref, vn_ref = (next(it) for _ in range(3))
    k_refs = [next(it) for _ in range(step_pages)]
    v_refs = [next(it) for _ in range(step_pages)]
    if kind == "fox":
        lf_refs = [next(it) for _ in range(step_pages)]
        nc_ref = next(it)
    if kind == "diff":
        lam_ref, sub_ref = next(it), next(it)
    o_ref = next(it)
    qs_ref = next(it)
    if kind == "sb":
        carry_ref, acc_ref = next(it), next(it)
    else:
        m_ref, l_ref, acc_ref = next(it), next(it), next(it)
    if kind == "fox":
        later_ref = next(it)
    del pt_ref
    p = pl.program_id(1)
    rows = n_new * heads
    vdim = acc_ref.shape[1]
    per_v = heads // v_heads

    def head_scores(key_of_head, keys_transposed):
        dot = _dot if keys_transposed else _nt_dot
        return jnp.concatenate([dot(qs_ref[h].astype(BF16), key_of_head(h)) for h in range(heads)], axis=0)

    def weighted_values(pr, c0, width, value_of_head, values_transposed):
        dot = _nt_dot if values_transposed else _dot
        outs = []
        for vh in range(v_heads):
            r0 = vh * per_v * n_new
            outs.append(dot(pr[r0:r0 + per_v * n_new, c0:c0 + width].astype(BF16), value_of_head(vh)))
        return jnp.concatenate(outs, axis=0)

    @pl.when(p == 0)
    def _():
        q = q_ref[0] * (HEAD_DIM ** -0.5)
        for h in range(heads):
            qs_ref[h] = q[:, h * HEAD_DIM:(h + 1) * HEAD_DIM]
        acc_ref[...] = jnp.zeros(acc_ref.shape, F32)
        kn = kn_ref[0].astype(BF16)
        vn = vn_ref[0].astype(BF16)
        s = head_scores(lambda h: kn[:, h * HEAD_DIM:(h + 1) * HEAD_DIM], False)
        t_idx = lax.broadcasted_iota(jnp.int32, (rows, NEW_PAD), 0) % n_new
        s_idx = lax.broadcasted_iota(jnp.int32, (rows, NEW_PAD), 1)
        new_value = lambda vh: vn[:, vh * vdim:(vh + 1) * vdim]
        if kind == "sb":
            mask = s_idx < t_idx
            sp = _softplus(s)
            log_rest = jnp.where(mask, -sp, 0.0)
            hi, lo = _split2(log_rest)
            tri = _strict_lower_ones(NEW_PAD)
            after = _dot(hi, tri) + _dot(lo, tri)
            a = jnp.where(mask, jnp.exp(s - sp + after), 0.0)
            acc_ref[...] = weighted_values(a, 0, NEW_PAD, new_value, False)
            carry_ref[...] = after[:, 0:1] + log_rest[:, 0:1]
        else:
            if kind == "fox":
                later_ref[...] = jnp.zeros(later_ref.shape, F32)
                s = s - _rows_repeat(nc_ref[0], n_new)
            mask = s_idx <= t_idx
            s = jnp.where(mask, s, MASK_VALUE)
            m_new = jnp.max(s, axis=1, keepdims=True)
            pr = jnp.where(mask, jnp.exp(s - m_new), 0.0)
            m_ref[...] = m_new
            l_ref[...] = jnp.sum(pr, axis=1, keepdims=True)
            acc_ref[...] = weighted_values(pr, 0, NEW_PAD, new_value, False)

    def page_keys(j):
        return lambda h: k_refs[j][0, h].astype(BF16)

    def page_values(j):
        if values_transposed:
            return lambda vh: v_refs[j][0, vh].astype(BF16)
        return lambda vh: v_refs[j][0, pl.ds(vh, page, stride=v_heads), :].astype(BF16)

    scores = [head_scores(page_keys(j), True) for j in range(step_pages)]
    tri = _strict_lower_ones(page)
    if kind == "sb":
        carry = carry_ref[...]
        pv = jnp.zeros(acc_ref.shape, F32)
        for j in range(step_pages):
            sp = _softplus(scores[j])
            log_rest = -sp
            hi, lo = _split2(log_rest)
            after = carry + _dot(hi, tri) + _dot(lo, tri)
            pv = pv + weighted_values(jnp.exp(scores[j] - sp + after), 0, page, page_values(j), values_transposed)
            carry = after[:, 0:1] + log_rest[:, 0:1]
        acc_ref[...] += pv
        carry_ref[...] = carry
    else:
        if kind == "fox":
            later = later_ref[...]
            for j in range(step_pages):
                lf = lf_refs[j][0]
                suffix = _dot_exact_rhs01(lf, tri)
                scores[j] = scores[j] + _rows_repeat(suffix + later, n_new)
                later = later + suffix[:, 0:1] + lf[:, 0:1]
            later_ref[...] = later
        s = jnp.concatenate(scores, axis=1)
        m_prev = m_ref[...]
        m_new = jnp.maximum(m_prev, jnp.max(s, axis=1, keepdims=True))
        alpha = jnp.exp(m_prev - m_new)
        pr = jnp.exp(s - m_new)
        l_ref[...] = alpha * l_ref[...] + jnp.sum(pr, axis=1, keepdims=True)
        pv = jnp.zeros(acc_ref.shape, F32)
        for j in range(step_pages):
            pv = pv + weighted_values(pr, j * page, page, page_values(j), values_transposed)
        acc_ref[...] = alpha * acc_ref[...] + pv
        m_ref[...] = m_new

    @pl.when(p == pl.num_programs(1) - 1)
    def _():
        if kind == "sb":
            accn = acc_ref[...]
        else:
            accn = acc_ref[...] / jnp.maximum(l_ref[...], 1e-30)
        blk = lambda h: accn[h * n_new:(h + 1) * n_new, :]
        if kind == "diff":
            lam = lam_ref[0, 0]
            parts = []
            for vh in range(v_heads):
                d = blk(2 * vh) - lam * blk(2 * vh + 1)
                parts.append(d * lax.rsqrt(jnp.mean(d * d, axis=-1, keepdims=True) + NORM_EPS) * sub_ref[...]
                             * out_scale)
            o_ref[0] = jnp.concatenate(parts, axis=1)
        else:
            o_ref[0] = jnp.concatenate([blk(h) for h in range(heads)], axis=1)


def sample_attn(kind, page_table, q, k_new, v_new, k_cache, v_cache, *, lf_cache_t=None, new_cum_t=None,
                lam=None, subln=None, out_scale=1.0):
    b, n_new, e = q.shape
    pool, page, heads, _ = k_cache.shape
    v_heads, vdim = v_cache.shape[2:]
    rows = n_new * heads
    n_pages = page_table.shape[1]
    last = n_pages - 1
    sp = STEP_PAGES
    assert n_pages % sp == 0
    values_transposed = vdim < LANES
    assert values_transposed or v_heads % SUBLANES == 0

    def page_spec(j, *blk):
        zeros = (0,) * len(blk)
        return pl.BlockSpec((1,) + blk, lambda bi, p, pt: (pt[bi, last - (p * sp + j)],) + zeros)

    k_t = k_cache.transpose(0, 2, 3, 1)
    if values_transposed:
        v_in, v_blk = v_cache.transpose(0, 2, 3, 1), (v_heads, vdim, page)
    else:
        v_in, v_blk = v_cache.reshape(pool, page * v_heads, vdim), (page * v_heads, vdim)
    ins = [q, _pad_rows(k_new, NEW_PAD), _pad_rows(v_new, NEW_PAD)] + [k_t] * sp + [v_in] * sp
    specs = ([pl.BlockSpec((1, n_new, e), lambda bi, p, pt: (bi, 0, 0))]
             + [pl.BlockSpec((1, NEW_PAD, e), lambda bi, p, pt: (bi, 0, 0))] * 2
             + [page_spec(j, heads, HEAD_DIM, page) for j in range(sp)]
             + [page_spec(j, *v_blk) for j in range(sp)])
    if kind == "fox":
        ins += [lf_cache_t] * sp + [new_cum_t]
        specs += [page_spec(j, heads, page) for j in range(sp)]
        specs.append(pl.BlockSpec((1, heads, NEW_PAD), lambda bi, p, pt: (bi, 0, 0)))
    if kind == "diff":
        ins += [lam.reshape(1, 1), subln.reshape(1, LANES)]
        specs += [pl.BlockSpec(memory_space=pltpu.SMEM), pl.BlockSpec((1, LANES), lambda bi, p, pt: (0, 0))]
    scratch = [pltpu.VMEM((heads, n_new, HEAD_DIM), F32)]
    if kind == "sb":
        scratch += [pltpu.VMEM((rows, 1), F32), pltpu.VMEM((rows, vdim), F32)]
    else:
        scratch += [pltpu.VMEM((rows, 1), F32), pltpu.VMEM((rows, 1), F32), pltpu.VMEM((rows, vdim), F32)]
    if kind == "fox":
        scratch.append(pltpu.VMEM((heads, 1), F32))
    return pl.pallas_call(
        functools.partial(_sattn_native_kernel, kind=kind, n_new=n_new, heads=heads, v_heads=v_heads,
                          out_scale=out_scale, step_pages=sp, page=page, values_transposed=values_transposed),
        out_shape=jax.ShapeDtypeStruct((b, n_new, e), F32),
        grid_spec=pltpu.PrefetchScalarGridSpec(
            num_scalar_prefetch=1,
            grid=(b, n_pages // sp),
            in_specs=specs,
            out_specs=pl.BlockSpec((1, n_new, e), lambda bi, p, pt: (bi, 0, 0)),
            scratch_shapes=scratch,
        ),
        compiler_params=_cparams(2),
        name="sample_attn_" + kind,
    )(page_table, *ins)


GATHER_PAGES = 8


def _softmax_tile(s, mask, v, m_ref, l_ref, acc_ref):
    if mask is not None:
        s = jnp.where(mask, s, MASK_VALUE)
    m_prev = m_ref[...]
    m_new = jnp.maximum(m_prev, jnp.max(s, axis=1, keepdims=True))
    alpha = jnp.exp(m_prev - m_new)
    p = jnp.exp(s - m_new)
    if mask is not None:
        p = jnp.where(mask, p, 0.0)
    l_ref[...] = alpha * l_ref[...] + jnp.sum(p, axis=1, keepdims=True)
    acc_ref[...] = alpha * acc_ref[...] + _dot(p.astype(BF16), v)
    m_ref[...] = m_new


def _finish_softmax(l_ref, acc_ref):
    return acc_ref[...] / jnp.maximum(l_ref[...], 1e-30)


def _reset_softmax(m_ref, l_ref, acc_ref):
    m_ref[...] = jnp.full(m_ref.shape, MASK_VALUE, F32)
    l_ref[...] = jnp.zeros(l_ref.shape, F32)
    acc_ref[...] = jnp.zeros(acc_ref.shape, F32)


def _select_blocks(p_sum, q_pos, n_sel, n_top):
    t, nc = p_sum.shape
    nb = -(-n_sel // LANES) * LANES
    c_idx = lax.broadcasted_iota(jnp.int32, (nc, nb), 0) * CMP_STRIDE
    j_idx = lax.broadcasted_iota(jnp.int32, (nc, nb), 1)
    overlap = (c_idx < j_idx * SEL_BLK + SEL_BLK) & (c_idx + CMP_BLK > j_idx * SEL_BLK) & (j_idx < n_sel)
    imp = _dot_exact_rhs01(p_sum, jnp.where(overlap, 1.0, 0.0).astype(BF16))
    blk = lax.broadcasted_iota(jnp.int32, (t, nb), 1)
    cur = q_pos // SEL_BLK
    forced = (blk == 0) | (blk == cur) | (blk == cur - 1)
    valid = blk * SEL_BLK <= q_pos
    score = jnp.where(forced, SEL_FORCE, jnp.where(valid, imp, -1.0))
    score = jnp.where(blk < n_sel, score, -3e38)
    rows = -(-n_sel // SUBLANES) * SUBLANES
    sc_t = score.T[:rows, :]
    j_iota = lax.broadcasted_iota(jnp.int32, sc_t.shape, 0)
    rank = jnp.zeros(sc_t.shape, F32)
    for jp in range(n_sel):
        other = sc_t[jp:jp + 1, :]
        ge = jnp.where(other >= sc_t, 1.0, 0.0)
        gt = jnp.where(other > sc_t, 1.0, 0.0)
        rank = rank + jnp.where(j_iota > jp, ge, gt)
    sel_t = jnp.where(rank < n_top, 1.0, 0.0)
    if rows < nb:
        sel_t = jnp.concatenate([sel_t, jnp.zeros((nb - rows, t), F32)], axis=0)
    return sel_t.T


def _expand_block_mask(sel, k0, tk):
    nb = sel.shape[1]
    j = lax.broadcasted_iota(jnp.int32, (nb, tk), 0)
    key = lax.broadcasted_iota(jnp.int32, (nb, tk), 1) + k0
    e = jnp.where(j == key // SEL_BLK, 1.0, 0.0).astype(BF16)
    return _dot(sel.astype(BF16), e)


def _gather_groups_kernel(pt_ref, *refs):
    del pt_ref
    page_refs, new_ref, o_ref = refs[:GATHER_PAGES], refs[GATHER_PAGES], refs[GATHER_PAGES + 1]
    page = new_ref.shape[1]
    is_new = pl.program_id(1) == pl.num_programs(1) - 1

    @pl.when(is_new)
    def _():
        o_ref[...] = jnp.zeros(o_ref.shape, F32)
        for g in range(GROUP):
            o_ref[0, g, 0:page, :] = new_ref[0, :, g * HEAD_DIM:(g + 1) * HEAD_DIM]

    @pl.when(jnp.logical_not(is_new))
    def _():
        for j in range(GATHER_PAGES):
            for g in range(GROUP):
                o_ref[0, g, j * page:(j + 1) * page, :] = page_refs[j][0, g].T


def gather_groups(cache_t, page_table, new_rows):
    b, n_pages = page_table.shape
    page = cache_t.shape[3]
    gp = GATHER_PAGES
    assert n_pages % gp == 0
    steps = n_pages // gp
    new_pad = _pad_rows(new_rows, page)

    def page_spec(j):
        return pl.BlockSpec((1, GROUP, HEAD_DIM, page),
                            lambda bi, p, pt: (pt[bi, jnp.minimum(p, steps - 1) * gp + j], 0, 0, 0))

    return pl.pallas_call(
        _gather_groups_kernel,
        out_shape=jax.ShapeDtypeStruct((b, GROUP, (n_pages + gp) * page, HEAD_DIM), F32),
        grid_spec=pltpu.PrefetchScalarGridSpec(
            num_scalar_prefetch=1,
            grid=(b, steps + 1),
            in_specs=[page_spec(j) for j in range(gp)]
            + [pl.BlockSpec((1, page, GLANES), lambda bi, p, pt: (bi, 0, 0))],
            out_specs=pl.BlockSpec((1, GROUP, gp * page, HEAD_DIM), lambda bi, p, pt: (bi, 0, p, 0)),
        ),
        compiler_params=_cparams(2),
        name="gather_groups",
    )(page_table, *([cache_t] * gp), new_pad)


def _snsa_kernel(*refs, n_new, pos0, n_sel, step_pages):
    (pt_ref, qc_ref, qr_ref, gate_ref, kc_ref, vc_ref, kwb_ref, vwb_ref, ksn_ref, vsn_ref, kwn_ref,
     vwn_ref) = refs[:12]
    ks_refs = refs[12:12 + step_pages]
    vs_refs = refs[12 + step_pages:12 + 2 * step_pages]
    o_ref, qs_ref, qg_ref, sel_ref, oc_ref, ow_ref, m_ref, l_ref, acc_ref, accs_ref = refs[12 + 2 * step_pages:]
    del pt_ref
    p = pl.program_id(1)
    heads = GROUP * GROUP
    rows = n_new * heads
    grows = GROUP * n_new
    scale = HEAD_DIM ** -0.5
    page = ks_refs[0].shape[3]

    def own_lanes(x):
        return jnp.concatenate([x[g * grows:(g + 1) * grows, g * HEAD_DIM:(g + 1) * HEAD_DIM]
                                for g in range(GROUP)], axis=0)

    def group_scores(key_of_group, keys_transposed):
        dot = _dot if keys_transposed else _nt_dot
        return jnp.concatenate([dot(qg_ref[g].astype(BF16), key_of_group(g)) for g in range(GROUP)], axis=0)

    def group_values(pr, c0, width, value_of_group, values_transposed):
        dot = _nt_dot if values_transposed else _dot
        return jnp.concatenate([dot(pr[g * grows:(g + 1) * grows, c0:c0 + width].astype(BF16), value_of_group(g))
                                for g in range(GROUP)], axis=0)

    @pl.when(p == 0)
    def _():
        t_idx = lax.broadcasted_iota(jnp.int32, (rows, 1), 0) % n_new
        q_pos = t_idx + pos0
        nc = kc_ref.shape[1]
        s = _nt_dot((qc_ref[0] * scale).astype(BF16), kc_ref[0].astype(BF16))
        cmp_end = lax.broadcasted_iota(jnp.int32, (rows, nc), 1) * CMP_STRIDE + (CMP_BLK - 1)
        mask = cmp_end <= q_pos
        s = jnp.where(mask, s, MASK_VALUE)
        pc = jnp.where(mask, jnp.exp(s - jnp.max(s, axis=1, keepdims=True)), 0.0)
        pc = pc / jnp.maximum(jnp.sum(pc, axis=1, keepdims=True), 1e-30)
        oc_ref[...] = own_lanes(_dot(pc.astype(BF16), vc_ref[0].astype(BF16)))
        i_idx = lax.broadcasted_iota(jnp.int32, (rows, rows), 0)
        r_idx = lax.broadcasted_iota(jnp.int32, (rows, rows), 1)
        same = (r_idx // grows == i_idx // n_new) & (r_idx % n_new == i_idx % n_new)
        group_sum = jnp.where(same & (i_idx < grows), 1.0, 0.0).astype(BF16)
        p_sum = _dot_exact_lhs01(group_sum, pc)
        g_pos = lax.broadcasted_iota(jnp.int32, (rows, 1), 0) % n_new + pos0
        sel = _select_blocks(p_sum, g_pos, n_sel, min(N_SELECT, n_sel))
        spread = jnp.where(r_idx == (i_idx // grows) * n_new + i_idx % n_new, 1.0, 0.0).astype(BF16)
        sel_ref[...] = _dot(spread, sel.astype(BF16)).astype(BF16)
        qr = qr_ref[0] * scale
        qs_ref[...] = qr.astype(BF16)
        for g in range(GROUP):
            qg_ref[g] = qr[g * grows:(g + 1) * grows, g * HEAD_DIM:(g + 1) * HEAD_DIM]
        _reset_softmax(m_ref, l_ref, acc_ref)
        wlen = kwb_ref.shape[1]
        s = _nt_dot(qs_ref[...], kwb_ref[0].astype(BF16))
        dist = t_idx + wlen - lax.broadcasted_iota(jnp.int32, (rows, wlen), 1)
        _softmax_tile(s, (dist >= 0) & (dist < WINDOW), vwb_ref[0].astype(BF16), m_ref, l_ref, acc_ref)
        s = _nt_dot(qs_ref[...], kwn_ref[0].astype(BF16))
        s_idx = lax.broadcasted_iota(jnp.int32, (rows, NEW_PAD), 1)
        _softmax_tile(s, s_idx <= t_idx, vwn_ref[0].astype(BF16), m_ref, l_ref, acc_ref)
        ow_ref[...] = own_lanes(_finish_softmax(l_ref, acc_ref))
        ksn = ksn_ref[0].astype(BF16)
        vsn = vsn_ref[0].astype(BF16)
        s = group_scores(lambda g: ksn[:, g * HEAD_DIM:(g + 1) * HEAD_DIM], False)
        chosen = _expand_block_mask(sel_ref[...], pos0, NEW_PAD)
        mask = jnp.where(s_idx <= t_idx, chosen, 0.0) > 0.5
        s = jnp.where(mask, s, MASK_VALUE)
        m_new = jnp.max(s, axis=1, keepdims=True)
        pr = jnp.where(mask, jnp.exp(s - m_new), 0.0)
        m_ref[...] = m_new
        l_ref[...] = jnp.sum(pr, axis=1, keepdims=True)
        accs_ref[...] = group_values(pr, 0, NEW_PAD, lambda g: vsn[:, g * HEAD_DIM:(g + 1) * HEAD_DIM], False)

    scores = [group_scores(lambda g, j=j: ks_refs[j][0, g].astype(BF16), True) for j in range(step_pages)]
    chosen = [_expand_block_mask(sel_ref[...], (p * step_pages + j) * page, page) for j in range(step_pages)]
    mask = jnp.concatenate(chosen, axis=1) > 0.5
    s = jnp.where(mask, jnp.concatenate(scores, axis=1), MASK_VALUE)
    m_prev = m_ref[...]
    m_new = jnp.maximum(m_prev, jnp.max(s, axis=1, keepdims=True))
    alpha = jnp.exp(m_prev - m_new)
    pr = jnp.where(mask, jnp.exp(s - m_new), 0.0)
    l_ref[...] = alpha * l_ref[...] + jnp.sum(pr, axis=1, keepdims=True)
    pv = jnp.zeros(accs_ref.shape, F32)
    for j in range(step_pages):
        pv = pv + group_values(pr, j * page, page, lambda g, j=j: vs_refs[j][0, g].astype(BF16), True)
    accs_ref[...] = alpha * accs_ref[...] + pv
    m_ref[...] = m_new

    @pl.when(p == pl.num_programs(1) - 1)
    def _():
        o_s = _finish_softmax(l_ref, accs_ref)
        gates = jax.nn.sigmoid(gate_ref[0])
        o_ref[0] = gates[:, 0:1] * oc_ref[...] + gates[:, 1:2] * o_s + gates[:, 2:3] * ow_ref[...]


def _group_rows(q):
    b, t, _ = q.shape
    qh = q.reshape(b, t, GROUP, GROUP, HEAD_DIM)
    eye = jnp.eye(GROUP, dtype=q.dtype)
    return jnp.einsum("btgrd,gk->bgrtkd", qh, eye).reshape(b, t * GROUP * GROUP, GLANES)


def sample_nsa(page_table, pos0, qc, qr, gate_rows, kcmp, vcmp, kwin_buf, vwin_buf, ks_new, vs_new, kw_new,
               vw_new, ks_cache, vs_cache):
    b, rows, _ = qc.shape
    n_pages = page_table.shape[1]
    page = ks_cache.shape[3]
    n_new = rows // (GROUP * GROUP)
    assert pos0 == n_pages * page and pos0 % SEL_BLK == 0 and n_new <= SEL_BLK
    n_sel = -(-(pos0 + n_new) // SEL_BLK)
    nb = -(-n_sel // LANES) * LANES
    nc = kcmp.shape[1]
    wlen = kwin_buf.shape[1]
    per_b = lambda r, c: pl.BlockSpec((1, r, c), lambda bi, p, pt: (bi, 0, 0))
    sp = STEP_PAGES
    assert n_pages % sp == 0

    def page_spec(j):
        return pl.BlockSpec((1, GROUP, HEAD_DIM, page), lambda bi, p, pt: (pt[bi, p * sp + j], 0, 0, 0))

    return pl.pallas_call(
        functools.partial(_snsa_kernel, n_new=n_new, pos0=pos0, n_sel=n_sel, step_pages=sp),
        out_shape=jax.ShapeDtypeStruct((b, rows, HEAD_DIM), F32),
        grid_spec=pltpu.PrefetchScalarGridSpec(
            num_scalar_prefetch=1,
            grid=(b, n_pages // sp),
            in_specs=[per_b(rows, GLANES), per_b(rows, GLANES), per_b(rows, SUBLANES), per_b(nc, GLANES),
                      per_b(nc, GLANES), per_b(wlen, GLANES), per_b(wlen, GLANES)]
            + [per_b(NEW_PAD, GLANES)] * 4 + [page_spec(j) for j in range(sp)] * 2,
            out_specs=per_b(rows, HEAD_DIM),
            scratch_shapes=[
                pltpu.VMEM((rows, GLANES), BF16),
                pltpu.VMEM((GROUP, rows // GROUP, HEAD_DIM), F32),
                pltpu.VMEM((rows, nb), BF16),
                pltpu.VMEM((rows, HEAD_DIM), F32),
                pltpu.VMEM((rows, HEAD_DIM), F32),
                pltpu.VMEM((rows, 1), F32),
                pltpu.VMEM((rows, 1), F32),
                pltpu.VMEM((rows, GLANES), F32),
                pltpu.VMEM((rows, HEAD_DIM), F32),
            ],
        ),
        compiler_params=_cparams(2),
        name="sample_nsa",
    )(page_table, qc, qr, gate_rows, kcmp, vcmp, kwin_buf, vwin_buf, ks_new, vs_new, kw_new, vw_new,
      *([ks_cache] * sp), *([vs_cache] * sp))


def sample_nsa_mixer(proj, gate, pos0, cos_t, sin_t, cache, page_table, p):
    b, t, _ = proj.shape
    flat = lambda c: c.reshape(c.shape[0], c.shape[1], GLANES)
    q_rot = rope_cols(proj, 0, 1024, cos_t, sin_t)
    ks = rope_cols(proj, 1536, 256, cos_t, sin_t)
    kw = rope_cols(proj, 2048, 256, cos_t, sin_t)
    kc, vc = proj[..., 1024:1280], proj[..., 1280:1536]
    vs, vw = proj[..., 1792:2048], proj[..., 2304:2560]

    def summaries(cache_rows, new_rows, idx):
        rows = gather_groups(cache_rows.transpose(0, 2, 3, 1), page_table, new_rows)
        cmp = compress_rows(rows, p["nsa_cmp_pe"][idx], p["nsa_cmp_w1"][idx], p["nsa_cmp_b1"][idx],
                            p["nsa_cmp_w2"][idx])
        cmp = cmp.transpose(0, 2, 1, 3).reshape(b, cmp.shape[2], GLANES)
        return _pad_rows(cmp, -(-cmp.shape[1] // LANES) * LANES)

    kcmp = summaries(cache["nsa_kcmp"], kc, 0)
    vcmp = summaries(cache["nsa_vcmp"], vc, 1)
    heads = GROUP * GROUP
    gate_rows = gate[..., :3 * heads].reshape(b, t, 3, heads).transpose(0, 3, 1, 2).reshape(b, heads * t, 3)
    gate_rows = jnp.pad(gate_rows, ((0, 0), (0, 0), (0, SUBLANES - 3)))
    newp = lambda a: _pad_rows(a, NEW_PAD)
    o_rows = sample_nsa(page_table, pos0, _group_rows(proj[..., :1024]), _group_rows(q_rot), gate_rows, kcmp, vcmp,
                        flat(cache["nsa_kwin"]), flat(cache["nsa_vwin"]), newp(ks), newp(vs), newp(kw), newp(vw),
                        cache["nsa_ksel"].transpose(0, 2, 3, 1), cache["nsa_vsel"].transpose(0, 2, 3, 1))
    o = o_rows.reshape(b, heads, t, HEAD_DIM).transpose(0, 2, 1, 3).reshape(b, t, heads * HEAD_DIM)
    g4 = lambda a: a.reshape(b, t, GROUP, HEAD_DIM)
    wlen = min(WINDOW, pos0 + t)
    kw_all = jnp.concatenate([cache["nsa_kwin"], g4(kw)], axis=1)
    vw_all = jnp.concatenate([cache["nsa_vwin"], g4(vw)], axis=1)
    return o, (g4(kc), g4(vc), g4(ks), g4(vs)), (kw_all[:, -wlen:], vw_all[:, -wlen:])


DEPTH = 4
D_FF = 2816
CONV_W = 3
X_HEADS = 4
X_HDIM = 128


def _pad_cols(w, n):
    return jnp.pad(w, ((0, 0), (0, n - w.shape[1])))


def _pad_vec(v, n):
    return jnp.pad(v, (0, n - v.shape[0]))


def _to_groups(x):
    b, t, _ = x.shape
    return x.reshape(b, t, GROUP, HEAD_DIM).transpose(0, 2, 1, 3)


def prepare_weights(p):
    w = {}
    hd = FORGET_HEADS * HEAD_DIM
    w["fox_main"] = p["fox_w_in"][:, :3 * hd].astype(BF16)
    w["fox_gate"] = _pad_cols(p["fox_w_in"][:, 3 * hd:], LANES).astype(BF16)
    w["fox_bias"] = _pad_vec(p["fox_b_f"], LANES)
    w["fox_out"] = p["fox_w_out"].astype(BF16)
    w["sb_in"] = p["sb_w_in"].astype(BF16)
    w["sb_out"] = p["sb_w_out"].astype(BF16)
    nsa_main = hd + 6 * GLANES
    w["nsa_main"] = p["nsa_w_in"][:, :nsa_main].astype(BF16)
    w["nsa_gate"] = _pad_cols(p["nsa_w_in"][:, nsa_main:], LANES).astype(BF16)
    w["nsa_out"] = p["nsa_w_out"].astype(BF16)
    w["diff_in"] = p["diff_w_in"].astype(BF16)
    w["diff_out"] = p["diff_w_out"].astype(BF16)
    w["x_q"] = p["x_w_q"].astype(BF16)
    w["x_o"] = p["x_w_o"].astype(BF16)
    w["ffn_up"] = p["ffn_w_up"].astype(BF16)
    w["ffn_down"] = p["ffn_w_down"].astype(BF16)
    w["conv_pack"] = jnp.concatenate(
        [p["ffn_conv_w"], p["ffn_conv_b"][:, None, :], jnp.zeros((DEPTH, SUBLANES - CONV_W - 1, 2 * D_FF), F32)],
        axis=1)
    return w


def _diff_lambda(p, lambda_init):
    return (jnp.exp(jnp.sum(p["diff_lq1"] * p["diff_lk1"])) - jnp.exp(jnp.sum(p["diff_lq2"] * p["diff_lk2"]))
            + lambda_init)


def _lambda_init(layer):
    return 0.8 - 0.6 * math.exp(-0.3 * layer)


def _layer_tail(x, layer, p, w, mem_k, mem_v, state8):
    b, t, d = x.shape
    x = cross_block(x, p["norm_mem"][layer], w["x_q"][layer], mem_k[layer].reshape(b, -1, X_HEADS * X_HDIM),
                    mem_v[layer].reshape(b, -1, X_HEADS * X_HDIM), w["x_o"][layer])
    x2 = x.reshape(b * t, d)
    u = matmul(rmsnorm_rows(x2, p["norm_ffn"][layer], BF16), w["ffn_up"][layer]).reshape(b, t, 2 * D_FF)
    act = conv_gate(u, state8, w["conv_pack"][layer])
    x = matmul(act.reshape(b * t, D_FF), w["ffn_down"][layer], res=x2).reshape(b, t, d)
    return x, u


def prompt_trunk(x, mem_k, mem_v, p, w):
    b, t, d = x.shape
    n = b * t
    new = {}
    conv_rows = []
    cos_t, sin_t = rope_tables(jnp.arange(t))
    zero_state = jnp.zeros((b, SUBLANES, 2 * D_FF), F32)
    for layer in range(DEPTH):
        kind = layer % 4
        x2 = x.reshape(n, d)
        xn = rmsnorm_rows(x2, p["norm_mix"][layer], BF16)
        if kind == 0:
            proj = matmul(xn, w["fox_main"]).reshape(b, t, -1)
            gate = matmul(xn, w["fox_gate"]).reshape(b, t, LANES)
            logf, _, cum_pieces = logf_cumsum(gate, w["fox_bias"])
            v_t = proj[..., 2048:3072].transpose(0, 2, 1)
            o = prompt_flash("fox", proj, 0, proj, 4, v_t, cp=cum_pieces)
            x = matmul(o.reshape(n, -1), w["fox_out"], res=x2).reshape(b, t, d)
            new["fox_k"] = proj[..., 1024:2048].reshape(b, t, 16, 64)
            new["fox_v"] = proj[..., 2048:3072].reshape(b, t, 16, 64)
            new["fox_logf"] = logf[..., :FORGET_HEADS]
        elif kind == 1:
            proj = matmul(xn, w["sb_in"]).reshape(b, t, -1)
            o = prompt_sb(proj, proj[..., 2048:3072].transpose(0, 2, 1))
            x = matmul(o.reshape(n, -1), w["sb_out"], res=x2).reshape(b, t, d)
            new["sb_k"] = proj[..., 1024:2048].reshape(b, t, 16, 64)
            new["sb_v"] = proj[..., 2048:3072].reshape(b, t, 16, 64)
        elif kind == 2:
            proj = matmul(xn, w["nsa_main"]).reshape(b, t, -1)
            gate = matmul(xn, w["nsa_gate"]).reshape(b, t, LANES)
            q_rot = rope_cols(proj, 0, 1024, cos_t, sin_t)
            kc, vc = proj[..., 1024:1280], proj[..., 1280:1536]
            vs, vw = proj[..., 1792:2048], proj[..., 2304:2560]
            ks = rope_cols(proj, 1536, 256, cos_t, sin_t)
            kw = rope_cols(proj, 2048, 256, cos_t, sin_t)
            tp = -(-t // SEL_BLK) * SEL_BLK
            tpad = ((0, 0), (0, 0), (0, tp - t), (0, 0))
            kcmp = compress_rows(jnp.pad(_to_groups(kc), tpad), p["nsa_cmp_pe"][0], p["nsa_cmp_w1"][0],
                                 p["nsa_cmp_b1"][0], p["nsa_cmp_w2"][0])
            vcmp = compress_rows(jnp.pad(_to_groups(vc), tpad), p["nsa_cmp_pe"][1], p["nsa_cmp_w1"][1],
                                 p["nsa_cmp_b1"][1], p["nsa_cmp_w2"][1])
            ncp = -(-kcmp.shape[2] // LANES) * LANES
            cpad = ((0, 0), (0, 0), (0, ncp - kcmp.shape[2]), (0, 0))
            gate_t = gate[..., :3 * GROUP * GROUP].reshape(b, t, 3, GROUP, GROUP).transpose(0, 3, 2, 4, 1)
            gate_t = jnp.pad(gate_t.reshape(b, GROUP, 3 * GROUP, t), ((0, 0), (0, 0), (0, GROUP), (0, 0)))
            o = prompt_nsa(proj, q_rot, gate_t, jnp.pad(kcmp, cpad), jnp.pad(vcmp, cpad).transpose(0, 1, 3, 2),
                           _to_groups(ks), vs.transpose(0, 2, 1), _to_groups(kw), vw.transpose(0, 2, 1))
            x = matmul(o.reshape(n, -1), w["nsa_out"], res=x2).reshape(b, t, d)
            g4 = lambda a: a.reshape(b, t, GROUP, HEAD_DIM)
            new["nsa_kcmp"], new["nsa_vcmp"], new["nsa_ksel"], new["nsa_vsel"] = g4(kc), g4(vc), g4(ks), g4(vs)
            wlen = min(WINDOW, t)
            new["nsa_kwin"], new["nsa_vwin"] = g4(kw)[:, t - wlen:], g4(vw)[:, t - wlen:]
        else:
            proj = matmul(xn, w["diff_in"]).reshape(b, t, -1)
            qk = rope_cols(proj, 0, 2048, cos_t, sin_t)
            li = _lambda_init(layer)
            o = prompt_flash("diff", qk, 0, qk, 4, proj[..., 2048:3072].transpose(0, 2, 1),
                             lam=_diff_lambda(p, li), subln=p["diff_subln"], out_scale=1.0 - li)
            x = matmul(o.reshape(n, -1), w["diff_out"], res=x2).reshape(b, t, d)
            new["diff_k"] = qk[..., 1024:2048].reshape(b, t, 16, 64)
            new["diff_v"] = proj[..., 2048:3072].reshape(b, t, 8, 128)
        x, u = _layer_tail(x, layer, p, w, mem_k, mem_v, zero_state)
        conv_rows.append(u[:, t - (CONV_W - 1):])
    new["ffn_conv"] = jnp.stack(conv_rows)
    y = rmsnorm_rows(x.reshape(n, d), p["norm_final"]).reshape(b, t, d)
    return y, new


def sample_trunk(x, pos0, mem_k, mem_v, conv_state, cache, page_table, p, w):
    b, t, d = x.shape
    n = b * t
    new = {}
    conv_rows = []
    cos_t, sin_t = rope_tables(pos0 + jnp.arange(t))
    for layer in range(DEPTH):
        kind = layer % 4
        x2 = x.reshape(n, d)
        xn = rmsnorm_rows(x2, p["norm_mix"][layer], BF16)
        if kind == 0:
            proj = matmul(xn, w["fox_main"]).reshape(b, t, -1)
            gate = matmul(xn, w["fox_gate"]).reshape(b, t, LANES)
            logf, cum, _ = logf_cumsum(gate, w["fox_bias"])
            new_cum_t = _pad_rows(cum[..., :FORGET_HEADS], NEW_PAD).transpose(0, 2, 1)
            o = sample_attn("fox", page_table, proj[..., :1024], proj[..., 1024:2048], proj[..., 2048:3072],
                            cache["fox_k"], cache["fox_v"],
                            lf_cache_t=cache["fox_logf"].transpose(0, 2, 1), new_cum_t=new_cum_t)
            x = matmul(o.reshape(n, -1), w["fox_out"], res=x2).reshape(b, t, d)
            new["fox_k"] = proj[..., 1024:2048].reshape(b, t, 16, 64)
            new["fox_v"] = proj[..., 2048:3072].reshape(b, t, 16, 64)
            new["fox_logf"] = logf[..., :FORGET_HEADS]
        elif kind == 1:
            proj = matmul(xn, w["sb_in"]).reshape(b, t, -1)
            o = sample_attn("sb", page_table, proj[..., :1024], proj[..., 1024:2048], proj[..., 2048:3072],
                            cache["sb_k"], cache["sb_v"])
            x = matmul(o.reshape(n, -1), w["sb_out"], res=x2).reshape(b, t, d)
            new["sb_k"] = proj[..., 1024:2048].reshape(b, t, 16, 64)
            new["sb_v"] = proj[..., 2048:3072].reshape(b, t, 16, 64)
        elif kind == 2:
            proj = matmul(xn, w["nsa_main"]).reshape(b, t, -1)
            gate = matmul(xn, w["nsa_gate"]).reshape(b, t, LANES)
            o, rows, win = sample_nsa_mixer(proj, gate, pos0, cos_t, sin_t, cache, page_table, p)
            x = matmul(o.reshape(n, -1), w["nsa_out"], res=x2).reshape(b, t, d)
            new["nsa_kcmp"], new["nsa_vcmp"], new["nsa_ksel"], new["nsa_vsel"] = rows
            new["nsa_kwin"], new["nsa_vwin"] = win
        else:
            proj = matmul(xn, w["diff_in"]).reshape(b, t, -1)
            qk = rope_cols(proj, 0, 2048, cos_t, sin_t)
            li = _lambda_init(layer)
            o = sample_attn("diff", page_table, qk[..., :1024], qk[..., 1024:2048], proj[..., 2048:3072],
                            cache["diff_k"], cache["diff_v"],
                            lam=_diff_lambda(p, li), subln=p["diff_subln"], out_scale=1.0 - li)
            x = matmul(o.reshape(n, -1), w["diff_out"], res=x2).reshape(b, t, d)
            new["diff_k"] = qk[..., 1024:2048].reshape(b, t, 16, 64)
            new["diff_v"] = proj[..., 2048:3072].reshape(b, t, 8, 128)
        state8 = jnp.pad(conv_state[layer], ((0, 0), (SUBLANES - (CONV_W - 1), 0), (0, 0)))
        x, u = _layer_tail(x, layer, p, w, mem_k, mem_v, state8)
        ext = jnp.concatenate([conv_state[layer], u], axis=1)
        conv_rows.append(ext[:, -(CONV_W - 1):])
    new["ffn_conv"] = jnp.stack(conv_rows)
    y = rmsnorm_rows(x.reshape(n, d), p["norm_final"]).reshape(b, t, d)
    return y, new


def memory_kv(mem, mem_norm, w_kv):
    b, m, d = mem.shape
    e = X_HEADS * X_HDIM
    kv = jnp.stack([matmul(mem.reshape(b * m, d), w_kv[layer].astype(BF16), gain=mem_norm[layer])
                    for layer in range(DEPTH)])
    xk = kv[..., :e].reshape(DEPTH, b, m, X_HEADS, X_HDIM)
    xv = kv[..., e:].reshape(DEPTH, b, m, X_HEADS, X_HDIM)
    return xk, xv


def kernel(x_prompt, x_sample, cache_fox_k, cache_fox_v, cache_fox_logf, cache_sb_k, cache_sb_v,
           cache_nsa_kcmp, cache_nsa_vcmp, cache_nsa_ksel, cache_nsa_vsel, state_nsa_kwin, state_nsa_vwin,
           cache_diff_k, cache_diff_v, cache_mem_k, cache_mem_v, state_ffn_conv, page_table, mem_prompt,
           norm_mix, norm_mem, norm_ffn, norm_final, fox_w_in, fox_b_f, fox_w_out, sb_w_in, sb_w_out,
           nsa_w_in, nsa_cmp_pe, nsa_cmp_w1, nsa_cmp_b1, nsa_cmp_w2, nsa_w_out,
           diff_w_in, diff_lq1, diff_lk1, diff_lq2, diff_lk2, diff_subln, diff_w_out,
           mem_norm, x_w_q, x_w_kv, x_w_o, ffn_w_up, ffn_conv_w, ffn_conv_b, ffn_w_down):
    p = {
        "norm_mix": norm_mix, "norm_mem": norm_mem, "norm_ffn": norm_ffn, "norm_final": norm_final,
        "fox_w_in": fox_w_in, "fox_b_f": fox_b_f, "fox_w_out": fox_w_out,
        "sb_w_in": sb_w_in, "sb_w_out": sb_w_out,
        "nsa_w_in": nsa_w_in, "nsa_cmp_pe": nsa_cmp_pe, "nsa_cmp_w1": nsa_cmp_w1, "nsa_cmp_b1": nsa_cmp_b1,
        "nsa_cmp_w2": nsa_cmp_w2, "nsa_w_out": nsa_w_out,
        "diff_w_in": diff_w_in, "diff_lq1": diff_lq1, "diff_lk1": diff_lk1, "diff_lq2": diff_lq2,
        "diff_lk2": diff_lk2, "diff_subln": diff_subln, "diff_w_out": diff_w_out,
        "x_w_q": x_w_q, "x_w_o": x_w_o,
        "ffn_w_up": ffn_w_up, "ffn_conv_w": ffn_conv_w, "ffn_conv_b": ffn_conv_b, "ffn_w_down": ffn_w_down,
    }
    w = prepare_weights(p)
    mem_k_p, mem_v_p = memory_kv(mem_prompt, mem_norm, x_w_kv)
    y_prompt, sp = prompt_trunk(x_prompt, mem_k_p, mem_v_p, p, w)
    cache = {
        "fox_k": cache_fox_k, "fox_v": cache_fox_v, "fox_logf": cache_fox_logf,
        "sb_k": cache_sb_k, "sb_v": cache_sb_v,
        "nsa_kcmp": cache_nsa_kcmp, "nsa_vcmp": cache_nsa_vcmp, "nsa_ksel": cache_nsa_ksel,
        "nsa_vsel": cache_nsa_vsel, "nsa_kwin": state_nsa_kwin, "nsa_vwin": state_nsa_vwin,
        "diff_k": cache_diff_k, "diff_v": cache_diff_v,
    }
    past_len = page_table.shape[1] * cache_fox_k.shape[1]
    y_sample, ss = sample_trunk(x_sample, past_len, cache_mem_k, cache_mem_v, state_ffn_conv, cache, page_table,
                                p, w)
    return (y_prompt, y_sample,
            sp["fox_k"], sp["fox_v"], sp["fox_logf"], sp["sb_k"], sp["sb_v"],
            sp["nsa_kcmp"], sp["nsa_vcmp"], sp["nsa_ksel"], sp["nsa_vsel"], sp["nsa_kwin"], sp["nsa_vwin"],
            sp["diff_k"], sp["diff_v"], mem_k_p, mem_v_p, sp["ffn_conv"],
            ss["fox_k"], ss["fox_v"], ss["fox_logf"], ss["sb_k"], ss["sb_v"],
            ss["nsa_kcmp"], ss["nsa_vcmp"], ss["nsa_ksel"], ss["nsa_vsel"], ss["nsa_kwin"], ss["nsa_vwin"],
            ss["diff_k"], ss["diff_v"], ss["ffn_conv"])
```

```python
import functools
import math

import jax
import jax.numpy as jnp
from jax import lax
from jax.experimental import pallas as pl
from jax.experimental.pallas import tpu as pltpu

F32 = jnp.float32
BF16 = jnp.bfloat16

V7X_VMEM_BYTES = 64 * 1024 * 1024
LANES = 128
SUBLANES = 8
VMEM_LIMIT = V7X_VMEM_BYTES - 8 * 1024 * 1024

NORM_EPS = 1e-6
MASK_VALUE = -1e30
ROPE_THETA = 10000.0
FORGET_HEADS = 16
HEAD_DIM = 64
GROUP = 4
GLANES = GROUP * HEAD_DIM


def _cparams(n_axes):
    return pltpu.CompilerParams(dimension_semantics=("arbitrary",) * n_axes, vmem_limit_bytes=VMEM_LIMIT)


def _nt_dot(a, b):
    return lax.dot_general(a, b, (((1,), (1,)), ((), ())), preferred_element_type=F32)


def _dot(a, b):
    return jnp.dot(a, b, preferred_element_type=F32)


def _split3(x):
    hi = x.astype(BF16)
    r1 = x - hi.astype(F32)
    mid = r1.astype(BF16)
    lo = (r1 - mid.astype(F32)).astype(BF16)
    return hi, mid, lo


def _split2(x):
    hi = x.astype(BF16)
    lo = (x - hi.astype(F32)).astype(BF16)
    return hi, lo


def _dot_exact_rhs01(x, m01):
    hi, mid, lo = _split3(x)
    return _dot(hi, m01) + _dot(mid, m01) + _dot(lo, m01)


def _dot_exact_lhs01(m01, x):
    hi, mid, lo = _split3(x)
    return _dot(m01, hi) + _dot(m01, mid) + _dot(m01, lo)


def _softplus(z):
    return jnp.maximum(z, 0.0) + jnp.log(1.0 + jnp.exp(-jnp.abs(z)))


def _log_sigmoid(z):
    return jnp.minimum(z, 0.0) - jnp.log1p(jnp.exp(-jnp.abs(z)))


def _mm_kernel(*refs, prologue, has_res, stage):
    it = iter(refs)
    x_ref = next(it)
    if prologue == "norm":
        g_ref = next(it)
    elif prologue == "add_gelu":
        x2_ref, pb_ref = next(it), next(it)
    w_ref = next(it)
    r_ref = next(it) if has_res else None
    o_ref = next(it)
    xs_ref = next(it) if stage else None

    if stage:
        @pl.when(pl.program_id(1) == 0)
        def _():
            x = x_ref[...].astype(F32)
            if prologue == "norm":
                x = x * lax.rsqrt(jnp.mean(x * x, axis=-1, keepdims=True) + NORM_EPS)
                x = x * g_ref[...]
            elif prologue == "add_gelu":
                x = jax.nn.gelu(x + x2_ref[...] + pb_ref[...])
            xs_ref[...] = x.astype(BF16)
        xb = xs_ref[...]
    else:
        xb = x_ref[...]
    y = _dot(xb, w_ref[...])
    if has_res:
        y = y + r_ref[...]
    o_ref[...] = y.astype(o_ref.dtype)


def _row_tile(n, cap):
    t = min(n, cap)
    while n % t:
        t //= 2
    return t


def matmul(x, w, *, gain=None, add=None, pre_bias=None, res=None, out_dtype=F32, tm_cap=1024, tn_cap=512):
    n, k = x.shape
    e = w.shape[1]
    tm = _row_tile(n, tm_cap)
    tn = _row_tile(e, tn_cap)
    assert n % tm == 0 and e % tn == 0 and tm % SUBLANES == 0 and tn % LANES == 0, (n, e, tm, tn)
    prologue = "norm" if gain is not None else ("add_gelu" if add is not None else None)
    stage = prologue is not None or x.dtype != BF16
    ins = [x]
    specs = [pl.BlockSpec((tm, k), lambda i, j: (i, 0))]
    if prologue == "norm":
        ins.append(gain.reshape(1, k).astype(F32))
        specs.append(pl.BlockSpec((1, k), lambda i, j: (0, 0)))
    elif prologue == "add_gelu":
        ins += [add, pre_bias.reshape(1, k).astype(F32)]
        specs += [pl.BlockSpec((tm, k), lambda i, j: (i, 0)), pl.BlockSpec((1, k), lambda i, j: (0, 0))]
    ins.append(w)
    specs.append(pl.BlockSpec((k, tn), lambda i, j: (0, j)))
    if res is not None:
        ins.append(res)
        specs.append(pl.BlockSpec((tm, tn), lambda i, j: (i, j)))
    kern = functools.partial(_mm_kernel, prologue=prologue, has_res=res is not None, stage=stage)
    return pl.pallas_call(
        kern,
        out_shape=jax.ShapeDtypeStruct((n, e), out_dtype),
        grid=(n // tm, e // tn),
        in_specs=specs,
        out_specs=pl.BlockSpec((tm, tn), lambda i, j: (i, j)),
        scratch_shapes=[pltpu.VMEM((tm, k), BF16)] if stage else [],
        compiler_params=_cparams(2),
        name="matmul",
    )(*ins)


def _rmsnorm_kernel(x_ref, g_ref, o_ref):
    x = x_ref[...]
    y = x * lax.rsqrt(jnp.mean(x * x, axis=-1, keepdims=True) + NORM_EPS) * g_ref[...]
    o_ref[...] = y.astype(o_ref.dtype)


def rmsnorm_rows(x, gain, out_dtype=F32):
    n, d = x.shape
    tm = _row_tile(n, 1024)
    return pl.pallas_call(
        _rmsnorm_kernel,
        out_shape=jax.ShapeDtypeStruct((n, d), out_dtype),
        grid=(n // tm,),
        in_specs=[pl.BlockSpec((tm, d), lambda i: (i, 0)), pl.BlockSpec((1, d), lambda i: (0, 0))],
        out_specs=pl.BlockSpec((tm, d), lambda i: (i, 0)),
        compiler_params=_cparams(1),
        name="rmsnorm",
    )(x, gain.reshape(1, d))


def _conv_gate_kernel(uv_ref, ug_ref, hv_ref, hg_ref, sv_ref, sg_ref, wv_ref, wg_ref, o_ref):
    first = pl.program_id(1) == 0

    def conv(u_ref, h_ref, s_ref, w_ref):
        u = u_ref[0]
        prev = jnp.where(first, s_ref[0], h_ref[0])
        p2 = prev[6:7, :]
        p1 = prev[7:8, :]
        row = lax.broadcasted_iota(jnp.int32, u.shape, 0)
        um1 = jnp.where(row == 0, p1, pltpu.roll(u, 1, 0))
        um2 = jnp.where(row == 0, p2, jnp.where(row == 1, p1, pltpu.roll(u, 2, 0)))
        w = w_ref[...]
        return w[3:4, :] + w[0:1, :] * um2 + w[1:2, :] * um1 + w[2:3, :] * u

    val = conv(uv_ref, hv_ref, sv_ref, wv_ref)
    gate = conv(ug_ref, hg_ref, sg_ref, wg_ref)
    o_ref[0] = (gate * jax.nn.sigmoid(gate) * val).astype(o_ref.dtype)


def conv_gate(u, state8, wpack):
    b, t, f2 = u.shape
    f = f2 // 2
    tf = f // 2
    assert tf % LANES == 0
    nf = f // tf
    tt = _row_tile(t, 512)
    hb = tt // SUBLANES

    def halo(i):
        return jnp.maximum(i * hb - 1, 0)

    return pl.pallas_call(
        _conv_gate_kernel,
        out_shape=jax.ShapeDtypeStruct((b, t, f), BF16),
        grid=(b, t // tt, nf),
        in_specs=[
            pl.BlockSpec((1, tt, tf), lambda bi, i, j: (bi, i, j)),
            pl.BlockSpec((1, tt, tf), lambda bi, i, j: (bi, i, nf + j)),
            pl.BlockSpec((1, SUBLANES, tf), lambda bi, i, j: (bi, halo(i), j)),
            pl.BlockSpec((1, SUBLANES, tf), lambda bi, i, j: (bi, halo(i), nf + j)),
            pl.BlockSpec((1, SUBLANES, tf), lambda bi, i, j: (bi, 0, j)),
            pl.BlockSpec((1, SUBLANES, tf), lambda bi, i, j: (bi, 0, nf + j)),
            pl.BlockSpec((SUBLANES, tf), lambda bi, i, j: (0, j)),
            pl.BlockSpec((SUBLANES, tf), lambda bi, i, j: (0, nf + j)),
        ],
        out_specs=pl.BlockSpec((1, tt, tf), lambda bi, i, j: (bi, i, j)),
        compiler_params=_cparams(3),
        name="conv_gate",
    )(u, u, u, u, state8, state8, wpack, wpack)


def _cross_kernel(x_ref, g_ref, wq_ref, mk_ref, mv_ref, wo_ref, o_ref, *, heads, hdim):
    x = x_ref[0]
    h = x * lax.rsqrt(jnp.mean(x * x, axis=-1, keepdims=True) + NORM_EPS) * g_ref[...]
    q = _dot(h.astype(BF16), wq_ref[...]) * (hdim ** -0.5)
    outs = []
    for hh in range(heads):
        sl = slice(hh * hdim, (hh + 1) * hdim)
        s = _nt_dot(q[:, sl].astype(BF16), mk_ref[0, :, sl].astype(BF16))
        p = jnp.exp(s - jnp.max(s, axis=-1, keepdims=True))
        p = p / jnp.sum(p, axis=-1, keepdims=True)
        outs.append(_dot(p.astype(BF16), mv_ref[0, :, sl].astype(BF16)))
    o = jnp.concatenate(outs, axis=-1)
    o_ref[0] = x + _dot(o.astype(BF16), wo_ref[...])


def cross_block(x, gain, wq, mk, mv, wo, *, heads=4, hdim=128):
    b, t, d = x.shape
    m = mk.shape[1]
    e = heads * hdim
    tt = _row_tile(t, 512)
    return pl.pallas_call(
        functools.partial(_cross_kernel, heads=heads, hdim=hdim),
        out_shape=jax.ShapeDtypeStruct((b, t, d), F32),
        grid=(b, t // tt),
        in_specs=[
            pl.BlockSpec((1, tt, d), lambda bi, i: (bi, i, 0)),
            pl.BlockSpec((1, d), lambda bi, i: (0, 0)),
            pl.BlockSpec((d, e), lambda bi, i: (0, 0)),
            pl.BlockSpec((1, m, e), lambda bi, i: (bi, 0, 0)),
            pl.BlockSpec((1, m, e), lambda bi, i: (bi, 0, 0)),
            pl.BlockSpec((e, d), lambda bi, i: (0, 0)),
        ],
        out_specs=pl.BlockSpec((1, tt, d), lambda bi, i: (bi, i, 0)),
        compiler_params=_cparams(2),
        name="cross_block",
    )(x, gain.reshape(1, d), wq, mk, mv, wo)


def rope_tables(pos):
    half = HEAD_DIM // 2
    inv = ROPE_THETA ** (-jnp.arange(half, dtype=F32) / half)
    ang = pos.astype(F32)[:, None] * inv[None, :]
    cos = jnp.cos(ang)
    sin = jnp.sin(ang)
    cos_t = jnp.tile(jnp.concatenate([cos, cos], axis=-1), (1, GROUP))
    sin_t = jnp.tile(jnp.concatenate([-sin, sin], axis=-1), (1, GROUP))
    return cos_t, sin_t


def _rope_apply(x, cos_t, sin_t):
    half = HEAD_DIM // 2
    lane = lax.broadcasted_iota(jnp.int32, x.shape, 1)
    first = (lane % HEAD_DIM) < half
    n = x.shape[1]
    swapped = jnp.where(first, pltpu.roll(x, n - half, 1), pltpu.roll(x, half, 1))
    return x * cos_t + swapped * sin_t


def _rope_kernel(x_ref, c_ref, s_ref, o_ref):
    o_ref[0] = _rope_apply(x_ref[0], c_ref[...], s_ref[...])


def rope_cols(x, col0, width, cos_t, sin_t):
    b, t, _ = x.shape
    assert col0 % GLANES == 0 and width % GLANES == 0
    tt = _row_tile(t, 512)
    c0 = col0 // GLANES
    return pl.pallas_call(
        _rope_kernel,
        out_shape=jax.ShapeDtypeStruct((b, t, width), F32),
        grid=(b, t // tt, width // GLANES),
        in_specs=[
            pl.BlockSpec((1, tt, GLANES), lambda bi, i, j: (bi, i, c0 + j)),
            pl.BlockSpec((tt, GLANES), lambda bi, i, j: (i, 0)),
            pl.BlockSpec((tt, GLANES), lambda bi, i, j: (i, 0)),
        ],
        out_specs=pl.BlockSpec((1, tt, GLANES), lambda bi, i, j: (bi, i, j)),
        compiler_params=_cparams(3),
        name="rope",
    )(x, cos_t, sin_t)


def _logf_cumsum_kernel(g_ref, b_ref, lf_ref, c_ref, cp_ref, *, chunk):
    t = g_ref.shape[1]
    row = lax.broadcasted_iota(jnp.int32, (chunk, chunk), 0)
    col = lax.broadcasted_iota(jnp.int32, (chunk, chunk), 1)
    tri = jnp.where(col <= row, 1.0, 0.0).astype(BF16)
    head_lane = lax.broadcasted_iota(jnp.int32, (chunk, g_ref.shape[2]), 1) < FORGET_HEADS

    def body(i, carry):
        sl = pl.ds(pl.multiple_of(i * chunk, chunk), chunk)
        lf = _log_sigmoid(g_ref[0, sl, :] + b_ref[...])
        lf_ref[0, sl, :] = lf
        c = _dot_exact_lhs01(tri, lf) + carry
        c_ref[0, sl, :] = c
        hi, mid, lo = _split3(jnp.where(head_lane, c, 0.0))
        placed = (hi.astype(F32) + pltpu.roll(mid.astype(F32), FORGET_HEADS, 1)
                  + pltpu.roll(lo.astype(F32), 2 * FORGET_HEADS, 1))
        cp_ref[0, sl, :] = placed.astype(BF16)
        return c[chunk - 1:chunk, :]

    lax.fori_loop(0, t // chunk, body, jnp.zeros((1, g_ref.shape[2]), F32))


def logf_cumsum(gate, bias):
    b, t, n = gate.shape
    chunk = _row_tile(t, 256)
    spec = pl.BlockSpec((1, t, n), lambda bi: (bi, 0, 0))
    return pl.pallas_call(
        functools.partial(_logf_cumsum_kernel, chunk=chunk),
        out_shape=(jax.ShapeDtypeStruct((b, t, n), F32), jax.ShapeDtypeStruct((b, t, n), F32),
                   jax.ShapeDtypeStruct((b, t, n), BF16)),
        grid=(b,),
        in_specs=[spec, pl.BlockSpec((1, n), lambda bi: (0, 0))],
        out_specs=(spec, spec, spec),
        compiler_params=_cparams(1),
        name="logf_cumsum",
    )(gate, bias.reshape(1, n))


CMP_STRIDE = 16
CMP_BLK = 32
SEL_BLK = 64
N_SELECT = 16
WINDOW = 512
SEL_FORCE = 1e4


def compress_rows(rows, pe, w1, b1, w2):
    b, g, tp, d = rows.shape
    return compress_chunks(rows.reshape(b, g, tp // CMP_STRIDE, CMP_STRIDE * d), pe, w1, b1, w2)


def compress_chunks(chunks, pe, w1, b1, w2):
    b, g, n, _ = chunks.shape
    d = HEAD_DIM
    hid = w1.shape[-1]
    a = chunks.reshape(b * g * n, CMP_STRIDE * d)
    w1f = w1.reshape(CMP_BLK * d, hid)
    w1cat = jnp.concatenate([w1f[:CMP_STRIDE * d], w1f[CMP_STRIDE * d:]], axis=1).astype(BF16)
    h = matmul(a, w1cat).reshape(b, g, n, 2 * hid)
    h_first = h[..., :hid]
    h_second = jnp.concatenate([h[:, :, 1:, hid:], jnp.zeros((b, g, 1, hid), F32)], axis=2)
    bias = b1 + jnp.einsum("k,kh->h", pe.reshape(-1), w1f, precision=lax.Precision.HIGHEST)
    w2p = jnp.pad(w2, ((0, 0), (0, LANES - d))).astype(BF16)
    out = matmul(h_first.reshape(b * g * n, hid), w2p, add=h_second.reshape(b * g * n, hid), pre_bias=bias)
    return out[:, :d].reshape(b, g, n, d)


KEY_TILES = 2


def _masked_q_t(q_t):
    row = lax.broadcasted_iota(jnp.int32, q_t.shape, 0) // HEAD_DIM
    return jnp.concatenate([jnp.where(row == r, q_t, 0.0).astype(BF16) for r in range(GROUP)], axis=1)


def _causal_mask_t(tk, tq, k0, q0, strict):
    key = lax.broadcasted_iota(jnp.int32, (tk, GROUP * tq), 0) + k0
    qry = lax.broadcasted_iota(jnp.int32, (tk, GROUP * tq), 1) % tq + q0
    return key < qry if strict else key <= qry


def _softmax_step_t(s, mask, m, l):
    if mask is not None:
        s = jnp.where(mask, s, MASK_VALUE)
    m_new = jnp.maximum(m, jnp.max(s, axis=0, keepdims=True))
    alpha = jnp.exp(m - m_new)
    p = jnp.exp(s - m_new)
    if mask is not None:
        p = jnp.where(mask, p, 0.0)
    return p, alpha, m_new, alpha * l + jnp.sum(p, axis=0, keepdims=True)


def _pflash_t_kernel(*refs, kind, tq, tk, vrows, out_scale):
    if kind == "fox":
        q_ref, k_ref, vt_ref, cp_ref, o_ref, acc_ref = refs
    else:
        q_ref, k_ref, vt_ref, lam_ref, sub_ref, o_ref, acc_ref = refs
    g = pl.program_id(1)
    qi = pl.program_id(2)
    n = GROUP * tq
    q4t = _masked_q_t((q_ref[0] * (HEAD_DIM ** -0.5)).T)
    if kind == "fox":
        piece_row = lax.broadcasted_iota(jnp.int32, (LANES, n), 0)
        head = lax.broadcasted_iota(jnp.int32, (LANES, n), 1) // tq + g * GROUP
        cind = jnp.where((piece_row < 3 * FORGET_HEADS) & (piece_row % FORGET_HEADS == head), -1.0, 0.0).astype(BF16)
    acc_ref[...] = jnp.zeros(acc_ref.shape, F32)

    def step(kt, masked, m, l):
        spans, scores, masks = [], [], []
        for u in range(KEY_TILES):
            k0 = (kt * KEY_TILES + u) * tk
            ks = pl.ds(pl.multiple_of(k0, tk), tk)
            s = _dot(k_ref[0, ks, :].astype(BF16), q4t)
            if kind == "fox":
                s = s + _dot(cp_ref[0, ks, :], cind)
            mask = _causal_mask_t(tk, tq, k0, qi * tq, False) if masked else None
            if masked:
                s = jnp.where(mask, s, MASK_VALUE)
            spans.append(ks)
            scores.append(s)
            masks.append(mask)
        m_new = m
        for s in scores:
            m_new = jnp.maximum(m_new, jnp.max(s, axis=0, keepdims=True))
        alpha = jnp.exp(m - m_new)
        l = alpha * l
        probs = []
        for s, mask in zip(scores, masks):
            p = jnp.exp(s - m_new)
            if masked:
                p = jnp.where(mask, p, 0.0)
            l = l + jnp.sum(p, axis=0, keepdims=True)
            probs.append(p.astype(BF16))
        for r in range(GROUP):
            v0 = (r * HEAD_DIM // vrows) * vrows
            cols = slice(r * tq, (r + 1) * tq)
            pv = _dot(vt_ref[0, v0:v0 + vrows, spans[0]].astype(BF16), probs[0][:, cols])
            for u in range(1, KEY_TILES):
                pv = pv + _dot(vt_ref[0, v0:v0 + vrows, spans[u]].astype(BF16), probs[u][:, cols])
            acc_ref[r] = alpha[:, cols] * acc_ref[r] + pv
        return m_new, l

    init = (jnp.full((1, n), MASK_VALUE, F32), jnp.zeros((1, n), F32))
    full_steps = qi // KEY_TILES
    m, l = lax.fori_loop(0, full_steps, lambda kt, c: step(kt, False, *c), init)
    m, l = step(full_steps, True, m, l)

    inv = 1.0 / jnp.maximum(l, 1e-30)
    blk = [acc_ref[r] * inv[:, r * tq:(r + 1) * tq] for r in range(GROUP)]
    if kind == "fox":
        o_t = jnp.concatenate(blk, axis=0)
    else:
        lam = lam_ref[0, 0]
        halves = []
        for hh in range(2):
            d = blk[2 * hh] - lam * blk[2 * hh + 1]
            y = d * lax.rsqrt(jnp.mean(d * d, axis=0, keepdims=True) + NORM_EPS) * sub_ref[...]
            halves.append(y * out_scale)
        o_t = jnp.concatenate(halves, axis=0)
    o_ref[0] = o_t.T.astype(o_ref.dtype)


def prompt_flash(kind, qa, qcol, ka, kcol, v_t, *, cp=None, lam=None, subln=None, out_scale=1.0, tile=256):
    b, t, _ = qa.shape
    tq = tk = _row_tile(t, tile)
    assert (t // tk) % KEY_TILES == 0
    vrows = HEAD_DIM if kind == "fox" else LANES
    ins = [qa, ka, v_t]
    specs = [
        pl.BlockSpec((1, tq, GLANES), lambda bi, g, i: (bi, i, qcol + g)),
        pl.BlockSpec((1, t, GLANES), lambda bi, g, i: (bi, 0, kcol + g)),
        pl.BlockSpec((1, GLANES, t), lambda bi, g, i: (bi, g, 0)),
    ]
    if kind == "fox":
        ins.append(cp)
        specs.append(pl.BlockSpec((1, t, LANES), lambda bi, g, i: (bi, 0, 0)))
    else:
        ins += [lam.reshape(1, 1), subln.reshape(LANES, 1)]
        specs += [pl.BlockSpec(memory_space=pltpu.SMEM), pl.BlockSpec((LANES, 1), lambda bi, g, i: (0, 0))]
    return pl.pallas_call(
        functools.partial(_pflash_t_kernel, kind=kind, tq=tq, tk=tk, vrows=vrows, out_scale=out_scale),
        out_shape=jax.ShapeDtypeStruct((b, t, GROUP * GLANES), BF16),
        grid=(b, GROUP, t // tq),
        in_specs=specs,
        out_specs=pl.BlockSpec((1, tq, GLANES), lambda bi, g, i: (bi, i, g)),
        scratch_shapes=[pltpu.VMEM((GROUP, vrows, tq), F32)],
        compiler_params=_cparams(3),
        name="prompt_flash_" + kind,
    )(*ins)


def _psb_t_kernel(q_ref, k_ref, vt_ref, o_ref, acc_ref, *, tq, tk):
    qi = pl.program_id(2)
    n = GROUP * tq
    q4t = _masked_q_t((q_ref[0] * (HEAD_DIM ** -0.5)).T)
    acc_ref[...] = jnp.zeros(acc_ref.shape, F32)
    row = lax.broadcasted_iota(jnp.int32, (tk, tk), 0)
    col = lax.broadcasted_iota(jnp.int32, (tk, tk), 1)
    later = jnp.where(col > row, 1.0, 0.0).astype(BF16)

    def step(kt, masked, carry):
        weights = []
        for u in reversed(range(KEY_TILES)):
            k0 = (kt * KEY_TILES + u) * tk
            ks = pl.ds(pl.multiple_of(k0, tk), tk)
            z = _dot(k_ref[0, ks, :].astype(BF16), q4t)
            sp = _softplus(z)
            log_rest = -sp
            mask = _causal_mask_t(tk, tq, k0, qi * tq, True) if masked else None
            if masked:
                log_rest = jnp.where(mask, log_rest, 0.0)
            hi, lo = _split2(log_rest)
            after = carry + _dot(later, hi) + _dot(later, lo)
            a = jnp.exp(z - sp + after)
            if masked:
                a = jnp.where(mask, a, 0.0)
            weights.append((ks, a.astype(BF16)))
            carry = carry + jnp.sum(log_rest, axis=0, keepdims=True)
        for r in range(GROUP):
            cols = slice(r * tq, (r + 1) * tq)
            rows = slice(r * HEAD_DIM, (r + 1) * HEAD_DIM)
            pv = _dot(vt_ref[0, rows, weights[0][0]].astype(BF16), weights[0][1][:, cols])
            for ks, ab in weights[1:]:
                pv = pv + _dot(vt_ref[0, rows, ks].astype(BF16), ab[:, cols])
            acc_ref[r] += pv
        return carry

    full_steps = qi // KEY_TILES
    carry = step(full_steps, True, jnp.zeros((1, n), F32))
    lax.fori_loop(0, full_steps, lambda i, c: step(full_steps - 1 - i, False, c), carry)
    o_ref[0] = jnp.concatenate([acc_ref[r] for r in range(GROUP)], axis=0).T.astype(o_ref.dtype)


def prompt_sb(proj, v_t, *, tile=256):
    b, t, _ = proj.shape
    tq = tk = _row_tile(t, tile)
    assert (t // tk) % KEY_TILES == 0
    return pl.pallas_call(
        functools.partial(_psb_t_kernel, tq=tq, tk=tk),
        out_shape=jax.ShapeDtypeStruct((b, t, GROUP * GLANES), BF16),
        grid=(b, GROUP, t // tq),
        in_specs=[
            pl.BlockSpec((1, tq, GLANES), lambda bi, g, i: (bi, i, g)),
            pl.BlockSpec((1, t, GLANES), lambda bi, g, i: (bi, 0, GROUP + g)),
            pl.BlockSpec((1, GLANES, t), lambda bi, g, i: (bi, g, 0)),
        ],
        out_specs=pl.BlockSpec((1, tq, GLANES), lambda bi, g, i: (bi, i, g)),
        scratch_shapes=[pltpu.VMEM((GROUP, HEAD_DIM, tq), F32)],
        compiler_params=_cparams(3),
        name="prompt_sb",
    )(proj, proj, v_t)


def _group_q_t(q):
    q_t = q.T
    return jnp.concatenate([q_t[r * HEAD_DIM:(r + 1) * HEAD_DIM, :] for r in range(GROUP)], axis=1).astype(BF16)


def _select_blocks_t(p_sum_t, q_pos, n_sel, n_top):
    nc, t = p_sum_t.shape
    nb = -(-n_sel // LANES) * LANES
    j_idx = lax.broadcasted_iota(jnp.int32, (nb, nc), 0)
    c_idx = lax.broadcasted_iota(jnp.int32, (nb, nc), 1) * CMP_STRIDE
    overlap = (c_idx < j_idx * SEL_BLK + SEL_BLK) & (c_idx + CMP_BLK > j_idx * SEL_BLK) & (j_idx < n_sel)
    imp = _dot_exact_lhs01(jnp.where(overlap, 1.0, 0.0).astype(BF16), p_sum_t)
    rows = -(-n_sel // SUBLANES) * SUBLANES
    imp = imp[:rows]
    blk = lax.broadcasted_iota(jnp.int32, (rows, t), 0)
    cur = q_pos // SEL_BLK
    forced = (blk == 0) | (blk == cur) | (blk == cur - 1)
    valid = blk * SEL_BLK <= q_pos
    score = jnp.where(forced, SEL_FORCE, jnp.where(valid, imp, -1.0))
    score = jnp.where(blk < n_sel, score, -3e38)
    rank = jnp.zeros(score.shape, F32)
    for jp in range(n_sel):
        other = score[jp:jp + 1, :]
        ge = jnp.where(other >= score, 1.0, 0.0)
        gt = jnp.where(other > score, 1.0, 0.0)
        rank = rank + jnp.where(blk > jp, ge, gt)
    sel = jnp.where(rank < n_top, 1.0, 0.0)
    if rows < nb:
        sel = jnp.concatenate([sel, jnp.zeros((nb - rows, t), F32)], axis=0)
    return sel


def _pnsa_t_kernel(qc_ref, qr_ref, gate_ref, kc_ref, vct_ref, ks_ref, vst_ref, kw_ref, vwt_ref, o_ref, acc_ref,
                   *, tq, tk, n_sel):
    qi = pl.program_id(2)
    scale = HEAD_DIM ** -0.5
    q0 = qi * tq
    n = GROUP * tq
    q_pos = lax.broadcasted_iota(jnp.int32, (1, n), 1) % tq + q0

    qc = _group_q_t(qc_ref[0] * scale)
    nc = kc_ref.shape[2]
    s = _dot(kc_ref[0, 0].astype(BF16), qc)
    cmp_end = lax.broadcasted_iota(jnp.int32, (nc, n), 0) * CMP_STRIDE + (CMP_BLK - 1)
    mask = cmp_end <= q_pos
    s = jnp.where(mask, s, MASK_VALUE)
    p = jnp.where(mask, jnp.exp(s - jnp.max(s, axis=0, keepdims=True)), 0.0)
    p = p / jnp.maximum(jnp.sum(p, axis=0, keepdims=True), 1e-30)
    o_c = _dot(vct_ref[0, 0].astype(BF16), p.astype(BF16))
    p_sum = p[:, 0:tq] + p[:, tq:2 * tq] + p[:, 2 * tq:3 * tq] + p[:, 3 * tq:4 * tq]
    sel = _select_blocks_t(p_sum, q_pos[:, 0:tq], n_sel, min(N_SELECT, n_sel)).astype(BF16)
    nb = sel.shape[0]

    qr = _group_q_t(qr_ref[0] * scale)
    init = (jnp.full((1, n), MASK_VALUE, F32), jnp.zeros((1, n), F32))
    acc_ref[...] = jnp.zeros(acc_ref.shape, F32)

    def sel_step(kt, c):
        m, l = c
        spans, scores, masks = [], [], []
        for u in range(KEY_TILES):
            k0 = (kt * KEY_TILES + u) * tk
            ks = pl.ds(pl.multiple_of(k0, tk), tk)
            s = _dot(ks_ref[0, 0, ks, :].astype(BF16), qr)
            key = lax.broadcasted_iota(jnp.int32, (tk, nb), 0) + k0
            blk = lax.broadcasted_iota(jnp.int32, (tk, nb), 1)
            chosen = _dot(jnp.where(blk == key // SEL_BLK, 1.0, 0.0).astype(BF16), sel)
            kpos = lax.broadcasted_iota(jnp.int32, (tk, tq), 0) + k0
            m1 = jnp.where(kpos <= q_pos[:, 0:tq], chosen, 0.0)
            mask = jnp.concatenate([m1] * GROUP, axis=1) > 0.5
            spans.append(ks)
            scores.append(jnp.where(mask, s, MASK_VALUE))
            masks.append(mask)
        m_new = m
        for s in scores:
            m_new = jnp.maximum(m_new, jnp.max(s, axis=0, keepdims=True))
        alpha = jnp.exp(m - m_new)
        l = alpha * l
        pv = jnp.zeros(acc_ref.shape, F32)
        for ks, s, mask in zip(spans, scores, masks):
            p = jnp.where(mask, jnp.exp(s - m_new), 0.0)
            l = l + jnp.sum(p, axis=0, keepdims=True)
            pv = pv + _dot(vst_ref[0, :, ks].astype(BF16), p.astype(BF16))
        acc_ref[...] = alpha * acc_ref[...] + pv
        return m_new, l

    m, l = lax.fori_loop(0, qi // KEY_TILES + 1, sel_step, init)
    o_s = acc_ref[...] / jnp.maximum(l, 1e-30)
    acc_ref[...] = jnp.zeros(acc_ref.shape, F32)

    def win_tile(kt, c):
        m, l = c
        ks = pl.ds(pl.multiple_of(kt * tk, tk), tk)
        s = _dot(kw_ref[0, 0, ks, :].astype(BF16), qr)
        dist = q_pos - (lax.broadcasted_iota(jnp.int32, (tk, n), 0) + kt * tk)
        m4 = (dist >= 0) & (dist < WINDOW)
        p, alpha, m, l = _softmax_step_t(s, m4, m, l)
        acc_ref[...] = alpha * acc_ref[...] + _dot(vwt_ref[0, :, ks].astype(BF16), p.astype(BF16))
        return m, l

    kt_lo = jnp.maximum(q0 - (WINDOW - 1), 0) // tk
    m, l = lax.fori_loop(kt_lo, qi + 1, win_tile, init)
    o_w = acc_ref[...] / jnp.maximum(l, 1e-30)

    gates = jax.nn.sigmoid(gate_ref[0, 0])
    outs = []
    for r in range(GROUP):
        cols = slice(r * tq, (r + 1) * tq)
        outs.append(gates[r:r + 1, :] * o_c[:, cols] + gates[GROUP + r:GROUP + r + 1, :] * o_s[:, cols]
                    + gates[2 * GROUP + r:2 * GROUP + r + 1, :] * o_w[:, cols])
    o_ref[0] = jnp.concatenate(outs, axis=0).T.astype(o_ref.dtype)


def prompt_nsa(proj, q_rot, gate_t, kcmp, vcmp_t, ksel, vsel_t, kwin, vwin_t, *, tile=256):
    b, t, _ = proj.shape
    tq = tk = _row_tile(t, tile)
    assert (t // tk) % KEY_TILES == 0
    nc = kcmp.shape[2]
    n_sel = t // SEL_BLK
    assert t % SEL_BLK == 0 and nc % LANES == 0
    k_spec = pl.BlockSpec((1, 1, t, HEAD_DIM), lambda bi, g, i: (bi, g, 0, 0))
    vt_spec = pl.BlockSpec((1, HEAD_DIM, t), lambda bi, g, i: (bi, g, 0))
    return pl.pallas_call(
        functools.partial(_pnsa_t_kernel, tq=tq, tk=tk, n_sel=n_sel),
        out_shape=jax.ShapeDtypeStruct((b, t, GROUP * GLANES), BF16),
        grid=(b, GROUP, t // tq),
        in_specs=[
            pl.BlockSpec((1, tq, GLANES), lambda bi, g, i: (bi, i, g)),
            pl.BlockSpec((1, tq, GLANES), lambda bi, g, i: (bi, i, g)),
            pl.BlockSpec((1, 1, 4 * GROUP, tq), lambda bi, g, i: (bi, g, 0, i)),
            pl.BlockSpec((1, 1, nc, HEAD_DIM), lambda bi, g, i: (bi, g, 0, 0)),
            pl.BlockSpec((1, 1, HEAD_DIM, nc), lambda bi, g, i: (bi, g, 0, 0)),
            k_spec, vt_spec, k_spec, vt_spec,
        ],
        out_specs=pl.BlockSpec((1, tq, GLANES), lambda bi, g, i: (bi, i, g)),
        scratch_shapes=[pltpu.VMEM((HEAD_DIM, GROUP * tq), F32)],
        compiler_params=_cparams(3),
        name="prompt_nsa",
    )(proj, q_rot, gate_t, kcmp, vcmp_t, ksel, vsel_t, kwin, vwin_t)


NEW_PAD = 16
STEP_PAGES = 8


def _pad_rows(x, n):
    return jnp.pad(x, ((0, 0), (0, n - x.shape[1]), (0, 0)))


def _rows_repeat(x, reps):
    return jnp.concatenate([jnp.broadcast_to(x[i:i + 1], (reps, x.shape[1])) for i in range(x.shape[0])], axis=0)


def _strict_lower_ones(n):
    row = lax.broadcasted_iota(jnp.int32, (n, n), 0)
    col = lax.broadcasted_iota(jnp.int32, (n, n), 1)
    return jnp.where(row > col, 1.0, 0.0).astype(BF16)


def _sattn_native_kernel(*refs, kind, n_new, heads, v_heads, out_scale, step_pages, page, values_transposed):
    it = iter(refs)
    pt_ref = next(it)
    q_ref, kn_ref, vn_ref = (next(it) for _ in range(3))
    k_refs = [next(it) for _ in range(step_pages)]
    v_refs = [next(it) for _ in range(step_pages)]
    if kind == "fox":
        lf_refs = [next(it) for _ in range(step_pages)]
        nc_ref = next(it)
    if kind == "diff":
        lam_ref, sub_ref = next(it), next(it)
    o_ref = next(it)
    qs_ref = next(it)
    if kind == "sb":
        carry_ref, acc_ref = next(it), next(it)
    else:
        m_ref, l_ref, acc_ref = next(it), next(it), next(it)
    if kind == "fox":
        later_ref = next(it)
    del pt_ref
    p = pl.program_id(1)
    rows = n_new * heads
    vdim = acc_ref.shape[1]
    per_v = heads // v_heads

    def head_scores(key_of_head, keys_transposed):
        dot = _dot if keys_transposed else _nt_dot
        return jnp.concatenate([dot(qs_ref[h].astype(BF16), key_of_head(h)) for h in range(heads)], axis=0)

    def weighted_values(pr, c0, width, value_of_head, values_transposed):
        dot = _nt_dot if values_transposed else _dot
        outs = []
        for vh in range(v_heads):
            r0 = vh * per_v * n_new
            outs.append(dot(pr[r0:r0 + per_v * n_new, c0:c0 + width].astype(BF16), value_of_head(vh)))
        return jnp.concatenate(outs, axis=0)

    @pl.when(p == 0)
    def _():
        q = q_ref[0] * (HEAD_DIM ** -0.5)
        for h in range(heads):
            qs_ref[h] = q[:, h * HEAD_DIM:(h + 1) * HEAD_DIM]
        acc_ref[...] = jnp.zeros(acc_ref.shape, F32)
        kn = kn_ref[0].astype(BF16)
        vn = vn_ref[0].astype(BF16)
        s = head_scores(lambda h: kn[:, h * HEAD_DIM:(h + 1) * HEAD_DIM], False)
        t_idx = lax.broadcasted_iota(jnp.int32, (rows, NEW_PAD), 0) % n_new
        s_idx = lax.broadcasted_iota(jnp.int32, (rows, NEW_PAD), 1)
        new_value = lambda vh: vn[:, vh * vdim:(vh + 1) * vdim]
        if kind == "sb":
            mask = s_idx < t_idx
            sp = _softplus(s)
            log_rest = jnp.where(mask, -sp, 0.0)
            hi, lo = _split2(log_rest)
            tri = _strict_lower_ones(NEW_PAD)
            after = _dot(hi, tri) + _dot(lo, tri)
            a = jnp.where(mask, jnp.exp(s - sp + after), 0.0)
            acc_ref[...] = weighted_values(a, 0, NEW_PAD, new_value, False)
            carry_ref[...] = after[:, 0:1] + log_rest[:, 0:1]
        else:
            if kind == "fox":
                later_ref[...] = jnp.zeros(later_ref.shape, F32)
                s = s - _rows_repeat(nc_ref[0], n_new)
            mask = s_idx <= t_idx
            s = jnp.where(mask, s, MASK_VALUE)
            m_new = jnp.max(s, axis=1, keepdims=True)
            pr = jnp.where(mask, jnp.exp(s - m_new), 0.0)
            m_ref[...] = m_new
            l_ref[...] = jnp.sum(pr, axis=1, keepdims=True)
            acc_ref[...] = weighted_values(pr, 0, NEW_PAD, new_value, False)

    def page_keys(j):
        return lambda h: k_refs[j][0, h].astype(BF16)

    def page_values(j):
        if values_transposed:
            return lambda vh: v_refs[j][0, vh].astype(BF16)
        return lambda vh: v_refs[j][0, pl.ds(vh, page, stride=v_heads), :].astype(BF16)

    scores = [head_scores(page_keys(j), True) for j in range(step_pages)]
    tri = _strict_lower_ones(page)
    if kind == "sb":
        carry = carry_ref[...]
        pv = jnp.zeros(acc_ref.shape, F32)
        for j in range(step_pages):
            sp = _softplus(scores[j])
            log_rest = -sp
            hi, lo = _split2(log_rest)
            after = carry + _dot(hi, tri) + _dot(lo, tri)
            pv = pv + weighted_values(jnp.exp(scores[j] - sp + after), 0, page, page_values(j), values_transposed)
            carry = after[:, 0:1] + log_rest[:, 0:1]
        acc_ref[...] += pv
        carry_ref[...] = carry
    else:
        if kind == "fox":
            later = later_ref[...]
            for j in range(step_pages):
                lf = lf_refs[j][0]
                suffix = _dot_exact_rhs01(lf, tri)
                scores[j] = scores[j] + _rows_repeat(suffix + later, n_new)
                later = later + suffix[:, 0:1] + lf[:, 0:1]
            later_ref[...] = later
        s = jnp.concatenate(scores, axis=1)
        m_prev = m_ref[...]
        m_new = jnp.maximum(m_prev, jnp.max(s, axis=1, keepdims=True))
        alpha = jnp.exp(m_prev - m_new)
        pr = jnp.exp(s - m_new)
        l_ref[...] = alpha * l_ref[...] + jnp.sum(pr, axis=1, keepdims=True)
        pv = jnp.zeros(acc_ref.shape, F32)
        for j in range(step_pages):
            pv = pv + weighted_values(pr, j * page, page, page_values(j), values_transposed)
        acc_ref[...] = alpha * acc_ref[...] + pv
        m_ref[...] = m_new

    @pl.when(p == pl.num_programs(1) - 1)
    def _():
        if kind == "sb":
            accn = acc_ref[...]
        else:
            accn = acc_ref[...] / jnp.maximum(l_ref[...], 1e-30)
        blk = lambda h: accn[h * n_new:(h + 1) * n_new, :]
        if kind == "diff":
            lam = lam_ref[0, 0]
            parts = []
            for vh in range(v_heads):
                d = blk(2 * vh) - lam * blk(2 * vh + 1)
                parts.append(d * lax.rsqrt(jnp.mean(d * d, axis=-1, keepdims=True) + NORM_EPS) * sub_ref[...]
                             * out_scale)
            o_ref[0] = jnp.concatenate(parts, axis=1)
        else:
            o_ref[0] = jnp.concatenate([blk(h) for h in range(heads)], axis=1)


def sample_attn(kind, page_table, q, k_new, v_new, k_cache, v_cache, *, lf_cache_t=None, new_cum_t=None,
                lam=None, subln=None, out_scale=1.0):
    b, n_new, e = q.shape
    pool, page, heads, _ = k_cache.shape
    v_heads, vdim = v_cache.shape[2:]
    rows = n_new * heads
    n_pages = page_table.shape[1]
    last = n_pages - 1
    sp = STEP_PAGES
    assert n_pages % sp == 0
    values_transposed = vdim < LANES
    assert values_transposed or v_heads % SUBLANES == 0

    def page_spec(j, *blk):
        zeros = (0,) * len(blk)
        return pl.BlockSpec((1,) + blk, lambda bi, p, pt: (pt[bi, last - (p * sp + j)],) + zeros)

    k_t = k_cache.transpose(0, 2, 3, 1)
    if values_transposed:
        v_in, v_blk = v_cache.transpose(0, 2, 3, 1), (v_heads, vdim, page)
    else:
        v_in, v_blk = v_cache.reshape(pool, page * v_heads, vdim), (page * v_heads, vdim)
    ins = [q, _pad_rows(k_new, NEW_PAD), _pad_rows(v_new, NEW_PAD)] + [k_t] * sp + [v_in] * sp
    specs = ([pl.BlockSpec((1, n_new, e), lambda bi, p, pt: (bi, 0, 0))]
             + [pl.BlockSpec((1, NEW_PAD, e), lambda bi, p, pt: (bi, 0, 0))] * 2
             + [page_spec(j, heads, HEAD_DIM, page) for j in range(sp)]
             + [page_spec(j, *v_blk) for j in range(sp)])
    if kind == "fox":
        ins += [lf_cache_t] * sp + [new_cum_t]
        specs += [page_spec(j, heads, page) for j in range(sp)]
        specs.append(pl.BlockSpec((1, heads, NEW_PAD), lambda bi, p, pt: (bi, 0, 0)))
    if kind == "diff":
        ins += [lam.reshape(1, 1), subln.reshape(1, LANES)]
        specs += [pl.BlockSpec(memory_space=pltpu.SMEM), pl.BlockSpec((1, LANES), lambda bi, p, pt: (0, 0))]
    scratch = [pltpu.VMEM((heads, n_new, HEAD_DIM), F32)]
    if kind == "sb":
        scratch += [pltpu.VMEM((rows, 1), F32), pltpu.VMEM((rows, vdim), F32)]
    else:
        scratch += [pltpu.VMEM((rows, 1), F32), pltpu.VMEM((rows, 1), F32), pltpu.VMEM((rows, vdim), F32)]
    if kind == "fox":
        scratch.append(pltpu.VMEM((heads, 1), F32))
    return pl.pallas_call(
        functools.partial(_sattn_native_kernel, kind=kind, n_new=n_new, heads=heads, v_heads=v_heads,
                          out_scale=out_scale, step_pages=sp, page=page, values_transposed=values_transposed),
        out_shape=jax.ShapeDtypeStruct((b, n_new, e), F32),
        grid_spec=pltpu.PrefetchScalarGridSpec(
            num_scalar_prefetch=1,
            grid=(b, n_pages // sp),
            in_specs=specs,
            out_specs=pl.BlockSpec((1, n_new, e), lambda bi, p, pt: (bi, 0, 0)),
            scratch_shapes=scratch,
        ),
        compiler_params=_cparams(2),
        name="sample_attn_" + kind,
    )(page_table, *ins)


GATHER_PAGES = 8


def _softmax_tile(s, mask, v, m_ref, l_ref, acc_ref):
    if mask is not None:
        s = jnp.where(mask, s, MASK_VALUE)
    m_prev = m_ref[...]
    m_new = jnp.maximum(m_prev, jnp.max(s, axis=1, keepdims=True))
    alpha = jnp.exp(m_prev - m_new)
    p = jnp.exp(s - m_new)
    if mask is not None:
        p = jnp.where(mask, p, 0.0)
    l_ref[...] = alpha * l_ref[...] + jnp.sum(p, axis=1, keepdims=True)
    acc_ref[...] = alpha * acc_ref[...] + _dot(p.astype(BF16), v)
    m_ref[...] = m_new


def _finish_softmax(l_ref, acc_ref):
    return acc_ref[...] / jnp.maximum(l_ref[...], 1e-30)


def _reset_softmax(m_ref, l_ref, acc_ref):
    m_ref[...] = jnp.full(m_ref.shape, MASK_VALUE, F32)
    l_ref[...] = jnp.zeros(l_ref.shape, F32)
    acc_ref[...] = jnp.zeros(acc_ref.shape, F32)


def _select_blocks(p_sum, q_pos, n_sel, n_top):
    t, nc = p_sum.shape
    nb = -(-n_sel // LANES) * LANES
    c_idx = lax.broadcasted_iota(jnp.int32, (nc, nb), 0) * CMP_STRIDE
    j_idx = lax.broadcasted_iota(jnp.int32, (nc, nb), 1)
    overlap = (c_idx < j_idx * SEL_BLK + SEL_BLK) & (c_idx + CMP_BLK > j_idx * SEL_BLK) & (j_idx < n_sel)
    imp = _dot_exact_rhs01(p_sum, jnp.where(overlap, 1.0, 0.0).astype(BF16))
    blk = lax.broadcasted_iota(jnp.int32, (t, nb), 1)
    cur = q_pos // SEL_BLK
    forced = (blk == 0) | (blk == cur) | (blk == cur - 1)
    valid = blk * SEL_BLK <= q_pos
    score = jnp.where(forced, SEL_FORCE, jnp.where(valid, imp, -1.0))
    score = jnp.where(blk < n_sel, score, -3e38)
    rows = -(-n_sel // SUBLANES) * SUBLANES
    sc_t = score.T[:rows, :]
    j_iota = lax.broadcasted_iota(jnp.int32, sc_t.shape, 0)
    rank = jnp.zeros(sc_t.shape, F32)
    for jp in range(n_sel):
        other = sc_t[jp:jp + 1, :]
        ge = jnp.where(other >= sc_t, 1.0, 0.0)
        gt = jnp.where(other > sc_t, 1.0, 0.0)
        rank = rank + jnp.where(j_iota > jp, ge, gt)
    sel_t = jnp.where(rank < n_top, 1.0, 0.0)
    if rows < nb:
        sel_t = jnp.concatenate([sel_t, jnp.zeros((nb - rows, t), F32)], axis=0)
    return sel_t.T


def _expand_block_mask(sel, k0, tk):
    nb = sel.shape[1]
    j = lax.broadcasted_iota(jnp.int32, (nb, tk), 0)
    key = lax.broadcasted_iota(jnp.int32, (nb, tk), 1) + k0
    e = jnp.where(j == key // SEL_BLK, 1.0, 0.0).astype(BF16)
    return _dot(sel.astype(BF16), e)


def _gather_groups_kernel(pt_ref, *refs):
    del pt_ref
    page_refs, new_ref, o_ref, rows_ref = refs[:GATHER_PAGES], refs[GATHER_PAGES], refs[GATHER_PAGES + 1], refs[-1]
    page = new_ref.shape[1]
    per_page = page // CMP_STRIDE
    is_new = pl.program_id(1) == pl.num_programs(1) - 1

    def emit_chunks(slot, g, row0):
        for r in range(CMP_STRIDE):
            o_ref[0, g, row0:row0 + per_page, r * HEAD_DIM:(r + 1) * HEAD_DIM] = (
                rows_ref[slot, pl.ds(r, per_page, stride=CMP_STRIDE), :])

    @pl.when(is_new)
    def _():
        o_ref[...] = jnp.zeros(o_ref.shape, F32)
        for g in range(GROUP):
            rows_ref[g] = new_ref[0, :, g * HEAD_DIM:(g + 1) * HEAD_DIM]
            emit_chunks(g, g, 0)

    @pl.when(jnp.logical_not(is_new))
    def _():
        for j in range(GATHER_PAGES):
            for g in range(GROUP):
                rows_ref[j * GROUP + g] = page_refs[j][0, g].T
                emit_chunks(j * GROUP + g, g, j * per_page)


def gather_groups(cache_t, page_table, new_rows):
    b, n_pages = page_table.shape
    page = cache_t.shape[3]
    gp = GATHER_PAGES
    assert n_pages % gp == 0 and page % CMP_STRIDE == 0
    steps = n_pages // gp
    per_step = gp * page // CMP_STRIDE
    new_pad = _pad_rows(new_rows, page)

    def page_spec(j):
        return pl.BlockSpec((1, GROUP, HEAD_DIM, page),
                            lambda bi, p, pt: (pt[bi, jnp.minimum(p, steps - 1) * gp + j], 0, 0, 0))

    return pl.pallas_call(
        _gather_groups_kernel,
        out_shape=jax.ShapeDtypeStruct((b, GROUP, (steps + 1) * per_step, CMP_STRIDE * HEAD_DIM), F32),
        grid_spec=pltpu.PrefetchScalarGridSpec(
            num_scalar_prefetch=1,
            grid=(b, steps + 1),
            in_specs=[page_spec(j) for j in range(gp)]
            + [pl.BlockSpec((1, page, GLANES), lambda bi, p, pt: (bi, 0, 0))],
            out_specs=pl.BlockSpec((1, GROUP, per_step, CMP_STRIDE * HEAD_DIM), lambda bi, p, pt: (bi, 0, p, 0)),
            scratch_shapes=[pltpu.VMEM((gp * GROUP, page, HEAD_DIM), F32)],
        ),
        compiler_params=_cparams(2),
        name="gather_groups",
    )(page_table, *([cache_t] * gp), new_pad)


def _snsa_kernel(*refs, n_new, pos0, n_sel, step_pages):
    (pt_ref, qc_ref, qr_ref, gate_ref, kc_ref, vc_ref, kwb_ref, vwb_ref, ksn_ref, vsn_ref, kwn_ref,
     vwn_ref) = refs[:12]
    ks_refs = refs[12:12 + step_pages]
    vs_refs = refs[12 + step_pages:12 + 2 * step_pages]
    o_ref, qs_ref, qg_ref, sel_ref, oc_ref, ow_ref, m_ref, l_ref, acc_ref, accs_ref = refs[12 + 2 * step_pages:]
    del pt_ref
    p = pl.program_id(1)
    heads = GROUP * GROUP
    rows = n_new * heads
    grows = GROUP * n_new
    scale = HEAD_DIM ** -0.5
    page = ks_refs[0].shape[3]

    def own_lanes(x):
        return jnp.concatenate([x[g * grows:(g + 1) * grows, g * HEAD_DIM:(g + 1) * HEAD_DIM]
                                for g in range(GROUP)], axis=0)

    def group_scores(key_of_group, keys_transposed):
        dot = _dot if keys_transposed else _nt_dot
        return jnp.concatenate([dot(qg_ref[g].astype(BF16), key_of_group(g)) for g in range(GROUP)], axis=0)

    def group_values(pr, c0, width, value_of_group, values_transposed):
        dot = _nt_dot if values_transposed else _dot
        return jnp.concatenate([dot(pr[g * grows:(g + 1) * grows, c0:c0 + width].astype(BF16), value_of_group(g))
                                for g in range(GROUP)], axis=0)

    @pl.when(p == 0)
    def _():
        t_idx = lax.broadcasted_iota(jnp.int32, (rows, 1), 0) % n_new
        q_pos = t_idx + pos0
        nc = kc_ref.shape[1]
        s = _nt_dot((qc_ref[0] * scale).astype(BF16), kc_ref[0].astype(BF16))
        cmp_end = lax.broadcasted_iota(jnp.int32, (rows, nc), 1) * CMP_STRIDE + (CMP_BLK - 1)
        mask = cmp_end <= q_pos
        s = jnp.where(mask, s, MASK_VALUE)
        pc = jnp.where(mask, jnp.exp(s - jnp.max(s, axis=1, keepdims=True)), 0.0)
        pc = pc / jnp.maximum(jnp.sum(pc, axis=1, keepdims=True), 1e-30)
        oc_ref[...] = own_lanes(_dot(pc.astype(BF16), vc_ref[0].astype(BF16)))
        i_idx = lax.broadcasted_iota(jnp.int32, (rows, rows), 0)
        r_idx = lax.broadcasted_iota(jnp.int32, (rows, rows), 1)
        same = (r_idx // grows == i_idx // n_new) & (r_idx % n_new == i_idx % n_new)
        group_sum = jnp.where(same & (i_idx < grows), 1.0, 0.0).astype(BF16)
        p_sum = _dot_exact_lhs01(group_sum, pc)
        g_pos = lax.broadcasted_iota(jnp.int32, (rows, 1), 0) % n_new + pos0
        sel = _select_blocks(p_sum, g_pos, n_sel, min(N_SELECT, n_sel))
        spread = jnp.where(r_idx == (i_idx // grows) * n_new + i_idx % n_new, 1.0, 0.0).astype(BF16)
        sel_ref[...] = _dot(spread, sel.astype(BF16)).astype(BF16)
        qr = qr_ref[0] * scale
        qs_ref[...] = qr.astype(BF16)
        for g in range(GROUP):
            qg_ref[g] = qr[g * grows:(g + 1) * grows, g * HEAD_DIM:(g + 1) * HEAD_DIM]
        _reset_softmax(m_ref, l_ref, acc_ref)
        wlen = kwb_ref.shape[1]
        s = _nt_dot(qs_ref[...], kwb_ref[0].astype(BF16))
        dist = t_idx + wlen - lax.broadcasted_iota(jnp.int32, (rows, wlen), 1)
        _softmax_tile(s, (dist >= 0) & (dist < WINDOW), vwb_ref[0].astype(BF16), m_ref, l_ref, acc_ref)
        s = _nt_dot(qs_ref[...], kwn_ref[0].astype(BF16))
        s_idx = lax.broadcasted_iota(jnp.int32, (rows, NEW_PAD), 1)
        _softmax_tile(s, s_idx <= t_idx, vwn_ref[0].astype(BF16), m_ref, l_ref, acc_ref)
        ow_ref[...] = own_lanes(_finish_softmax(l_ref, acc_ref))
        ksn = ksn_ref[0].astype(BF16)
        vsn = vsn_ref[0].astype(BF16)
        s = group_scores(lambda g: ksn[:, g * HEAD_DIM:(g + 1) * HEAD_DIM], False)
        chosen = _expand_block_mask(sel_ref[...], pos0, NEW_PAD)
        mask = jnp.where(s_idx <= t_idx, chosen, 0.0) > 0.5
        s = jnp.where(mask, s, MASK_VALUE)
        m_new = jnp.max(s, axis=1, keepdims=True)
        pr = jnp.where(mask, jnp.exp(s - m_new), 0.0)
        m_ref[...] = m_new
        l_ref[...] = jnp.sum(pr, axis=1, keepdims=True)
        accs_ref[...] = group_values(pr, 0, NEW_PAD, lambda g: vsn[:, g * HEAD_DIM:(g + 1) * HEAD_DIM], False)

    scores = [group_scores(lambda g, j=j: ks_refs[j][0, g].astype(BF16), True) for j in range(step_pages)]
    chosen = [_expand_block_mask(sel_ref[...], (p * step_pages + j) * page, page) for j in range(step_pages)]
    mask = jnp.concatenate(chosen, axis=1) > 0.5
    s = jnp.where(mask, jnp.concatenate(scores, axis=1), MASK_VALUE)
    m_prev = m_ref[...]
    m_new = jnp.maximum(m_prev, jnp.max(s, axis=1, keepdims=True))
    alpha = jnp.exp(m_prev - m_new)
    pr = jnp.where(mask, jnp.exp(s - m_new), 0.0)
    l_ref[...] = alpha * l_ref[...] + jnp.sum(pr, axis=1, keepdims=True)
    pv = jnp.zeros(accs_ref.shape, F32)
    for j in range(step_pages):
        pv = pv + group_values(pr, j * page, page, lambda g, j=j: vs_refs[j][0, g].astype(BF16), True)
    accs_ref[...] = alpha * accs_ref[...] + pv
    m_ref[...] = m_new

    @pl.when(p == pl.num_programs(1) - 1)
    def _():
        o_s = _finish_softmax(l_ref, accs_ref)
        gates = jax.nn.sigmoid(gate_ref[0])
        o_ref[0] = gates[:, 0:1] * oc_ref[...] + gates[:, 1:2] * o_s + gates[:, 2:3] * ow_ref[...]


def _group_rows(q):
    b, t, _ = q.shape
    qh = q.reshape(b, t, GROUP, GROUP, HEAD_DIM)
    eye = jnp.eye(GROUP, dtype=q.dtype)
    return jnp.einsum("btgrd,gk->bgrtkd", qh, eye).reshape(b, t * GROUP * GROUP, GLANES)


def sample_nsa(page_table, pos0, qc, qr, gate_rows, kcmp, vcmp, kwin_buf, vwin_buf, ks_new, vs_new, kw_new,
               vw_new, ks_cache, vs_cache):
    b, rows, _ = qc.shape
    n_pages = page_table.shape[1]
    page = ks_cache.shape[3]
    n_new = rows // (GROUP * GROUP)
    assert pos0 == n_pages * page and pos0 % SEL_BLK == 0 and n_new <= SEL_BLK
    n_sel = -(-(pos0 + n_new) // SEL_BLK)
    nb = -(-n_sel // LANES) * LANES
    nc = kcmp.shape[1]
    wlen = kwin_buf.shape[1]
    per_b = lambda r, c: pl.BlockSpec((1, r, c), lambda bi, p, pt: (bi, 0, 0))
    sp = STEP_PAGES
    assert n_pages % sp == 0

    def page_spec(j):
        return pl.BlockSpec((1, GROUP, HEAD_DIM, page), lambda bi, p, pt: (pt[bi, p * sp + j], 0, 0, 0))

    return pl.pallas_call(
        functools.partial(_snsa_kernel, n_new=n_new, pos0=pos0, n_sel=n_sel, step_pages=sp),
        out_shape=jax.ShapeDtypeStruct((b, rows, HEAD_DIM), F32),
        grid_spec=pltpu.PrefetchScalarGridSpec(
            num_scalar_prefetch=1,
            grid=(b, n_pages // sp),
            in_specs=[per_b(rows, GLANES), per_b(rows, GLANES), per_b(rows, SUBLANES), per_b(nc, GLANES),
                      per_b(nc, GLANES), per_b(wlen, GLANES), per_b(wlen, GLANES)]
            + [per_b(NEW_PAD, GLANES)] * 4 + [page_spec(j) for j in range(sp)] * 2,
            out_specs=per_b(rows, HEAD_DIM),
            scratch_shapes=[
                pltpu.VMEM((rows, GLANES), BF16),
                pltpu.VMEM((GROUP, rows // GROUP, HEAD_DIM), F32),
                pltpu.VMEM((rows, nb), BF16),
                pltpu.VMEM((rows, HEAD_DIM), F32),
                pltpu.VMEM((rows, HEAD_DIM), F32),
                pltpu.VMEM((rows, 1), F32),
                pltpu.VMEM((rows, 1), F32),
                pltpu.VMEM((rows, GLANES), F32),
                pltpu.VMEM((rows, HEAD_DIM), F32),
            ],
        ),
        compiler_params=_cparams(2),
        name="sample_nsa",
    )(page_table, qc, qr, gate_rows, kcmp, vcmp, kwin_buf, vwin_buf, ks_new, vs_new, kw_new, vw_new,
      *([ks_cache] * sp), *([vs_cache] * sp))


def sample_nsa_mixer(proj, gate, pos0, cos_t, sin_t, cache, page_table, p):
    b, t, _ = proj.shape
    flat = lambda c: c.reshape(c.shape[0], c.shape[1], GLANES)
    q_rot = rope_cols(proj, 0, 1024, cos_t, sin_t)
    ks = rope_cols(proj, 1536, 256, cos_t, sin_t)
    kw = rope_cols(proj, 2048, 256, cos_t, sin_t)
    kc, vc = proj[..., 1024:1280], proj[..., 1280:1536]
    vs, vw = proj[..., 1792:2048], proj[..., 2304:2560]

    def summaries(cache_rows, new_rows, idx):
        chunks = gather_groups(cache_rows.transpose(0, 2, 3, 1), page_table, new_rows)
        cmp = compress_chunks(chunks, p["nsa_cmp_pe"][idx], p["nsa_cmp_w1"][idx], p["nsa_cmp_b1"][idx],
                              p["nsa_cmp_w2"][idx])
        cmp = cmp.transpose(0, 2, 1, 3).reshape(b, cmp.shape[2], GLANES)
        return _pad_rows(cmp, -(-cmp.shape[1] // LANES) * LANES)

    kcmp = summaries(cache["nsa_kcmp"], kc, 0)
    vcmp = summaries(cache["nsa_vcmp"], vc, 1)
    heads = GROUP * GROUP
    gate_rows = gate[..., :3 * heads].reshape(b, t, 3, heads).transpose(0, 3, 1, 2).reshape(b, heads * t, 3)
    gate_rows = jnp.pad(gate_rows, ((0, 0), (0, 0), (0, SUBLANES - 3)))
    newp = lambda a: _pad_rows(a, NEW_PAD)
    o_rows = sample_nsa(page_table, pos0, _group_rows(proj[..., :1024]), _group_rows(q_rot), gate_rows, kcmp, vcmp,
                        flat(cache["nsa_kwin"]), flat(cache["nsa_vwin"]), newp(ks), newp(vs), newp(kw), newp(vw),
                        cache["nsa_ksel"].transpose(0, 2, 3, 1), cache["nsa_vsel"].transpose(0, 2, 3, 1))
    o = o_rows.reshape(b, heads, t, HEAD_DIM).transpose(0, 2, 1, 3).reshape(b, t, heads * HEAD_DIM)
    g4 = lambda a: a.reshape(b, t, GROUP, HEAD_DIM)
    wlen = min(WINDOW, pos0 + t)
    kw_all = jnp.concatenate([cache["nsa_kwin"], g4(kw)], axis=1)
    vw_all = jnp.concatenate([cache["nsa_vwin"], g4(vw)], axis=1)
    return o, (g4(kc), g4(vc), g4(ks), g4(vs)), (kw_all[:, -wlen:], vw_all[:, -wlen:])


DEPTH = 4
D_FF = 2816
CONV_W = 3
X_HEADS = 4
X_HDIM = 128


def _pad_cols(w, n):
    return jnp.pad(w, ((0, 0), (0, n - w.shape[1])))


def _pad_vec(v, n):
    return jnp.pad(v, (0, n - v.shape[0]))


def _to_groups(x):
    b, t, _ = x.shape
    return x.reshape(b, t, GROUP, HEAD_DIM).transpose(0, 2, 1, 3)


def prepare_weights(p):
    w = {}
    hd = FORGET_HEADS * HEAD_DIM
    w["fox_main"] = p["fox_w_in"][:, :3 * hd].astype(BF16)
    w["fox_gate"] = _pad_cols(p["fox_w_in"][:, 3 * hd:], LANES).astype(BF16)
    w["fox_bias"] = _pad_vec(p["fox_b_f"], LANES)
    w["fox_out"] = p["fox_w_out"].astype(BF16)
    w["sb_in"] = p["sb_w_in"].astype(BF16)
    w["sb_out"] = p["sb_w_out"].astype(BF16)
    nsa_main = hd + 6 * GLANES
    w["nsa_main"] = p["nsa_w_in"][:, :nsa_main].astype(BF16)
    w["nsa_gate"] = _pad_cols(p["nsa_w_in"][:, nsa_main:], LANES).astype(BF16)
    w["nsa_out"] = p["nsa_w_out"].astype(BF16)
    w["diff_in"] = p["diff_w_in"].astype(BF16)
    w["diff_out"] = p["diff_w_out"].astype(BF16)
    w["x_q"] = p["x_w_q"].astype(BF16)
    w["x_o"] = p["x_w_o"].astype(BF16)
    w["ffn_up"] = p["ffn_w_up"].astype(BF16)
    w["ffn_down"] = p["ffn_w_down"].astype(BF16)
    w["conv_pack"] = jnp.concatenate(
        [p["ffn_conv_w"], p["ffn_conv_b"][:, None, :], jnp.zeros((DEPTH, SUBLANES - CONV_W - 1, 2 * D_FF), F32)],
        axis=1)
    return w


def _diff_lambda(p, lambda_init):
    return (jnp.exp(jnp.sum(p["diff_lq1"] * p["diff_lk1"])) - jnp.exp(jnp.sum(p["diff_lq2"] * p["diff_lk2"]))
            + lambda_init)


def _lambda_init(layer):
    return 0.8 - 0.6 * math.exp(-0.3 * layer)


def _layer_tail(x, layer, p, w, mem_k, mem_v, state8):
    b, t, d = x.shape
    x = cross_block(x, p["norm_mem"][layer], w["x_q"][layer], mem_k[layer].reshape(b, -1, X_HEADS * X_HDIM),
                    mem_v[layer].reshape(b, -1, X_HEADS * X_HDIM), w["x_o"][layer])
    x2 = x.reshape(b * t, d)
    u = matmul(rmsnorm_rows(x2, p["norm_ffn"][layer], BF16), w["ffn_up"][layer]).reshape(b, t, 2 * D_FF)
    act = conv_gate(u, state8, w["conv_pack"][layer])
    x = matmul(act.reshape(b * t, D_FF), w["ffn_down"][layer], res=x2).reshape(b, t, d)
    return x, u


def prompt_trunk(x, mem_k, mem_v, p, w):
    b, t, d = x.shape
    n = b * t
    new = {}
    conv_rows = []
    cos_t, sin_t = rope_tables(jnp.arange(t))
    zero_state = jnp.zeros((b, SUBLANES, 2 * D_FF), F32)
    for layer in range(DEPTH):
        kind = layer % 4
        x2 = x.reshape(n, d)
        xn = rmsnorm_rows(x2, p["norm_mix"][layer], BF16)
        if kind == 0:
            proj = matmul(xn, w["fox_main"]).reshape(b, t, -1)
            gate = matmul(xn, w["fox_gate"]).reshape(b, t, LANES)
            logf, _, cum_pieces = logf_cumsum(gate, w["fox_bias"])
            v_t = proj[..., 2048:3072].transpose(0, 2, 1)
            o = prompt_flash("fox", proj, 0, proj, 4, v_t, cp=cum_pieces)
            x = matmul(o.reshape(n, -1), w["fox_out"], res=x2).reshape(b, t, d)
            new["fox_k"] = proj[..., 1024:2048].reshape(b, t, 16, 64)
            new["fox_v"] = proj[..., 2048:3072].reshape(b, t, 16, 64)
            new["fox_logf"] = logf[..., :FORGET_HEADS]
        elif kind == 1:
            proj = matmul(xn, w["sb_in"]).reshape(b, t, -1)
            o = prompt_sb(proj, proj[..., 2048:3072].transpose(0, 2, 1))
            x = matmul(o.reshape(n, -1), w["sb_out"], res=x2).reshape(b, t, d)
            new["sb_k"] = proj[..., 1024:2048].reshape(b, t, 16, 64)
            new["sb_v"] = proj[..., 2048:3072].reshape(b, t, 16, 64)
        elif kind == 2:
            proj = matmul(xn, w["nsa_main"]).reshape(b, t, -1)
            gate = matmul(xn, w["nsa_gate"]).reshape(b, t, LANES)
            q_rot = rope_cols(proj, 0, 1024, cos_t, sin_t)
            kc, vc = proj[..., 1024:1280], proj[..., 1280:1536]
            vs, vw = proj[..., 1792:2048], proj[..., 2304:2560]
            ks = rope_cols(proj, 1536, 256, cos_t, sin_t)
            kw = rope_cols(proj, 2048, 256, cos_t, sin_t)
            tp = -(-t // SEL_BLK) * SEL_BLK
            tpad = ((0, 0), (0, 0), (0, tp - t), (0, 0))
            kcmp = compress_rows(jnp.pad(_to_groups(kc), tpad), p["nsa_cmp_pe"][0], p["nsa_cmp_w1"][0],
                                 p["nsa_cmp_b1"][0], p["nsa_cmp_w2"][0])
            vcmp = compress_rows(jnp.pad(_to_groups(vc), tpad), p["nsa_cmp_pe"][1], p["nsa_cmp_w1"][1],
                                 p["nsa_cmp_b1"][1], p["nsa_cmp_w2"][1])
            ncp = -(-kcmp.shape[2] // LANES) * LANES
            cpad = ((0, 0), (0, 0), (0, ncp - kcmp.shape[2]), (0, 0))
            gate_t = gate[..., :3 * GROUP * GROUP].reshape(b, t, 3, GROUP, GROUP).transpose(0, 3, 2, 4, 1)
            gate_t = jnp.pad(gate_t.reshape(b, GROUP, 3 * GROUP, t), ((0, 0), (0, 0), (0, GROUP), (0, 0)))
            o = prompt_nsa(proj, q_rot, gate_t, jnp.pad(kcmp, cpad), jnp.pad(vcmp, cpad).transpose(0, 1, 3, 2),
                           _to_groups(ks), vs.transpose(0, 2, 1), _to_groups(kw), vw.transpose(0, 2, 1))
            x = matmul(o.reshape(n, -1), w["nsa_out"], res=x2).reshape(b, t, d)
            g4 = lambda a: a.reshape(b, t, GROUP, HEAD_DIM)
            new["nsa_kcmp"], new["nsa_vcmp"], new["nsa_ksel"], new["nsa_vsel"] = g4(kc), g4(vc), g4(ks), g4(vs)
            wlen = min(WINDOW, t)
            new["nsa_kwin"], new["nsa_vwin"] = g4(kw)[:, t - wlen:], g4(vw)[:, t - wlen:]
        else:
            proj = matmul(xn, w["diff_in"]).reshape(b, t, -1)
            qk = rope_cols(proj, 0, 2048, cos_t, sin_t)
            li = _lambda_init(layer)
            o = prompt_flash("diff", qk, 0, qk, 4, proj[..., 2048:3072].transpose(0, 2, 1),
                             lam=_diff_lambda(p, li), subln=p["diff_subln"], out_scale=1.0 - li)
            x = matmul(o.reshape(n, -1), w["diff_out"], res=x2).reshape(b, t, d)
            new["diff_k"] = qk[..., 1024:2048].reshape(b, t, 16, 64)
            new["diff_v"] = proj[..., 2048:3072].reshape(b, t, 8, 128)
        x, u = _layer_tail(x, layer, p, w, mem_k, mem_v, zero_state)
        conv_rows.append(u[:, t - (CONV_W - 1):])
    new["ffn_conv"] = jnp.stack(conv_rows)
    y = rmsnorm_rows(x.reshape(n, d), p["norm_final"]).reshape(b, t, d)
    return y, new


def sample_trunk(x, pos0, mem_k, mem_v, conv_state, cache, page_table, p, w):
    b, t, d = x.shape
    n = b * t
    new = {}
    conv_rows = []
    cos_t, sin_t = rope_tables(pos0 + jnp.arange(t))
    for layer in range(DEPTH):
        kind = layer % 4
        x2 = x.reshape(n, d)
        xn = rmsnorm_rows(x2, p["norm_mix"][layer], BF16)
        if kind == 0:
            proj = matmul(xn, w["fox_main"]).reshape(b, t, -1)
            gate = matmul(xn, w["fox_gate"]).reshape(b, t, LANES)
            logf, cum, _ = logf_cumsum(gate, w["fox_bias"])
            new_cum_t = _pad_rows(cum[..., :FORGET_HEADS], NEW_PAD).transpose(0, 2, 1)
            o = sample_attn("fox", page_table, proj[..., :1024], proj[..., 1024:2048], proj[..., 2048:3072],
                            cache["fox_k"], cache["fox_v"],
                            lf_cache_t=cache["fox_logf"].transpose(0, 2, 1), new_cum_t=new_cum_t)
            x = matmul(o.reshape(n, -1), w["fox_out"], res=x2).reshape(b, t, d)
            new["fox_k"] = proj[..., 1024:2048].reshape(b, t, 16, 64)
            new["fox_v"] = proj[..., 2048:3072].reshape(b, t, 16, 64)
            new["fox_logf"] = logf[..., :FORGET_HEADS]
        elif kind == 1:
            proj = matmul(xn, w["sb_in"]).reshape(b, t, -1)
            o = sample_attn("sb", page_table, proj[..., :1024], proj[..., 1024:2048], proj[..., 2048:3072],
                            cache["sb_k"], cache["sb_v"])
            x = matmul(o.reshape(n, -1), w["sb_out"], res=x2).reshape(b, t, d)
            new["sb_k"] = proj[..., 1024:2048].reshape(b, t, 16, 64)
            new["sb_v"] = proj[..., 2048:3072].reshape(b, t, 16, 64)
        elif kind == 2:
            proj = matmul(xn, w["nsa_main"]).reshape(b, t, -1)
            gate = matmul(xn, w["nsa_gate"]).reshape(b, t, LANES)
            o, rows, win = sample_nsa_mixer(proj, gate, pos0, cos_t, sin_t, cache, page_table, p)
            x = matmul(o.reshape(n, -1), w["nsa_out"], res=x2).reshape(b, t, d)
            new["nsa_kcmp"], new["nsa_vcmp"], new["nsa_ksel"], new["nsa_vsel"] = rows
            new["nsa_kwin"], new["nsa_vwin"] = win
        else:
            proj = matmul(xn, w["diff_in"]).reshape(b, t, -1)
            qk = rope_cols(proj, 0, 2048, cos_t, sin_t)
            li = _lambda_init(layer)
            o = sample_attn("diff", page_table, qk[..., :1024], qk[..., 1024:2048], proj[..., 2048:3072],
                            cache["diff_k"], cache["diff_v"],
                            lam=_diff_lambda(p, li), subln=p["diff_subln"], out_scale=1.0 - li)
            x = matmul(o.reshape(n, -1), w["diff_out"], res=x2).reshape(b, t, d)
            new["diff_k"] = qk[..., 1024:2048].reshape(b, t, 16, 64)
            new["diff_v"] = proj[..., 2048:3072].reshape(b, t, 8, 128)
        state8 = jnp.pad(conv_state[layer], ((0, 0), (SUBLANES - (CONV_W - 1), 0), (0, 0)))
        x, u = _layer_tail(x, layer, p, w, mem_k, mem_v, state8)
        ext = jnp.concatenate([conv_state[layer], u], axis=1)
        conv_rows.append(ext[:, -(CONV_W - 1):])
    new["ffn_conv"] = jnp.stack(conv_rows)
    y = rmsnorm_rows(x.reshape(n, d), p["norm_final"]).reshape(b, t, d)
    return y, new


def memory_kv(mem, mem_norm, w_kv):
    b, m, d = mem.shape
    e = X_HEADS * X_HDIM
    kv = jnp.stack([matmul(mem.reshape(b * m, d), w_kv[layer].astype(BF16), gain=mem_norm[layer])
                    for layer in range(DEPTH)])
    xk = kv[..., :e].reshape(DEPTH, b, m, X_HEADS, X_HDIM)
    xv = kv[..., e:].reshape(DEPTH, b, m, X_HEADS, X_HDIM)
    return xk, xv


def kernel(x_prompt, x_sample, cache_fox_k, cache_fox_v, cache_fox_logf, cache_sb_k, cache_sb_v,
           cache_nsa_kcmp, cache_nsa_vcmp, cache_nsa_ksel, cache_nsa_vsel, state_nsa_kwin, state_nsa_vwin,
           cache_diff_k, cache_diff_v, cache_mem_k, cache_mem_v, state_ffn_conv, page_table, mem_prompt,
           norm_mix, norm_mem, norm_ffn, norm_final, fox_w_in, fox_b_f, fox_w_out, sb_w_in, sb_w_out,
           nsa_w_in, nsa_cmp_pe, nsa_cmp_w1, nsa_cmp_b1, nsa_cmp_w2, nsa_w_out,
           diff_w_in, diff_lq1, diff_lk1, diff_lq2, diff_lk2, diff_subln, diff_w_out,
           mem_norm, x_w_q, x_w_kv, x_w_o, ffn_w_up, ffn_conv_w, ffn_conv_b, ffn_w_down):
    p = {
        "norm_mix": norm_mix, "norm_mem": norm_mem, "norm_ffn": norm_ffn, "norm_final": norm_final,
        "fox_w_in": fox_w_in, "fox_b_f": fox_b_f, "fox_w_out": fox_w_out,
        "sb_w_in": sb_w_in, "sb_w_out": sb_w_out,
        "nsa_w_in": nsa_w_in, "nsa_cmp_pe": nsa_cmp_pe, "nsa_cmp_w1": nsa_cmp_w1, "nsa_cmp_b1": nsa_cmp_b1,
        "nsa_cmp_w2": nsa_cmp_w2, "nsa_w_out": nsa_w_out,
        "diff_w_in": diff_w_in, "diff_lq1": diff_lq1, "diff_lk1": diff_lk1, "diff_lq2": diff_lq2,
        "diff_lk2": diff_lk2, "diff_subln": diff_subln, "diff_w_out": diff_w_out,
        "x_w_q": x_w_q, "x_w_o": x_w_o,
        "ffn_w_up": ffn_w_up, "ffn_conv_w": ffn_conv_w, "ffn_conv_b": ffn_conv_b, "ffn_w_down": ffn_w_down,
    }
    w = prepare_weights(p)
    mem_k_p, mem_v_p = memory_kv(mem_prompt, mem_norm, x_w_kv)
    y_prompt, sp = prompt_trunk(x_prompt, mem_k_p, mem_v_p, p, w)
    cache = {
        "fox_k": cache_fox_k, "fox_v": cache_fox_v, "fox_logf": cache_fox_logf,
        "sb_k": cache_sb_k, "sb_v": cache_sb_v,
        "nsa_kcmp": cache_nsa_kcmp, "nsa_vcmp": cache_nsa_vcmp, "nsa_ksel": cache_nsa_ksel,
        "nsa_vsel": cache_nsa_vsel, "nsa_kwin": state_nsa_kwin, "nsa_vwin": state_nsa_vwin,
        "diff_k": cache_diff_k, "diff_v": cache_diff_v,
    }
    past_len = page_table.shape[1] * cache_fox_k.shape[1]
    y_sample, ss = sample_trunk(x_sample, past_len, cache_mem_k, cache_mem_v, state_ffn_conv, cache, page_table,
                                p, w)
    return (y_prompt, y_sample,
            sp["fox_k"], sp["fox_v"], sp["fox_logf"], sp["sb_k"], sp["sb_v"],
            sp["nsa_kcmp"], sp["nsa_vcmp"], sp["nsa_ksel"], sp["nsa_vsel"], sp["nsa_kwin"], sp["nsa_vwin"],
            sp["diff_k"], sp["diff_v"], mem_k_p, mem_v_p, sp["ffn_conv"],
            ss["fox_k"], ss["fox_v"], ss["fox_logf"], ss["sb_k"], ss["sb_v"],
            ss["nsa_kcmp"], ss["nsa_vcmp"], ss["nsa_ksel"], ss["nsa_vsel"], ss["nsa_kwin"], ss["nsa_vwin"],
            ss["diff_k"], ss["diff_v"], ss["ffn_conv"])
```
